```python
import math
import jax
import jax.numpy as jnp
from jax import lax
import numpy as np

D_MODEL = 1024
BATCH = 16
SEQ = 2048
DEPTH = 1
DEC_BATCH = 128
DEC_SEQ = 8
PAST_LEN = 16384
PAGE_SIZE = 128

MIX_W = D_MODEL
SSM_W = MIX_W // 2
SSM_CH = 16
SSM_GROUPS = SSM_W // SSM_CH
SSM_P = 64
N_HEADS = 8
V_DIM = (MIX_W - SSM_W) // N_HEADS
NOPE_DIM = 64
ROPE_DIM = 32
QK_DIM = NOPE_DIM + ROPE_DIM
Q_RANK = D_MODEL // 4
KV_RANK = D_MODEL // 8
D_IN = SSM_W + Q_RANK + KV_RANK + ROPE_DIM
ROPE_BASE = 10000.0
Q_BLOCK = 128
N_GROUPS = 4
EXPERTS_PER_GROUP = 8
N_EXPERTS = N_GROUPS * EXPERTS_PER_GROUP
TOP_K = 2
D_EXPERT = D_MODEL // 4
EPS = 1e-6
DT_MIN = 0.001
DT_MAX = 0.1

kernel_name = 'hymba_s5_mla_hier_moe_step'


def rms_norm(x, g):
    xf = x.astype(jnp.float32)
    xf = xf * lax.rsqrt(jnp.mean(xf * xf, axis=-1, keepdims=True) + EPS)
    return (xf * g.astype(jnp.float32)).astype(x.dtype)


def rope_tables(pos):
    inv_freq = ROPE_BASE ** (-jnp.arange(0, ROPE_DIM, 2, dtype=jnp.float32) / ROPE_DIM)
    ang = pos.astype(jnp.float32)[:, None] * inv_freq[None, :]
    return jnp.cos(ang), jnp.sin(ang)


def apply_rope(x, cos, sin):
    xf = x.astype(jnp.float32)
    x1, x2 = xf[..., :ROPE_DIM // 2], xf[..., ROPE_DIM // 2:]
    return jnp.concatenate([x1 * cos - x2 * sin, x1 * sin + x2 * cos], axis=-1).astype(x.dtype)


def in_projection(n, w_in):
    z = jnp.einsum('btd,de->bte', n, w_in)
    o1, o2, o3 = SSM_W, SSM_W + Q_RANK, SSM_W + Q_RANK + KV_RANK
    return z[..., :o1], z[..., o1:o2], z[..., o2:o3], z[..., o3:]


def s5_discretize(p):
    f = jnp.float32
    a_re, a_im = p['ssm_a_re'].astype(f), p['ssm_a_im'].astype(f)
    dt = jnp.exp(p['ssm_log_dt'].astype(f))[:, None]
    mag, ang = jnp.exp(a_re * dt), a_im * dt
    lb_re, lb_im = mag * jnp.cos(ang), mag * jnp.sin(ang)
    den = a_re * a_re + a_im * a_im
    num_re = lb_re - 1.0
    coef_re = (num_re * a_re + lb_im * a_im) / den
    coef_im = (lb_im * a_re - num_re * a_im) / den
    b_re, b_im = p['ssm_b_re'].astype(f), p['ssm_b_im'].astype(f)
    bb_re = coef_re[..., None] * b_re - coef_im[..., None] * b_im
    bb_im = coef_re[..., None] * b_im + coef_im[..., None] * b_re
    return lb_re, lb_im, bb_re, bb_im


def complex_affine_combine(e1, e2):
    a1r, a1i, b1r, b1i = e1
    a2r, a2i, b2r, b2i = e2
    return (a2r * a1r - a2i * a1i, a2r * a1i + a2i * a1r,
            a2r * b1r - a2i * b1i + b2r, a2r * b1i + a2i * b1r + b2i)


def s5_mixer(u, h0_re, h0_im, p):
    f = jnp.float32
    nb, t = u.shape[0], u.shape[1]
    lb_re, lb_im, bb_re, bb_im = s5_discretize(p)
    uf = u.astype(f)
    ug = uf.reshape(nb, t, SSM_GROUPS, SSM_CH)
    bu_re = jnp.einsum('btgc,gpc->btgp', ug, bb_re)
    bu_im = jnp.einsum('btgc,gpc->btgp', ug, bb_im)
    if h0_re is not None:
        h0r, h0i = h0_re.astype(f), h0_im.astype(f)
        bu_re = bu_re.at[:, 0].add(lb_re * h0r - lb_im * h0i)
        bu_im = bu_im.at[:, 0].add(lb_re * h0i + lb_im * h0r)
    a_re = jnp.broadcast_to(lb_re, (1, t, SSM_GROUPS, SSM_P))
    a_im = jnp.broadcast_to(lb_im, (1, t, SSM_GROUPS, SSM_P))
    _, _, h_re, h_im = lax.associative_scan(complex_affine_combine, (a_re, a_im, bu_re, bu_im), axis=1)
    c_re, c_im = p['ssm_c_re'].astype(f), p['ssm_c_im'].astype(f)
    y = jnp.einsum('btgp,gcp->btgc', h_re, c_re) - jnp.einsum('btgp,gcp->btgc', h_im, c_im)
    y = y.reshape(nb, t, SSM_W) + p['ssm_d'].astype(f) * uf
    y = jax.nn.gelu(y)
    out = y * jax.nn.sigmoid(y @ p['ssm_w_glu'].astype(f) + p['ssm_b_glu'].astype(f))
    return out.astype(u.dtype), h_re[:, -1], h_im[:, -1]


def mla_queries(cq, pos, p):
    q = jnp.einsum('btr,rhd->bthd', rms_norm(cq, p['q_norm_g']), p['w_uq'])
    cos, sin = rope_tables(pos)
    q = jnp.concatenate([q[..., :NOPE_DIM], apply_rope(q[..., NOPE_DIM:], cos[:, None], sin[:, None])], axis=-1)
    return rms_norm(q, p['qk_norm_q']) * (1.0 / math.sqrt(QK_DIM))


def mla_latent(ckv_raw, kr_raw, pos, p):
    cos, sin = rope_tables(pos)
    return rms_norm(ckv_raw, p['kv_norm_g']), apply_rope(kr_raw, cos, sin)


def mla_keys(ckv, kr, w_uk, qk_norm_k):
    k_nope = jnp.einsum('btr,rhd->bthd', ckv, w_uk)
    k_rope = jnp.broadcast_to(kr[:, :, None, :], k_nope.shape[:-1] + (ROPE_DIM,))
    return rms_norm(jnp.concatenate([k_nope, k_rope.astype(k_nope.dtype)], axis=-1), qk_norm_k)


def mla_prompt_attention(q, ckv, kr, p):
    nb, t = q.shape[0], q.shape[1]
    k = mla_keys(ckv, kr, p['w_uk'], p['qk_norm_k'])
    v = jnp.einsum('btr,rhd->bthd', ckv, p['w_uv'])
    n_blocks = t // Q_BLOCK
    qb = q.reshape(nb, n_blocks, Q_BLOCK, N_HEADS, QK_DIM).swapaxes(0, 1)
    kpos = jnp.arange(t)

    def block(args):
        q_blk, blk = args
        s = jnp.einsum('bqhd,bkhd->bhqk', q_blk, k).astype(jnp.float32)
        qpos = blk * Q_BLOCK + jnp.arange(Q_BLOCK)
        s = jnp.where(kpos[None, :] <= qpos[:, None], s, -jnp.inf)
        pr = jax.nn.softmax(s, axis=-1)
        return jnp.einsum('bhqk,bkhd->bqhd', pr.astype(v.dtype), v)

    o = lax.map(block, (qb, jnp.arange(n_blocks)))
    return o.swapaxes(0, 1).reshape(nb, t, N_HEADS * V_DIM)


def mla_sample_attention(q, ckv_new, kr_new, cache_ckv, cache_krope, page_table, p):
    f = jnp.float32
    nb, t = q.shape[0], q.shape[1]

    def page_step(carry, pages):
        m, l, acc = carry
        ckv_p = cache_ckv[pages]
        kr_p = cache_krope[pages]
        k_p = mla_keys(ckv_p, kr_p, p['w_uk'], p['qk_norm_k'])
        s = jnp.einsum('bthd,bkhd->bhtk', q, k_p).astype(f)
        m_new = jnp.maximum(m, s.max(axis=-1))
        corr = jnp.exp(m - m_new)
        pr = jnp.exp(s - m_new[..., None])
        l = l * corr + pr.sum(axis=-1)
        acc = acc * corr[..., None] + jnp.einsum('bhtk,bkr->bhtr', pr, ckv_p.astype(f))
        return (m_new, l, acc), None

    init = (jnp.full((nb, N_HEADS, t), -jnp.inf, f),
            jnp.zeros((nb, N_HEADS, t), f),
            jnp.zeros((nb, N_HEADS, t, KV_RANK), f))
    (m, l, acc), _ = lax.scan(page_step, init, page_table.T)
    k_new = mla_keys(ckv_new, kr_new, p['w_uk'], p['qk_norm_k'])
    s = jnp.einsum('bthd,bkhd->bhtk', q, k_new).astype(f)
    s = jnp.where(jnp.tril(jnp.ones((t, t), dtype=bool)), s, -jnp.inf)
    m_f = jnp.maximum(m, s.max(axis=-1))
    corr = jnp.exp(m - m_f)
    pr = jnp.exp(s - m_f[..., None])
    l = l * corr + pr.sum(axis=-1)
    acc = acc * corr[..., None] + jnp.einsum('bhtk,bkr->bhtr', pr, ckv_new.astype(f))
    o_lat = acc / l[..., None]
    o = jnp.einsum('bhtr,rhd->bthd', o_lat, p['w_uv'].astype(f))
    return o.reshape(nb, t, N_HEADS * V_DIM).astype(q.dtype)


def merge_heads(ssm_out, attn_out, p):
    y = jnp.concatenate([rms_norm(ssm_out, p['out_norm_ssm']), rms_norm(attn_out, p['out_norm_attn'])], axis=-1)
    return jnp.einsum('btm,md->btd', y, p['w_out'])


def hier_moe(x, p):
    f = jnp.float32

    def per_seq(xs):
        t = xs.shape[0]
        lg = (xs @ p['w_router_group'] + p['b_router_group']).astype(f)
        g = jnp.argmax(lg, axis=-1)
        p_sel = jax.nn.softmax(lg, axis=-1).max(axis=-1, keepdims=True)
        le = (xs @ p['w_router_expert'] + p['b_router_expert']).astype(f)
        le = le.reshape(t, N_GROUPS, EXPERTS_PER_GROUP)
        le_g = jnp.einsum('tge,tg->te', le, jax.nn.one_hot(g, N_GROUPS, dtype=f))
        top_v, top_i = lax.top_k(le_g, TOP_K)
        w = jax.nn.softmax(top_v, axis=-1) * p_sel
        eidx = g[:, None] * EXPERTS_PER_GROUP + top_i
        gate = jnp.einsum('tk,tke->te', w, jax.nn.one_hot(eidx, N_EXPERTS, dtype=f))
        hg = jnp.einsum('td,edf->tef', xs, p['w_gate'])
        hu = jnp.einsum('td,edf->tef', xs, p['w_up'])
        h = jax.nn.silu(hg) * hu * gate[:, :, None].astype(hg.dtype)
        return jnp.einsum('tef,efd->td', h, p['w_down'])

    return lax.map(per_seq, x)


def decoder_layer(x_p, x_s, cache_ckv, cache_krope, st_re, st_im, page_table, p):
    nb_p, t_p = x_p.shape[0], x_p.shape[1]
    pos_p = jnp.arange(t_p)
    u, cq, ckv_raw, kr_raw = in_projection(rms_norm(x_p, p['norm1_g']), p['w_in'])
    ssm_p, hp_re, hp_im = s5_mixer(u, None, None, p)
    q = mla_queries(cq, pos_p, p)
    ckv_p, kr_p = mla_latent(ckv_raw, kr_raw, pos_p, p)
    att_p = mla_prompt_attention(q, ckv_p, kr_p, p)
    h_p = x_p + merge_heads(ssm_p, att_p, p)
    y_p = h_p + hier_moe(rms_norm(h_p, p['norm2_g']), p)
    pos_s = PAST_LEN + jnp.arange(x_s.shape[1])
    u, cq, ckv_raw, kr_raw = in_projection(rms_norm(x_s, p['norm1_g']), p['w_in'])
    ssm_s, hs_re, hs_im = s5_mixer(u, st_re, st_im, p)
    q = mla_queries(cq, pos_s, p)
    ckv_s, kr_s = mla_latent(ckv_raw, kr_raw, pos_s, p)
    att_s = mla_sample_attention(q, ckv_s, kr_s, cache_ckv, cache_krope, page_table, p)
    h_s = x_s + merge_heads(ssm_s, att_s, p)
    y_s = h_s + hier_moe(rms_norm(h_s, p['norm2_g']), p)
    n_pg = t_p // PAGE_SIZE
    return (y_p, y_s,
            ckv_p.reshape(nb_p, n_pg, PAGE_SIZE, KV_RANK), kr_p.reshape(nb_p, n_pg, PAGE_SIZE, ROPE_DIM),
            hp_re, hp_im, ckv_s, kr_s, hs_re, hs_im)


def setup_inputs(seed: int = 0) -> dict:
    key = jax.random.key(seed)
    ks = jax.random.split(key, 40)
    f = jnp.float32

    def nrm(i, shape, scale):
        return scale * jax.random.normal(ks[i], shape, f)

    def gain(i, n):
        return 1.0 + 0.01 * jax.random.normal(ks[i], (DEPTH, n), f)

    n_pages = PAST_LEN // PAGE_SIZE
    n_used = DEC_BATCH * n_pages
    n_pool = n_used + n_used // 4
    page_table = jax.random.permutation(ks[0], n_pool)[:n_used].reshape(DEC_BATCH, n_pages).astype(jnp.int32)
    n_idx = jnp.arange(SSM_P, dtype=f)
    gps = (DEPTH, SSM_GROUPS, SSM_P)
    return {
        'x_prompt': nrm(1, (BATCH, SEQ, D_MODEL), 1.0),
        'x_sample': nrm(2, (DEC_BATCH, DEC_SEQ, D_MODEL), 1.0),
        'cache_ckv': nrm(3, (DEPTH, n_pool, PAGE_SIZE, KV_RANK), 1.0),
        'cache_krope': nrm(4, (DEPTH, n_pool, PAGE_SIZE, ROPE_DIM), 1.0),
        'state_ssm_re': nrm(5, (DEPTH, DEC_BATCH, SSM_GROUPS, SSM_P), 0.5),
        'state_ssm_im': nrm(6, (DEPTH, DEC_BATCH, SSM_GROUPS, SSM_P), 0.5),
        'page_table': page_table,
        'norm1_g': gain(7, D_MODEL),
        'w_in': nrm(8, (DEPTH, D_MODEL, D_IN), D_MODEL ** -0.5),
        'ssm_a_re': -0.5 * (1.0 + nrm(9, gps, 0.01)),
        'ssm_a_im': jnp.pi * (n_idx + nrm(10, gps, 0.01)),
        'ssm_log_dt': jax.random.uniform(ks[11], (DEPTH, SSM_GROUPS), f, math.log(DT_MIN), math.log(DT_MAX)),
        'ssm_b_re': nrm(12, (DEPTH, SSM_GROUPS, SSM_P, SSM_CH), (2 * SSM_CH) ** -0.5),
        'ssm_b_im': nrm(13, (DEPTH, SSM_GROUPS, SSM_P, SSM_CH), (2 * SSM_CH) ** -0.5),
        'ssm_c_re': nrm(14, (DEPTH, SSM_GROUPS, SSM_CH, SSM_P), (2 * SSM_P) ** -0.5),
        'ssm_c_im': nrm(15, (DEPTH, SSM_GROUPS, SSM_CH, SSM_P), (2 * SSM_P) ** -0.5),
        'ssm_d': nrm(16, (DEPTH, SSM_W), 1.0),
        'ssm_w_glu': nrm(17, (DEPTH, SSM_W, SSM_W), SSM_W ** -0.5),
        'ssm_b_glu': nrm(18, (DEPTH, SSM_W), 0.01),
        'q_norm_g': gain(19, Q_RANK),
        'w_uq': nrm(20, (DEPTH, Q_RANK, N_HEADS, QK_DIM), Q_RANK ** -0.5),
        'kv_norm_g': gain(21, KV_RANK),
        'w_uk': nrm(22, (DEPTH, KV_RANK, N_HEADS, NOPE_DIM), KV_RANK ** -0.5),
        'w_uv': nrm(23, (DEPTH, KV_RANK, N_HEADS, V_DIM), KV_RANK ** -0.5),
        'qk_norm_q': gain(24, QK_DIM),
        'qk_norm_k': gain(25, QK_DIM),
        'out_norm_ssm': gain(26, SSM_W),
        'out_norm_attn': gain(27, N_HEADS * V_DIM),
        'w_out': nrm(28, (DEPTH, MIX_W, D_MODEL), MIX_W ** -0.5),
        'norm2_g': gain(29, D_MODEL),
        'w_router_group': nrm(30, (DEPTH, D_MODEL, N_GROUPS), D_MODEL ** -0.5),
        'b_router_group': nrm(31, (DEPTH, N_GROUPS), 0.01),
        'w_router_expert': nrm(32, (DEPTH, D_MODEL, N_EXPERTS), D_MODEL ** -0.5),
        'b_router_expert': nrm(33, (DEPTH, N_EXPERTS), 0.01),
        'w_gate': nrm(34, (DEPTH, N_EXPERTS, D_MODEL, D_EXPERT), D_MODEL ** -0.5),
        'w_up': nrm(35, (DEPTH, N_EXPERTS, D_MODEL, D_EXPERT), D_MODEL ** -0.5),
        'w_down': nrm(36, (DEPTH, N_EXPERTS, D_EXPERT, D_MODEL), D_EXPERT ** -0.5),
    }


def reference(x_prompt, x_sample, cache_ckv, cache_krope, state_ssm_re, state_ssm_im, page_table,
              norm1_g, w_in, ssm_a_re, ssm_a_im, ssm_log_dt, ssm_b_re, ssm_b_im, ssm_c_re, ssm_c_im,
              ssm_d, ssm_w_glu, ssm_b_glu, q_norm_g, w_uq, kv_norm_g, w_uk, w_uv, qk_norm_q, qk_norm_k,
              out_norm_ssm, out_norm_attn, w_out, norm2_g, w_router_group, b_router_group,
              w_router_expert, b_router_expert, w_gate, w_up, w_down):
    h_p, h_s = x_prompt, x_sample
    ckv_p_l, kr_p_l, re_p_l, im_p_l = [], [], [], []
    ckv_s_l, kr_s_l, re_s_l, im_s_l = [], [], [], []
    for layer in range(DEPTH):
        p = {
            'norm1_g': norm1_g[layer], 'w_in': w_in[layer],
            'ssm_a_re': ssm_a_re[layer], 'ssm_a_im': ssm_a_im[layer], 'ssm_log_dt': ssm_log_dt[layer],
            'ssm_b_re': ssm_b_re[layer], 'ssm_b_im': ssm_b_im[layer],
            'ssm_c_re': ssm_c_re[layer], 'ssm_c_im': ssm_c_im[layer], 'ssm_d': ssm_d[layer],
            'ssm_w_glu': ssm_w_glu[layer], 'ssm_b_glu': ssm_b_glu[layer],
            'q_norm_g': q_norm_g[layer], 'w_uq': w_uq[layer], 'kv_norm_g': kv_norm_g[layer],
            'w_uk': w_uk[layer], 'w_uv': w_uv[layer],
            'qk_norm_q': qk_norm_q[layer], 'qk_norm_k': qk_norm_k[layer],
            'out_norm_ssm': out_norm_ssm[layer], 'out_norm_attn': out_norm_attn[layer], 'w_out': w_out[layer],
            'norm2_g': norm2_g[layer],
            'w_router_group': w_router_group[layer], 'b_router_group': b_router_group[layer],
            'w_router_expert': w_router_expert[layer], 'b_router_expert': b_router_expert[layer],
            'w_gate': w_gate[layer], 'w_up': w_up[layer], 'w_down': w_down[layer],
        }
        (h_p, h_s, ckv_p, kr_p, hp_re, hp_im, ckv_s, kr_s, hs_re, hs_im) = decoder_layer(
            h_p, h_s, cache_ckv[layer], cache_krope[layer], state_ssm_re[layer], state_ssm_im[layer],
            page_table, p)
        ckv_p_l.append(ckv_p); kr_p_l.append(kr_p); re_p_l.append(hp_re); im_p_l.append(hp_im)
        ckv_s_l.append(ckv_s); kr_s_l.append(kr_s); re_s_l.append(hs_re); im_s_l.append(hs_im)
    return (h_p, h_s,
            jnp.stack(ckv_p_l), jnp.stack(kr_p_l), jnp.stack(re_p_l), jnp.stack(im_p_l),
            jnp.stack(ckv_s_l), jnp.stack(kr_s_l), jnp.stack(re_s_l), jnp.stack(im_s_l))
```

```python
import functools
import math

import jax
import jax.numpy as jnp
from jax import lax
from jax.experimental import pallas as pl
from jax.experimental.pallas import tpu as pltpu

F32 = jnp.float32
BF16 = jnp.bfloat16
HIGHEST = lax.Precision.HIGHEST

LANES = 128
SUBLANES = 8
VMEM_LIMIT = 56 * 1024 * 1024

EPS = 1e-6
ROPE_BASE = 10000.0
PAGE = 128
SSM_CH = 16
SSM_P = 64
N_HEADS = 8
NOPE = 64
ROPE = 32
QK = NOPE + ROPE
V_DIM = 64
N_GROUPS = 4
EXP_PER_GROUP = 8
N_EXPERTS = N_GROUPS * EXP_PER_GROUP

NT_DIMS = (((1,), (1,)), ((), ()))


def _cparams(sem, vmem=VMEM_LIMIT):
    return pltpu.CompilerParams(dimension_semantics=sem, vmem_limit_bytes=vmem)


def _rms(x, g):
    return x * lax.rsqrt(jnp.mean(x * x, axis=-1, keepdims=True) + EPS) * g


def _rms_qk(x, g):
    return x * lax.rsqrt(jnp.sum(x * x, axis=-1, keepdims=True) * (1.0 / QK) + EPS) * g


def _rope(x, cos, sinp, sinm):
    return x * cos + pltpu.roll(x, 16, 1) * sinp + pltpu.roll(x, LANES - 16, 1) * sinm


def _rope_tables(pos):
    inv_freq = ROPE_BASE ** (-jnp.arange(0, ROPE, 2, dtype=F32) / ROPE)
    ang = pos.astype(F32)[:, None] * inv_freq[None, :]
    c, s = jnp.cos(ang), jnp.sin(ang)
    t = pos.shape[0]
    z = lambda n: jnp.zeros((t, n), F32)
    cosf = jnp.concatenate([jnp.ones((t, NOPE), F32), c, c, z(LANES - QK)], axis=1)
    sinp = jnp.concatenate([z(NOPE + ROPE // 2), s, z(LANES - QK)], axis=1)
    sinm = jnp.concatenate([z(NOPE), -s, z(LANES - QK + ROPE // 2)], axis=1)
    return cosf, sinp, sinm


def _in_proj_kernel(x_ref, g1_ref, w_ref, gq_ref, gkv_ref, cos_ref, sinp_ref, sinm_ref,
                    u_ref, cq_ref, ckv_ref, kr_ref, *, ssm_w, q_rank, kv_rank):
    n = _rms(x_ref[...], g1_ref[...])
    z = jnp.dot(n.astype(BF16), w_ref[...], preferred_element_type=F32)
    o1, o2, o3 = ssm_w, ssm_w + q_rank, ssm_w + q_rank + kv_rank
    u_ref[...] = z[:, :o1].astype(u_ref.dtype)
    cq_ref[...] = _rms(z[:, o1:o2], gq_ref[...]).astype(cq_ref.dtype)
    ckv_ref[...] = _rms(z[:, o2:o3], gkv_ref[...])
    kr_ref[...] = _rope(z[:, o3:o3 + LANES], cos_ref[...], sinp_ref[...], sinm_ref[...])


def _in_proj(x, g1, w_pad, gq, gkv, tables, tm, ssm_w, q_rank, kv_rank):
    n, d = x.shape
    period = tables[0].shape[0]
    nper = period // tm
    row = lambda i: (i, 0)
    const = lambda i: (0, 0)
    tab = pl.BlockSpec((tm, LANES), lambda i: (i % nper, 0))
    return pl.pallas_call(
        functools.partial(_in_proj_kernel, ssm_w=ssm_w, q_rank=q_rank, kv_rank=kv_rank),
        grid=(n // tm,),
        in_specs=[pl.BlockSpec((tm, d), row), pl.BlockSpec((1, d), const),
                  pl.BlockSpec(w_pad.shape, const), pl.BlockSpec((1, q_rank), const),
                  pl.BlockSpec((1, kv_rank), const), tab, tab, tab],
        out_specs=[pl.BlockSpec((tm, ssm_w), row), pl.BlockSpec((tm, q_rank), row),
                   pl.BlockSpec((tm, kv_rank), row), pl.BlockSpec((tm, LANES), row)],
        out_shape=[jax.ShapeDtypeStruct((n, ssm_w), BF16), jax.ShapeDtypeStruct((n, q_rank), BF16),
                   jax.ShapeDtypeStruct((n, kv_rank), F32), jax.ShapeDtypeStruct((n, LANES), F32)],
        compiler_params=_cparams(("parallel",)),
    )(x, g1, w_pad, gq, gkv, *tables)


def _s5_weights(a_re, a_im, log_dt, b_re, b_im, c_re, c_im, d, lc):
    g, p = a_re.shape
    dt = jnp.exp(log_dt)[:, None]
    den = a_re * a_re + a_im * a_im
    mag, ang = jnp.exp(a_re * dt), a_im * dt
    lb_re, lb_im = mag * jnp.cos(ang), mag * jnp.sin(ang)
    num_re = lb_re - 1.0
    coef_re = (num_re * a_re + lb_im * a_im) / den
    coef_im = (lb_im * a_re - num_re * a_im) / den
    bb_re = coef_re[..., None] * b_re - coef_im[..., None] * b_im
    bb_im = coef_re[..., None] * b_im + coef_im[..., None] * b_re

    def lam_pow(k):
        kk = k.astype(F32)[:, None, None]
        m = jnp.exp(a_re * dt * kk)
        return m * jnp.cos(ang * kk), m * jnp.sin(ang * kk)

    steps = jnp.arange(lc)
    pr, pi = lam_pow(lc - 1 - steps)
    bz_re = pr[..., None] * bb_re[None] - pi[..., None] * bb_im[None]
    bz_im = pr[..., None] * bb_im[None] + pi[..., None] * bb_re[None]
    bz = jnp.concatenate([bz_re, bz_im], axis=2)
    bz = bz.transpose(1, 0, 3, 2).reshape(g, lc * SSM_CH, 2 * p)
    qr, qi = lam_pow(steps + 1)
    cl_re = c_re[None] * qr[:, :, None, :] - c_im[None] * qi[:, :, None, :]
    cl_im = c_re[None] * qi[:, :, None, :] + c_im[None] * qr[:, :, None, :]
    cz = jnp.concatenate([cl_re, -cl_im], axis=3)
    cz = cz.transpose(1, 3, 0, 2).reshape(g, 2 * p, lc * SSM_CH)
    tr, ti = lam_pow(steps)
    cb_re = jnp.einsum('gcp,kgp,gpd->gkcd', c_re, tr, bb_re, precision=HIGHEST)
    cb_re -= jnp.einsum('gcp,kgp,gpd->gkcd', c_re, ti, bb_im, precision=HIGHEST)
    cb_re -= jnp.einsum('gcp,kgp,gpd->gkcd', c_im, tr, bb_im, precision=HIGHEST)
    cb_re -= jnp.einsum('gcp,kgp,gpd->gkcd', c_im, ti, bb_re, precision=HIGHEST)
    tau = steps[None, :] - steps[:, None]
    ksel = cb_re[:, jnp.clip(tau, 0, lc - 1)]
    ksel = jnp.where((tau >= 0)[None, :, :, None, None], ksel, 0.0)
    tz = ksel.transpose(0, 1, 4, 2, 3).reshape(g, lc * SSM_CH, lc * SSM_CH)
    lr, li = lam_pow(jnp.array([lc]))
    lam_a = jnp.concatenate([lr[0], lr[0]], axis=-1)[:, None, :]
    lam_b = jnp.concatenate([-li[0], li[0]], axis=-1)[:, None, :]
    dvec = jnp.tile(d.reshape(g, 1, SSM_CH), (1, lc, 1)).reshape(g, 1, lc * SSM_CH)
    return tz.astype(BF16), bz.astype(BF16), cz.astype(BF16), lam_a, lam_b, dvec


def _s5_kernel(x_ref, tz_ref, bz_ref, cz_ref, la_ref, lb_ref, d_ref, h0_ref, y_ref, hf_ref, s_scr, *, nk, nb):
    x = x_ref[0]
    s_scr[...] = jnp.dot(x, bz_ref[0], preferred_element_type=F32)
    la, lb = la_ref[0], lb_ref[0]

    def step(k, h):
        rows = pl.ds(pl.multiple_of(k * nb, nb), nb)
        s_k = s_scr[rows, :]
        s_scr[rows, :] = h
        return la * h + lb * pltpu.roll(h, SSM_P, 1) + s_k

    hf_ref[0] = lax.fori_loop(0, nk, step, h0_ref[0])
    y = jnp.dot(x, tz_ref[0], preferred_element_type=F32)
    y += jnp.dot(s_scr[...].astype(BF16), cz_ref[0], preferred_element_type=F32)
    y_ref[0] = (y + d_ref[0] * x.astype(F32)).astype(y_ref.dtype)


def _s5(u, h0, weights, nb, t, lc):
    tz, bz, cz, lam_a, lam_b, dvec = weights
    g = tz.shape[0]
    nk = t // lc
    w = lc * SSM_CH
    xg = u.reshape(nb, nk, lc, g, SSM_CH).transpose(3, 1, 0, 2, 4).reshape(g, nk * nb, w)
    grp = lambda i: (i, 0, 0)
    y, hf = pl.pallas_call(
        functools.partial(_s5_kernel, nk=nk, nb=nb),
        grid=(g,),
        in_specs=[pl.BlockSpec((1, nk * nb, w), grp), pl.BlockSpec((1, w, w), grp),
                  pl.BlockSpec((1, w, 2 * SSM_P), grp), pl.BlockSpec((1, 2 * SSM_P, w), grp),
                  pl.BlockSpec((1, 1, 2 * SSM_P), grp), pl.BlockSpec((1, 1, 2 * SSM_P), grp),
                  pl.BlockSpec((1, 1, w), grp), pl.BlockSpec((1, nb, 2 * SSM_P), grp)],
        out_specs=[pl.BlockSpec((1, nk * nb, w), grp), pl.BlockSpec((1, nb, 2 * SSM_P), grp)],
        out_shape=[jax.ShapeDtypeStruct((g, nk * nb, w), BF16), jax.ShapeDtypeStruct((g, nb, 2 * SSM_P), F32)],
        scratch_shapes=[pltpu.VMEM((nk * nb, 2 * SSM_P), F32)],
        compiler_params=_cparams(("parallel",)),
    )(xg, tz, bz, cz, lam_a, lam_b, dvec, h0)
    y = y.reshape(g, nk, nb, lc, SSM_CH).transpose(2, 1, 3, 0, 4).reshape(nb * t, g * SSM_CH)
    return y, hf


def _attn_prompt_kernel(cq_ref, ckv_ref, krp_ref, cos_ref, sinp_ref, sinm_ref, wq_ref, wk_ref, wv_ref,
                        gq_ref, gk_ref, o_ref, k_scr, v_scr, *, tq, tk):
    qi = pl.program_id(1)

    @pl.when(qi == 0)
    def _():
        c = ckv_ref[...].astype(BF16)
        krp = krp_ref[...]
        for h in range(N_HEADS):
            kh = jnp.dot(c, wk_ref[h], preferred_element_type=F32) + krp
            k_scr[h] = _rms_qk(kh, gk_ref[...]).astype(BF16)
        for hp in range(N_HEADS // 2):
            v_scr[hp] = jnp.dot(c, wv_ref[hp], preferred_element_type=F32).astype(BF16)

    cq = cq_ref[...]
    cos, sinp, sinm = cos_ref[...], sinp_ref[...], sinm_ref[...]
    row = qi * tq + lax.broadcasted_iota(jnp.int32, (tq, tk), 0)
    col = lax.broadcasted_iota(jnp.int32, (tq, tk), 1)
    lane = lax.broadcasted_iota(jnp.int32, (tq, LANES), 1)
    n_full = qi * (tq // tk)

    for hp in range(N_HEADS // 2):
        outs = []
        for h in (2 * hp, 2 * hp + 1):
            q = _rope(jnp.dot(cq, wq_ref[h], preferred_element_type=F32), cos, sinp, sinm)
            qb = (_rms_qk(q, gq_ref[...]) * (1.0 / math.sqrt(QK))).astype(BF16)

            def kv_step(j, carry, masked, h=h, hp=hp, qb=qb):
                m, l, acc = carry
                ks = pl.ds(pl.multiple_of(j * tk, tk), tk)
                s = lax.dot_general(qb, k_scr[h, ks, :], NT_DIMS, preferred_element_type=F32)
                if masked:
                    s = jnp.where(col + j * tk <= row, s, -jnp.inf)
                m_new = jnp.maximum(m, jnp.max(s, axis=-1, keepdims=True))
                p = jnp.exp(s - m_new)
                corr = jnp.exp(m - m_new)
                l = l * corr + jnp.sum(p, axis=-1, keepdims=True)
                acc = acc * corr + jnp.dot(p.astype(BF16), v_scr[hp, ks, :], preferred_element_type=F32)
                return m_new, l, acc

            carry = (jnp.full((tq, 1), -jnp.inf, F32), jnp.zeros((tq, 1), F32), jnp.zeros((tq, LANES), F32))
            carry = lax.fori_loop(0, n_full, functools.partial(kv_step, masked=False), carry)
            for jj in range(tq // tk):
                carry = kv_step(n_full + jj, carry, True)
            outs.append(carry[2] / carry[1])
        o_ref[:, hp * LANES:(hp + 1) * LANES] = jnp.where(lane < V_DIM, outs[0], outs[1])


def _attn_prompt(cq, ckv, krp, tables, wq, wk, wvp, gq, gk, nb, t, tq, tk):
    n = cq.shape[0]
    nq = t // tq
    qrow = lambda b, i: (b * nq + i, 0)
    seq = lambda b, i: (b, 0)
    tab = pl.BlockSpec((tq, LANES), lambda b, i: (i, 0))
    c3 = lambda b, i: (0, 0, 0)
    c2 = lambda b, i: (0, 0)
    return pl.pallas_call(
        functools.partial(_attn_prompt_kernel, tq=tq, tk=tk),
        grid=(nb, nq),
        in_specs=[pl.BlockSpec((tq, cq.shape[1]), qrow), pl.BlockSpec((t, ckv.shape[1]), seq),
                  pl.BlockSpec((t, LANES), seq), tab, tab, tab,
                  pl.BlockSpec(wq.shape, c3), pl.BlockSpec(wk.shape, c3), pl.BlockSpec(wvp.shape, c3),
                  pl.BlockSpec((1, LANES), c2), pl.BlockSpec((1, LANES), c2)],
        out_specs=pl.BlockSpec((tq, N_HEADS * V_DIM), qrow),
        out_shape=jax.ShapeDtypeStruct((n, N_HEADS * V_DIM), F32),
        scratch_shapes=[pltpu.VMEM((N_HEADS, t, LANES), BF16), pltpu.VMEM((N_HEADS // 2, t, LANES), BF16)],
        compiler_params=_cparams(("parallel", "arbitrary")),
    )(cq, ckv, krp, *tables, wq, wk, wvp, gq, gk)


def _q_sample_kernel(cq_ref, cos_ref, sinp_ref, sinm_ref, wq_ref, wukt_ref, sel_ref, gq_ref, gk_ref,
                     qa_ref, qr_ref):
    cq = cq_ref[...]
    for h in range(N_HEADS):
        q = _rope(jnp.dot(cq, wq_ref[h], preferred_element_type=F32), cos_ref[...], sinp_ref[...], sinm_ref[...])
        q = _rms_qk(q, gq_ref[...]) * (1.0 / math.sqrt(QK))
        qk = (q * gk_ref[...]).astype(BF16)
        qa_ref[h] = jnp.dot(qk, wukt_ref[h], preferred_element_type=F32).astype(BF16)
        qr_ref[h] = jnp.dot(qk, sel_ref[...], preferred_element_type=F32).astype(BF16)


def _q_sample(cq, tables, wq, wukt, sel, gq, gk):
    n = cq.shape[0]
    return pl.pallas_call(
        _q_sample_kernel,
        out_shape=[jax.ShapeDtypeStruct((N_HEADS, n, LANES), BF16), jax.ShapeDtypeStruct((N_HEADS, n, ROPE), BF16)],
        compiler_params=pltpu.CompilerParams(vmem_limit_bytes=VMEM_LIMIT),
    )(cq, *tables, wq, wukt, sel, gq, gk)


def _attn_sample_kernel(pt_ref, qa_ref, qr_ref, cnew_ref, krnew_ref, wukt_ref, wv_ref, ckv_hbm, kr_hbm,
                        o_ref, cbuf, kbuf, sems, m_scr, l_scr, acc_scr, *, pc, nc, npg, t_dec):
    b, c = pl.program_id(0), pl.program_id(1)
    g = b * nc + c
    slot = g % 2
    nkeys = pc * PAGE
    rows = N_HEADS * t_dec

    def page_copies(step, slot_):
        base = step * pc
        cps = []
        for j in range(pc):
            pg = pt_ref[base + j]
            dst = pl.ds(j * PAGE, PAGE)
            cps.append(pltpu.make_async_copy(ckv_hbm.at[pg], cbuf.at[slot_, dst, :], sems.at[0, slot_]))
            cps.append(pltpu.make_async_copy(kr_hbm.at[pg], kbuf.at[slot_, dst, :], sems.at[1, slot_]))
        return cps

    @pl.when(g == 0)
    def _():
        for cp in page_copies(g, slot):
            cp.start()

    @pl.when(g + 1 < pl.num_programs(0) * nc)
    def _():
        for cp in page_copies(g + 1, 1 - slot):
            cp.start()

    @pl.when(c == 0)
    def _():
        m_scr[...] = jnp.full(m_scr.shape, -jnp.inf, F32)
        l_scr[...] = jnp.zeros(l_scr.shape, F32)
        acc_scr[...] = jnp.zeros(acc_scr.shape, F32)

    qa = qa_ref[...].reshape(rows, LANES)
    qr = qr_ref[...].reshape(rows, ROPE)
    ones = jnp.ones((SUBLANES, ROPE), F32)

    def attend(cf, krf, mask):
        nk = cf.shape[0]
        cb = cf.astype(BF16)
        knt = lax.dot_general(wukt_ref[...], cb, NT_DIMS, preferred_element_type=F32)
        n2 = jnp.sum((knt * knt).reshape(N_HEADS, NOPE, nk), axis=1)
        kr2 = lax.dot_general(ones, krf * krf, NT_DIMS, precision=HIGHEST, preferred_element_type=F32)
        rinv = lax.rsqrt((n2 + kr2) * (1.0 / QK) + EPS)
        s = lax.dot_general(qa, cb, NT_DIMS, preferred_element_type=F32)
        s += lax.dot_general(qr, krf.astype(BF16), NT_DIMS, preferred_element_type=F32)
        s = (s.reshape(N_HEADS, t_dec, nk) * rinv[:, None, :]).reshape(rows, nk)
        if mask is not None:
            s = jnp.where(mask, s, -jnp.inf)
        m = m_scr[...]
        m_new = jnp.maximum(m, jnp.max(s, axis=-1, keepdims=True))
        p = jnp.exp(s - m_new)
        corr = jnp.exp(m - m_new)
        l_scr[...] = l_scr[...] * corr + jnp.sum(p, axis=-1, keepdims=True)
        acc_scr[...] = acc_scr[...] * corr + jnp.dot(p.astype(BF16), cb, preferred_element_type=F32)
        m_scr[...] = m_new

    for cp in page_copies(g, slot):
        cp.wait()
    attend(cbuf[slot], kbuf[slot], None)

    @pl.when(c == nc - 1)
    def _():
        key = lax.broadcasted_iota(jnp.int32, (rows, PAGE), 1)
        step = lax.broadcasted_iota(jnp.int32, (rows, PAGE), 0) % t_dec
        attend(cnew_ref[0], krnew_ref[0], key <= step)
        o_lat = (acc_scr[...] / l_scr[...]).astype(BF16)
        lane = lax.broadcasted_iota(jnp.int32, (t_dec, LANES), 1)
        for hp in range(N_HEADS // 2):
            lo = jnp.dot(o_lat[(2 * hp) * t_dec:(2 * hp + 1) * t_dec], wv_ref[hp], preferred_element_type=F32)
            hi = jnp.dot(o_lat[(2 * hp + 1) * t_dec:(2 * hp + 2) * t_dec], wv_ref[hp], preferred_element_type=F32)
            o_ref[:, hp * LANES:(hp + 1) * LANES] = jnp.where(lane < V_DIM, lo, hi)


def _attn_sample(page_table, qa, qr, cnew_pad, krnew_pad, wukt_all, wvp, cache_ckv, cache_kr, t_dec, pc):
    nb, npg = page_table.shape
    nc = npg // pc
    kv_rank = cache_ckv.shape[-1]
    perb3 = lambda b, c, pt: (0, b, 0)
    new3 = lambda b, c, pt: (b, 0, 0)
    grid_spec = pltpu.PrefetchScalarGridSpec(
        num_scalar_prefetch=1,
        grid=(nb, nc),
        in_specs=[pl.BlockSpec((N_HEADS, t_dec, LANES), perb3), pl.BlockSpec((N_HEADS, t_dec, ROPE), perb3),
                  pl.BlockSpec((1, PAGE, kv_rank), new3), pl.BlockSpec((1, PAGE, ROPE), new3),
                  pl.BlockSpec(wukt_all.shape, lambda b, c, pt: (0, 0)),
                  pl.BlockSpec(wvp.shape, lambda b, c, pt: (0, 0, 0)),
                  pl.BlockSpec(memory_space=pl.ANY), pl.BlockSpec(memory_space=pl.ANY)],
        out_specs=pl.BlockSpec((t_dec, N_HEADS * V_DIM), lambda b, c, pt: (b, 0)),
        scratch_shapes=[pltpu.VMEM((2, pc * PAGE, kv_rank), F32), pltpu.VMEM((2, pc * PAGE, ROPE), F32),
                        pltpu.SemaphoreType.DMA((2, 2)),
                        pltpu.VMEM((N_HEADS * t_dec, 1), F32), pltpu.VMEM((N_HEADS * t_dec, 1), F32),
                        pltpu.VMEM((N_HEADS * t_dec, kv_rank), F32)],
    )
    return pl.pallas_call(
        functools.partial(_attn_sample_kernel, pc=pc, nc=nc, npg=npg, t_dec=t_dec),
        grid_spec=grid_spec,
        out_shape=jax.ShapeDtypeStruct((nb * t_dec, N_HEADS * V_DIM), F32),
        compiler_params=_cparams(("arbitrary", "arbitrary")),
    )(page_table.reshape(-1), qa, qr, cnew_pad, krnew_pad, wukt_all, wvp, cache_ckv, cache_kr)


def _merge_kernel(xp_ref, xs_ref, yp_ref, ys_ref, ap_ref, as_ref, wglu_ref, bglu_ref, gs_ref, ga_ref,
                  wo1_ref, wo2_ref, g2_ref, wr_ref, br_ref, tri_ref,
                  h_ref, xn_ref, meta_ref, cnt_ref, run_scr, *, n_prompt_tiles):
    i = pl.program_id(0)

    @pl.when(i == 0)
    def _():
        run_scr[...] = jnp.zeros(run_scr.shape, F32)

    is_p = i < n_prompt_tiles
    x = jnp.where(is_p, xp_ref[...], xs_ref[...])
    y = jnp.where(is_p, yp_ref[...], ys_ref[...]).astype(F32)
    att = jnp.where(is_p, ap_ref[...], as_ref[...])

    y = jax.nn.gelu(y)
    glu = jnp.dot(y.astype(BF16), wglu_ref[...], preferred_element_type=F32) + bglu_ref[...]
    ssm = y * jax.nn.sigmoid(glu)
    mix = jnp.dot(_rms(ssm, gs_ref[...]).astype(BF16), wo1_ref[...], preferred_element_type=F32)
    mix += jnp.dot(_rms(att, ga_ref[...]).astype(BF16), wo2_ref[...], preferred_element_type=F32)
    h = x + mix
    h_ref[...] = h
    xn = _rms(h, g2_ref[...])
    xn_ref[...] = xn

    tm = x.shape[0]
    logits = jnp.dot(xn, wr_ref[...], precision=HIGHEST, preferred_element_type=F32) + br_ref[...]
    lane_i = lax.broadcasted_iota(jnp.int32, (tm, LANES), 1)
    lane = lane_i.astype(F32)
    big = float(LANES)
    first = lambda hit: jnp.min(jnp.where(hit, lane, big), axis=-1, keepdims=True)
    gl = jnp.where(lane_i < N_GROUPS, logits, -jnp.inf)
    gmax = jnp.max(gl, axis=-1, keepdims=True)
    grp = first(gl == gmax)
    p_sel = 1.0 / jnp.sum(jnp.exp(gl - gmax), axis=-1, keepdims=True)
    lane_grp = ((lane_i - N_GROUPS) >> 3).astype(F32)
    in_grp = (lane_i >= N_GROUPS) & (lane_i < N_GROUPS + N_EXPERTS) & (lane_grp == grp)
    el = jnp.where(in_grp, logits, -jnp.inf)
    m1 = jnp.max(el, axis=-1, keepdims=True)
    i1 = first(el == m1)
    el2 = jnp.where(lane == i1, -jnp.inf, el)
    m2 = jnp.max(el2, axis=-1, keepdims=True)
    i2 = first(el2 == m2)
    e21 = jnp.exp(m2 - m1)
    w1 = p_sel / (1.0 + e21)
    w2 = p_sel * e21 / (1.0 + e21)
    e1, e2 = i1 - N_GROUPS, i2 - N_GROUPS

    oh1, oh2 = lane == e1, lane == e2
    oh = jnp.where(oh1 | oh2, 1.0, 0.0)
    before = jnp.dot(tri_ref[...], oh.astype(BF16), preferred_element_type=F32) + run_scr[...]
    r1 = jnp.sum(jnp.where(oh1, before, 0.0), axis=-1, keepdims=True)
    r2 = jnp.sum(jnp.where(oh2, before, 0.0), axis=-1, keepdims=True)
    run_scr[...] += jnp.sum(oh, axis=0, keepdims=True)
    cnt_ref[...] = run_scr[...]
    meta = jnp.zeros((tm, LANES), F32)
    for k, v in enumerate((e1, e2, w1, w2, r1, r2)):
        meta = jnp.where(lane_i == k, v, meta)
    meta_ref[...] = meta


def _merge(x_p, x_s, y_p, y_s, a_p, a_s, wglu, bglu, gs, ga, wo1, wo2, g2, wr, br, tm):
    n_p, d = x_p.shape
    n_s = x_s.shape[0]
    npt, nst = n_p // tm, n_s // tm
    n_all = n_p + n_s
    tri = (lax.broadcasted_iota(jnp.int32, (tm, tm), 0) > lax.broadcasted_iota(jnp.int32, (tm, tm), 1)).astype(BF16)
    prow = lambda i: (jnp.minimum(i, npt - 1), 0)
    srow = lambda i: (jnp.maximum(i - npt, 0), 0)
    row = lambda i: (i, 0)
    const = lambda i: (0, 0)
    w = y_p.shape[1]
    full = lambda a: pl.BlockSpec(a.shape, const)
    return pl.pallas_call(
        functools.partial(_merge_kernel, n_prompt_tiles=npt),
        grid=(npt + nst,),
        in_specs=[pl.BlockSpec((tm, d), prow), pl.BlockSpec((tm, d), srow),
                  pl.BlockSpec((tm, w), prow), pl.BlockSpec((tm, w), srow),
                  pl.BlockSpec((tm, w), prow), pl.BlockSpec((tm, w), srow),
                  full(wglu), full(bglu), full(gs), full(ga), full(wo1), full(wo2), full(g2), full(wr), full(br),
                  full(tri)],
        out_specs=[pl.BlockSpec((tm, d), row), pl.BlockSpec((tm, d), row), pl.BlockSpec((tm, LANES), row),
                   pl.BlockSpec((1, LANES), const)],
        out_shape=[jax.ShapeDtypeStruct((n_all, d), F32), jax.ShapeDtypeStruct((n_all, d), F32),
                   jax.ShapeDtypeStruct((n_all, LANES), F32), jax.ShapeDtypeStruct((1, LANES), F32)],
        scratch_shapes=[pltpu.VMEM((1, LANES), F32)],
        compiler_params=_cparams(("arbitrary",)),
    )(x_p, x_s, y_p, y_s, a_p, a_s, wglu, bglu, gs, ga, wo1, wo2, g2, wr, br, tri)


def _start_row_gather(idx_ref, base, src_hbm, dst, sem, nrows):
    def body(r, _):
        pltpu.make_async_copy(src_hbm.at[pl.ds(idx_ref[base + r], 1)], dst.at[pl.ds(r, 1)], sem).start()
        return 0
    lax.fori_loop(0, nrows, body, 0, unroll=8)


def _wait_row_gather(src_hbm, dst, sem, nrows):
    pltpu.make_async_copy(src_hbm.at[pl.ds(0, nrows)], dst, sem).wait()


def _experts_kernel(te_ref, nu_ref, rt_ref, xn_hbm, wg_ref, wu_ref, wd_ref, o_ref,
                    xbuf, sems, wg_scr, wu_scr, wd_scr, *, tme):
    i = pl.program_id(0)
    slot = i % 2
    n_used = nu_ref[0]

    @pl.when((i == 0) & (n_used > 0))
    def _():
        _start_row_gather(rt_ref, 0, xn_hbm, xbuf.at[0], sems.at[0], tme)

    @pl.when(i + 1 < n_used)
    def _():
        _start_row_gather(rt_ref, (i + 1) * tme, xn_hbm, xbuf.at[1 - slot], sems.at[1 - slot], tme)

    @pl.when((i == 0) | (te_ref[i] != te_ref[jnp.maximum(i - 1, 0)]))
    def _():
        wg_scr[...] = wg_ref[0].astype(BF16)
        wu_scr[...] = wu_ref[0].astype(BF16)
        wd_scr[...] = wd_ref[0].astype(BF16)

    @pl.when(i < n_used)
    def _():
        _wait_row_gather(xn_hbm, xbuf.at[slot], sems.at[slot], tme)
        x = xbuf[slot].astype(BF16)
        hg = jnp.dot(x, wg_scr[...], preferred_element_type=F32)
        hu = jnp.dot(x, wu_scr[...], preferred_element_type=F32)
        hh = (jax.nn.silu(hg) * hu).astype(BF16)
        o_ref[...] = jnp.dot(hh, wd_scr[...], preferred_element_type=F32)

    @pl.when(i >= n_used)
    def _():
        o_ref[...] = jnp.zeros(o_ref.shape, F32)


def _experts(tile_expert, n_used, row_token, xn, w_gate, w_up, w_down, tme):
    nt = tile_expert.shape[0]
    ne, d, de = w_gate.shape
    wmap = lambda i, te, nu, rt: (te[i], 0, 0)
    grid_spec = pltpu.PrefetchScalarGridSpec(
        num_scalar_prefetch=3,
        grid=(nt,),
        in_specs=[pl.BlockSpec(memory_space=pl.ANY),
                  pl.BlockSpec((1, d, de), wmap), pl.BlockSpec((1, d, de), wmap), pl.BlockSpec((1, de, d), wmap)],
        out_specs=pl.BlockSpec((tme, d), lambda i, te, nu, rt: (i, 0)),
        scratch_shapes=[pltpu.VMEM((2, tme, d), F32), pltpu.SemaphoreType.DMA((2,)),
                        pltpu.VMEM((d, de), BF16), pltpu.VMEM((d, de), BF16), pltpu.VMEM((de, d), BF16)],
    )
    return pl.pallas_call(
        functools.partial(_experts_kernel, tme=tme),
        grid_spec=grid_spec,
        out_shape=jax.ShapeDtypeStruct((nt * tme, d), F32),
        compiler_params=_cparams(("arbitrary",)),
    )(tile_expert, n_used, row_token, xn, w_gate, w_up, w_down)


def _combine_kernel(pos_ref, h_ref, meta_ref, eo_hbm, y_ref, buf, sems, *, tmc, nsteps):
    i = pl.program_id(0)
    slot = i % 2

    @pl.when(i == 0)
    def _():
        _start_row_gather(pos_ref, 0, eo_hbm, buf.at[0], sems.at[0], 2 * tmc)

    @pl.when(i + 1 < nsteps)
    def _():
        _start_row_gather(pos_ref, (i + 1) * 2 * tmc, eo_hbm, buf.at[1 - slot], sems.at[1 - slot], 2 * tmc)

    _wait_row_gather(eo_hbm, buf.at[slot], sems.at[slot], 2 * tmc)
    meta = meta_ref[...]
    y_ref[...] = h_ref[...] + meta[:, 2:3] * buf[slot, :tmc, :] + meta[:, 3:4] * buf[slot, tmc:, :]


def _combine(pos, h_all, meta_all, eo, row0, n, tmc):
    d = h_all.shape[1]
    nsteps = n // tmc
    off = row0 // tmc
    grid_spec = pltpu.PrefetchScalarGridSpec(
        num_scalar_prefetch=1,
        grid=(nsteps,),
        in_specs=[pl.BlockSpec((tmc, d), lambda i, p: (i + off, 0)),
                  pl.BlockSpec((tmc, LANES), lambda i, p: (i + off, 0)),
                  pl.BlockSpec(memory_space=pl.ANY)],
        out_specs=pl.BlockSpec((tmc, d), lambda i, p: (i, 0)),
        scratch_shapes=[pltpu.VMEM((2, 2 * tmc, d), F32), pltpu.SemaphoreType.DMA((2,))],
    )
    return pl.pallas_call(
        functools.partial(_combine_kernel, tmc=tmc, nsteps=nsteps),
        grid_spec=grid_spec,
        out_shape=jax.ShapeDtypeStruct((n, d), F32),
        compiler_params=_cparams(("arbitrary",)),
    )(pos, h_all, meta_all, eo)


def _pick_tile(n, pref):
    t = min(pref, n)
    while n % t:
        t //= 2
    return t


def _layer(x_p, x_s, cache_ckv, cache_kr, st_re, st_im, page_table, p):
    nb_p, t_p, d = x_p.shape
    nb_s, t_s, _ = x_s.shape
    n_p, n_s = nb_p * t_p, nb_s * t_s
    past_len = page_table.shape[1] * PAGE
    g = p['ssm_a_re'].shape[0]
    ssm_w = g * SSM_CH
    q_rank = p['w_uq'].shape[0]
    kv_rank = p['w_uk'].shape[0]

    w_in = p['w_in']
    o3 = ssm_w + q_rank + kv_rank
    w_pad = jnp.zeros((d, o3 + LANES), F32).at[:, :o3].set(w_in[:, :o3])
    w_pad = w_pad.at[:, o3 + NOPE:o3 + QK].set(w_in[:, o3:]).astype(BF16)
    row2 = lambda v: v.reshape(1, -1).astype(F32)
    padq = lambda v: jnp.pad(v, (0, LANES - QK)).reshape(1, LANES)
    wq = jnp.pad(p['w_uq'].transpose(1, 0, 2), ((0, 0), (0, 0), (0, LANES - QK))).astype(BF16)
    wk = jnp.pad(p['w_uk'].transpose(1, 0, 2), ((0, 0), (0, 0), (0, LANES - NOPE))).astype(BF16)
    wvp = p['w_uv'].reshape(kv_rank, N_HEADS // 2, 2 * V_DIM).transpose(1, 0, 2).astype(BF16)
    wukt_all = p['w_uk'].reshape(kv_rank, N_HEADS * NOPE).T.astype(BF16)
    wukt = jnp.pad(p['w_uk'].transpose(1, 2, 0), ((0, 0), (0, LANES - NOPE), (0, 0))).astype(BF16)
    sel = (jnp.arange(LANES)[:, None] == NOPE + jnp.arange(ROPE)[None, :]).astype(BF16)
    gq, gk = padq(p['qk_norm_q']), padq(p['qk_norm_k'])
    ssm_params = (p['ssm_a_re'], p['ssm_a_im'], p['ssm_log_dt'], p['ssm_b_re'], p['ssm_b_im'],
                  p['ssm_c_re'], p['ssm_c_im'], p['ssm_d'])

    tm_p = _pick_tile(t_p, 512)
    tab_p = _rope_tables(jnp.arange(t_p))
    u, cq, ckv_p, krp_p = _in_proj(x_p.reshape(n_p, d), row2(p['norm1_g']), w_pad, row2(p['q_norm_g']),
                                   row2(p['kv_norm_g']), tab_p, tm_p, ssm_w, q_rank, kv_rank)
    lc_p = _pick_tile(t_p, 16)
    y_ssm_p, hf_p = _s5(u, jnp.zeros((g, nb_p, 2 * SSM_P), F32), _s5_weights(*ssm_params, lc_p), nb_p, t_p, lc_p)
    tq = _pick_tile(t_p, 256)
    att_p = _attn_prompt(cq, ckv_p, krp_p, tab_p, wq, wk, wvp, gq, gk, nb_p, t_p, tq, tq)

    tm_s = _pick_tile(n_s, 512)
    pos_s = past_len + jnp.arange(t_s)
    tab_s = tuple(jnp.tile(a, (tm_s // t_s, 1)) for a in _rope_tables(pos_s))
    u, cq, ckv_s, krp_s = _in_proj(x_s.reshape(n_s, d), row2(p['norm1_g']), w_pad, row2(p['q_norm_g']),
                                   row2(p['kv_norm_g']), tab_s, tm_s, ssm_w, q_rank, kv_rank)
    h0 = jnp.concatenate([st_re, st_im], axis=-1).transpose(1, 0, 2).astype(F32)
    y_ssm_s, hf_s = _s5(u, h0, _s5_weights(*ssm_params, t_s), nb_s, t_s, t_s)
    tab_q = tuple(jnp.tile(a, (n_s // tm_s, 1)) for a in tab_s)
    qa, qr = _q_sample(cq, tab_q, wq, wukt, sel, gq, gk)
    kr_s = krp_s[:, NOPE:QK]
    cnew = jnp.pad(ckv_s.reshape(nb_s, t_s, kv_rank), ((0, 0), (0, PAGE - t_s), (0, 0)))
    krnew = jnp.pad(kr_s.reshape(nb_s, t_s, ROPE), ((0, 0), (0, PAGE - t_s), (0, 0)))
    pc = _pick_tile(page_table.shape[1], 16)
    att_s = _attn_sample(page_table, qa, qr, cnew, krnew, wukt_all, wvp, cache_ckv, cache_kr, t_s, pc)

    tm = _pick_tile(math.gcd(n_p, n_s), 512)
    w_out = p['w_out'].astype(BF16)
    wr = jnp.zeros((d, LANES), F32).at[:, :N_GROUPS].set(p['w_router_group'])
    wr = wr.at[:, N_GROUPS:N_GROUPS + N_EXPERTS].set(p['w_router_expert'])
    br = jnp.zeros((1, LANES), F32).at[0, :N_GROUPS].set(p['b_router_group'])
    br = br.at[0, N_GROUPS:N_GROUPS + N_EXPERTS].set(p['b_router_expert'])
    h_all, xn_all, meta, cnt = _merge(
        x_p.reshape(n_p, d), x_s.reshape(n_s, d), y_ssm_p, y_ssm_s, att_p, att_s,
        p['ssm_w_glu'].astype(BF16), row2(p['ssm_b_glu']), row2(p['out_norm_ssm']), row2(p['out_norm_attn']),
        w_out[:ssm_w], w_out[ssm_w:], row2(p['norm2_g']), wr, br, tm)

    n_all = n_p + n_s
    tme = 256
    nt = (2 * n_all) // tme + N_EXPERTS
    counts = cnt[0, :N_EXPERTS].astype(jnp.int32)
    tiles_per = (counts + tme - 1) // tme
    tile_end = jnp.cumsum(tiles_per)
    offs = (tile_end - tiles_per) * tme
    e1, e2 = meta[:, 0].astype(jnp.int32), meta[:, 1].astype(jnp.int32)
    pos1 = offs[e1] + meta[:, 4].astype(jnp.int32)
    pos2 = offs[e2] + meta[:, 5].astype(jnp.int32)
    tok = jnp.arange(n_all, dtype=jnp.int32)
    row_token = jnp.zeros((nt * tme,), jnp.int32).at[pos1].set(tok).at[pos2].set(tok)
    tile_expert = jnp.minimum(jnp.searchsorted(tile_end, jnp.arange(nt), side='right'), N_EXPERTS - 1).astype(jnp.int32)
    n_used = tile_end[-1:].astype(jnp.int32)
    eo = _experts(tile_expert, n_used, row_token, xn_all, p['w_gate'], p['w_up'], p['w_down'], tme)

    tmc = _pick_tile(math.gcd(n_p, n_s), 256)
    tile_pos = lambda a, b: jnp.concatenate([a.reshape(-1, tmc), b.reshape(-1, tmc)], axis=1).reshape(-1)
    y_p = _combine(tile_pos(pos1[:n_p], pos2[:n_p]), h_all, meta, eo, 0, n_p, tmc)
    y_s = _combine(tile_pos(pos1[n_p:], pos2[n_p:]), h_all, meta, eo, n_p, n_s, tmc)

    n_pg = t_p // PAGE
    split = lambda hf: (hf[..., :SSM_P].transpose(1, 0, 2), hf[..., SSM_P:].transpose(1, 0, 2))
    hp_re, hp_im = split(hf_p)
    hs_re, hs_im = split(hf_s)
    return (y_p.reshape(nb_p, t_p, d), y_s.reshape(nb_s, t_s, d),
            ckv_p.reshape(nb_p, n_pg, PAGE, kv_rank), krp_p[:, NOPE:QK].reshape(nb_p, n_pg, PAGE, ROPE),
            hp_re, hp_im, ckv_s.reshape(nb_s, t_s, kv_rank), kr_s.reshape(nb_s, t_s, ROPE), hs_re, hs_im)


_PARAM_NAMES = ('norm1_g', 'w_in', 'ssm_a_re', 'ssm_a_im', 'ssm_log_dt', 'ssm_b_re', 'ssm_b_im', 'ssm_c_re',
                'ssm_c_im', 'ssm_d', 'ssm_w_glu', 'ssm_b_glu', 'q_norm_g', 'w_uq', 'kv_norm_g', 'w_uk', 'w_uv',
                'qk_norm_q', 'qk_norm_k', 'out_norm_ssm', 'out_norm_attn', 'w_out', 'norm2_g', 'w_router_group',
                'b_router_group', 'w_router_expert', 'b_router_expert', 'w_gate', 'w_up', 'w_down')


def kernel(x_prompt, x_sample, cache_ckv, cache_krope, state_ssm_re, state_ssm_im, page_table, norm1_g, w_in, ssm_a_re, ssm_a_im, ssm_log_dt, ssm_b_re, ssm_b_im, ssm_c_re, ssm_c_im, ssm_d, ssm_w_glu, ssm_b_glu, q_norm_g, w_uq, kv_norm_g, w_uk, w_uv, qk_norm_q, qk_norm_k, out_norm_ssm, out_norm_attn, w_out, norm2_g, w_router_group, b_router_group, w_router_expert, b_router_expert, w_gate, w_up, w_down):
    params = (norm1_g, w_in, ssm_a_re, ssm_a_im, ssm_log_dt, ssm_b_re, ssm_b_im, ssm_c_re, ssm_c_im, ssm_d,
              ssm_w_glu, ssm_b_glu, q_norm_g, w_uq, kv_norm_g, w_uk, w_uv, qk_norm_q, qk_norm_k, out_norm_ssm,
              out_norm_attn, w_out, norm2_g, w_router_group, b_router_group, w_router_expert, b_router_expert,
              w_gate, w_up, w_down)
    depth = w_in.shape[0]
    h_p, h_s = x_prompt, x_sample
    outs = [[] for _ in range(8)]
    for layer in range(depth):
        p = {k: v[layer] for k, v in zip(_PARAM_NAMES, params)}
        res = _layer(h_p, h_s, cache_ckv[layer], cache_krope[layer], state_ssm_re[layer], state_ssm_im[layer],
                     page_table, p)
        h_p, h_s = res[0], res[1]
        for acc, r in zip(outs, res[2:]):
            acc.append(r)
    return (h_p, h_s) + tuple(jnp.stack(o) for o in outs)
```

```python
import functools
import math

import jax
import jax.numpy as jnp
from jax import lax
from jax.experimental import pallas as pl
from jax.experimental.pallas import tpu as pltpu

F32 = jnp.float32
BF16 = jnp.bfloat16
HIGHEST = lax.Precision.HIGHEST

LANES = 128
SUBLANES = 8
VMEM_LIMIT = 56 * 1024 * 1024

EPS = 1e-6
ROPE_BASE = 10000.0
PAGE = 128
SSM_CH = 16
SSM_P = 64
GPB = LANES // SSM_CH
S5_MAX_TOKENS = 8192
N_HEADS = 8
NOPE = 64
ROPE = 32
QK = NOPE + ROPE
V_DIM = 64
N_GROUPS = 4
EXP_PER_GROUP = 8
N_EXPERTS = N_GROUPS * EXP_PER_GROUP

NT_DIMS = (((1,), (1,)), ((), ()))


def _cparams(sem, vmem=VMEM_LIMIT):
    return pltpu.CompilerParams(dimension_semantics=sem, vmem_limit_bytes=vmem)


def _rms(x, g):
    return x * lax.rsqrt(jnp.mean(x * x, axis=-1, keepdims=True) + EPS) * g


def _rms_qk(x, g):
    return x * lax.rsqrt(jnp.sum(x * x, axis=-1, keepdims=True) * (1.0 / QK) + EPS) * g


def _rope(x, cos, sinp, sinm):
    return x * cos + pltpu.roll(x, 16, 1) * sinp + pltpu.roll(x, LANES - 16, 1) * sinm


def _rope_tables(pos):
    inv_freq = ROPE_BASE ** (-jnp.arange(0, ROPE, 2, dtype=F32) / ROPE)
    ang = pos.astype(F32)[:, None] * inv_freq[None, :]
    c, s = jnp.cos(ang), jnp.sin(ang)
    t = pos.shape[0]
    z = lambda n: jnp.zeros((t, n), F32)
    cosf = jnp.concatenate([jnp.ones((t, NOPE), F32), c, c, z(LANES - QK)], axis=1)
    sinp = jnp.concatenate([z(NOPE + ROPE // 2), s, z(LANES - QK)], axis=1)
    sinm = jnp.concatenate([z(NOPE), -s, z(LANES - QK + ROPE // 2)], axis=1)
    return cosf, sinp, sinm


def _in_proj_kernel(x_ref, g1_ref, w_ref, gq_ref, gkv_ref, cos_ref, sinp_ref, sinm_ref,
                    u_ref, cq_ref, ckv_ref, kr_ref, *, ssm_w, q_rank, kv_rank):
    n = _rms(x_ref[...], g1_ref[...])
    z = jnp.dot(n.astype(BF16), w_ref[...], preferred_element_type=F32)
    o1, o2, o3 = ssm_w, ssm_w + q_rank, ssm_w + q_rank + kv_rank
    for j in range(ssm_w // LANES):
        u_ref[j] = z[:, j * LANES:(j + 1) * LANES]
    cq_ref[...] = _rms(z[:, o1:o2], gq_ref[...]).astype(cq_ref.dtype)
    ckv_ref[...] = _rms(z[:, o2:o3], gkv_ref[...])
    kr_ref[...] = _rope(z[:, o3:o3 + LANES], cos_ref[...], sinp_ref[...], sinm_ref[...])


def _in_proj(x, g1, w_pad, gq, gkv, tables, tm, ssm_w, q_rank, kv_rank):
    n, d = x.shape
    period = tables[0].shape[0]
    nper = period // tm
    row = lambda i: (i, 0)
    const = lambda i: (0, 0)
    tab = pl.BlockSpec((tm, LANES), lambda i: (i % nper, 0))
    return pl.pallas_call(
        functools.partial(_in_proj_kernel, ssm_w=ssm_w, q_rank=q_rank, kv_rank=kv_rank),
        grid=(n // tm,),
        in_specs=[pl.BlockSpec((tm, d), row), pl.BlockSpec((1, d), const),
                  pl.BlockSpec(w_pad.shape, const), pl.BlockSpec((1, q_rank), const),
                  pl.BlockSpec((1, kv_rank), const), tab, tab, tab],
        out_specs=[pl.BlockSpec((ssm_w // LANES, tm, LANES), lambda i: (0, i, 0)), pl.BlockSpec((tm, q_rank), row),
                   pl.BlockSpec((tm, kv_rank), row), pl.BlockSpec((tm, LANES), row)],
        out_shape=[jax.ShapeDtypeStruct((ssm_w // LANES, n, LANES), F32), jax.ShapeDtypeStruct((n, q_rank), BF16),
                   jax.ShapeDtypeStruct((n, kv_rank), F32), jax.ShapeDtypeStruct((n, LANES), F32)],
        compiler_params=_cparams(("parallel",)),
    )(x, g1, w_pad, gq, gkv, *tables)


def _s5_weights(a_re, a_im, log_dt, b_re, b_im, c_re, c_im, d, lc):
    g, p = a_re.shape
    dt = jnp.exp(log_dt)[:, None]
    den = a_re * a_re + a_im * a_im
    mag, ang = jnp.exp(a_re * dt), a_im * dt
    lb_re, lb_im = mag * jnp.cos(ang), mag * jnp.sin(ang)
    num_re = lb_re - 1.0
    coef_re = (num_re * a_re + lb_im * a_im) / den
    coef_im = (lb_im * a_re - num_re * a_im) / den
    bb_re = coef_re[..., None] * b_re - coef_im[..., None] * b_im
    bb_im = coef_re[..., None] * b_im + coef_im[..., None] * b_re

    def lam_pow(k):
        kk = k.astype(F32)[:, None, None]
        m = jnp.exp(a_re * dt * kk)
        return m * jnp.cos(ang * kk), m * jnp.sin(ang * kk)

    steps = jnp.arange(lc)
    pr, pi = lam_pow(lc - 1 - steps)
    bz_re = pr[..., None] * bb_re[None] - pi[..., None] * bb_im[None]
    bz_im = pr[..., None] * bb_im[None] + pi[..., None] * bb_re[None]
    bz = jnp.concatenate([bz_re, bz_im], axis=2)
    bz = bz.transpose(1, 0, 3, 2).reshape(g, lc * SSM_CH, 2 * p)
    qr, qi = lam_pow(steps + 1)
    cl_re = c_re[None] * qr[:, :, None, :] - c_im[None] * qi[:, :, None, :]
    cl_im = c_re[None] * qi[:, :, None, :] + c_im[None] * qr[:, :, None, :]
    cz = jnp.concatenate([cl_re, -cl_im], axis=3)
    cz = cz.transpose(1, 3, 0, 2).reshape(g, 2 * p, lc * SSM_CH)
    tr, ti = lam_pow(steps)
    cb_re = jnp.einsum('gcp,kgp,gpd->gkcd', c_re, tr, bb_re, precision=HIGHEST)
    cb_re -= jnp.einsum('gcp,kgp,gpd->gkcd', c_re, ti, bb_im, precision=HIGHEST)
    cb_re -= jnp.einsum('gcp,kgp,gpd->gkcd', c_im, tr, bb_im, precision=HIGHEST)
    cb_re -= jnp.einsum('gcp,kgp,gpd->gkcd', c_im, ti, bb_re, precision=HIGHEST)
    tau = steps[None, :] - steps[:, None]
    ksel = cb_re[:, jnp.clip(tau, 0, lc - 1)]
    ksel = jnp.where((tau >= 0)[None, :, :, None, None], ksel, 0.0)
    tz = ksel.transpose(0, 1, 4, 2, 3).reshape(g, lc * SSM_CH, lc * SSM_CH)
    nsl = g // GPB
    eye = jnp.eye(GPB, dtype=F32)
    bd = lambda a: a[..., None, :] * eye.reshape(1, 1, GPB, 1, 1, GPB, 1)
    bzg = bz.reshape(nsl, GPB, lc, SSM_CH, 2, p).transpose(0, 2, 1, 3, 4, 5)
    bz_s = bd(bzg).reshape(nsl, lc * LANES, 2 * GPB * p)
    tzg = tz.reshape(nsl, GPB, lc, SSM_CH, lc, SSM_CH).transpose(0, 2, 1, 3, 4, 5)
    wu = bd(tzg).reshape(nsl, lc * LANES, lc * LANES)
    czg = cz.reshape(nsl, GPB, 2, p, lc, SSM_CH).transpose(0, 2, 1, 3, 4, 5)
    wh = bd(czg).reshape(nsl, 2 * GPB * p, lc * LANES)
    wy = jnp.concatenate([wh, wu], axis=1)
    lr, li = lam_pow(jnp.array([lc]))
    lam_re = lr[0].reshape(nsl, 1, GPB * p)
    lam_im = li[0].reshape(nsl, 1, GPB * p)
    dvec = jnp.tile(d.reshape(nsl, 1, LANES), (1, 1, lc))
    return bz_s.astype(BF16), wy.astype(BF16), lam_re, lam_im, dvec


def _s5_kernel(u_ref, bz_ref, wy_ref, lr_ref, li_ref, d_ref, h0_ref, y_ref, hf_ref, s_scr, *, nk, bs, lc, rb):
    nh = s_scr.shape[0] // 2
    blk = lambda a, c: a[:, c * LANES:(c + 1) * LANES]

    def ucat(r0):
        return jnp.concatenate([u_ref[0, pl.ds(r0 * lc + t, rb, stride=lc), :] for t in range(lc)], axis=1)

    def phase1(i, _):
        r0 = pl.multiple_of(i * rb, rb)
        s = jnp.dot(ucat(r0).astype(BF16), bz_ref[0], preferred_element_type=F32)
        for c in range(2 * nh):
            s_scr[c, pl.ds(r0, rb), :] = blk(s, c)
        return 0

    lax.fori_loop(0, (bs * nk) // rb, phase1, 0)
    lr, li = lr_ref[0], li_ref[0]

    def step(k, h):
        rows = pl.ds(k, bs, stride=nk)
        new = [None] * (2 * nh)
        for c in range(nh):
            h_re, h_im = h[c], h[nh + c]
            s_re, s_im = s_scr[c, rows, :], s_scr[nh + c, rows, :]
            s_scr[c, rows, :] = h_re
            s_scr[nh + c, rows, :] = h_im
            new[c] = blk(lr, c) * h_re - blk(li, c) * h_im + s_re
            new[nh + c] = blk(lr, c) * h_im + blk(li, c) * h_re + s_im
        return tuple(new)

    h0 = h0_ref[0, 0]
    hf = lax.fori_loop(0, nk, step, tuple(blk(h0, c) for c in range(2 * nh)))
    hf_ref[0, 0] = jnp.concatenate(hf, axis=1)

    def phase3(i, _):
        r0 = pl.multiple_of(i * rb, rb)
        uc = ucat(r0)
        hprev = jnp.concatenate([s_scr[c, pl.ds(r0, rb), :] for c in range(2 * nh)], axis=1)
        lhs = jnp.concatenate([hprev.astype(BF16), uc.astype(BF16)], axis=1)
        y = jnp.dot(lhs, wy_ref[0], preferred_element_type=F32) + d_ref[0] * uc
        for t in range(lc):
            y_ref[0, pl.ds(r0 * lc + t, rb, stride=lc), :] = y[:, t * LANES:(t + 1) * LANES]
        return 0

    lax.fori_loop(0, (bs * nk) // rb, phase3, 0)


def _s5(u, h0, weights, nb, t, lc):
    bz, wy, lam_re, lam_im, dvec = weights
    nsl = u.shape[0]
    nk = t // lc
    sw = h0.shape[-1]
    bs = nb
    while bs * t > S5_MAX_TOKENS and bs % 2 == 0:
        bs //= 2
    rb = _pick_tile(bs * nk, 256)
    h0 = h0.reshape(nsl, nb // bs, bs, sw)
    slab = lambda j, b: (j, 0, 0)
    tok = lambda j, b: (j, b, 0)
    st = lambda j, b: (j, b, 0, 0)
    y, hf = pl.pallas_call(
        functools.partial(_s5_kernel, nk=nk, bs=bs, lc=lc, rb=rb),
        grid=(nsl, nb // bs),
        in_specs=[pl.BlockSpec((1, bs * t, LANES), tok), pl.BlockSpec((1,) + bz.shape[1:], slab),
                  pl.BlockSpec((1,) + wy.shape[1:], slab), pl.BlockSpec((1, 1, sw // 2), slab),
                  pl.BlockSpec((1, 1, sw // 2), slab), pl.BlockSpec((1, 1, lc * LANES), slab),
                  pl.BlockSpec((1, 1, bs, sw), st)],
        out_specs=[pl.BlockSpec((1, bs * t, LANES), tok), pl.BlockSpec((1, 1, bs, sw), st)],
        out_shape=[jax.ShapeDtypeStruct(u.shape, F32), jax.ShapeDtypeStruct(h0.shape, F32)],
        scratch_shapes=[pltpu.VMEM((sw // LANES, bs * nk, LANES), F32)],
        compiler_params=_cparams(("parallel", "parallel")),
    )(u, bz, wy, lam_re, lam_im, dvec, h0)
    return y, hf.reshape(nsl, nb, sw)


def _attn_prompt_kernel(cq_ref, ckv_ref, krp_ref, cos_ref, sinp_ref, sinm_ref, wq_ref, wk_ref, wv_ref,
                        gq_ref, gk_ref, o_ref, k_scr, v_scr, *, tq, tk):
    qi = pl.program_id(1)

    @pl.when(qi == 0)
    def _():
        c = ckv_ref[...].astype(BF16)
        krp = krp_ref[...]
        for h in range(N_HEADS):
            kh = jnp.dot(c, wk_ref[h], preferred_element_type=F32) + krp
            k_scr[h] = _rms_qk(kh, gk_ref[...]).astype(BF16)
        for hp in range(N_HEADS // 2):
            v_scr[hp] = jnp.dot(c, wv_ref[hp], preferred_element_type=F32).astype(BF16)

    cq = cq_ref[...]
    cos, sinp, sinm = cos_ref[...], sinp_ref[...], sinm_ref[...]
    row = qi * tq + lax.broadcasted_iota(jnp.int32, (tq, tk), 0)
    col = lax.broadcasted_iota(jnp.int32, (tq, tk), 1)
    lane = lax.broadcasted_iota(jnp.int32, (tq, LANES), 1)
    n_full = qi * (tq // tk)

    for hp in range(N_HEADS // 2):
        outs = []
        for h in (2 * hp, 2 * hp + 1):
            q = _rope(jnp.dot(cq, wq_ref[h], preferred_element_type=F32), cos, sinp, sinm)
            qb = (_rms_qk(q, gq_ref[...]) * (1.0 / math.sqrt(QK))).astype(BF16)

            def kv_step(j, carry, masked, h=h, hp=hp, qb=qb):
                m, l, acc = carry
                ks = pl.ds(pl.multiple_of(j * tk, tk), tk)
                s = lax.dot_general(qb, k_scr[h, ks, :], NT_DIMS, preferred_element_type=F32)
                if masked:
                    s = jnp.where(col + j * tk <= row, s, -jnp.inf)
                m_new = jnp.maximum(m, jnp.max(s, axis=-1, keepdims=True))
                p = jnp.exp(s - m_new)
                corr = jnp.exp(m - m_new)
                l = l * corr + jnp.sum(p, axis=-1, keepdims=True)
                acc = acc * corr + jnp.dot(p.astype(BF16), v_scr[hp, ks, :], preferred_element_type=F32)
                return m_new, l, acc

            carry = (jnp.full((tq, 1), -jnp.inf, F32), jnp.zeros((tq, 1), F32), jnp.zeros((tq, LANES), F32))
            carry = lax.fori_loop(0, n_full, functools.partial(kv_step, masked=False), carry)
            for jj in range(tq // tk):
                carry = kv_step(n_full + jj, carry, True)
            outs.append(carry[2] / carry[1])
        o_ref[:, hp * LANES:(hp + 1) * LANES] = jnp.where(lane < V_DIM, outs[0], outs[1])


def _attn_prompt(cq, ckv, krp, tables, wq, wk, wvp, gq, gk, nb, t, tq, tk):
    n = cq.shape[0]
    nq = t // tq
    qrow = lambda b, i: (b * nq + i, 0)
    seq = lambda b, i: (b, 0)
    tab = pl.BlockSpec((tq, LANES), lambda b, i: (i, 0))
    c3 = lambda b, i: (0, 0, 0)
    c2 = lambda b, i: (0, 0)
    return pl.pallas_call(
        functools.partial(_attn_prompt_kernel, tq=tq, tk=tk),
        grid=(nb, nq),
        in_specs=[pl.BlockSpec((tq, cq.shape[1]), qrow), pl.BlockSpec((t, ckv.shape[1]), seq),
                  pl.BlockSpec((t, LANES), seq), tab, tab, tab,
                  pl.BlockSpec(wq.shape, c3), pl.BlockSpec(wk.shape, c3), pl.BlockSpec(wvp.shape, c3),
                  pl.BlockSpec((1, LANES), c2), pl.BlockSpec((1, LANES), c2)],
        out_specs=pl.BlockSpec((tq, N_HEADS * V_DIM), qrow),
        out_shape=jax.ShapeDtypeStruct((n, N_HEADS * V_DIM), F32),
        scratch_shapes=[pltpu.VMEM((N_HEADS, t, LANES), BF16), pltpu.VMEM((N_HEADS // 2, t, LANES), BF16)],
        compiler_params=_cparams(("parallel", "arbitrary")),
    )(cq, ckv, krp, *tables, wq, wk, wvp, gq, gk)


def _q_sample_kernel(cq_ref, cos_ref, sinp_ref, sinm_ref, wq_ref, wukt_ref, sel_ref, gq_ref, gk_ref,
                     qa_ref, qr_ref):
    cq = cq_ref[...]
    for h in range(N_HEADS):
        q = _rope(jnp.dot(cq, wq_ref[h], preferred_element_type=F32), cos_ref[...], sinp_ref[...], sinm_ref[...])
        q = _rms_qk(q, gq_ref[...]) * (1.0 / math.sqrt(QK))
        qk = (q * gk_ref[...]).astype(BF16)
        qa_ref[h] = jnp.dot(qk, wukt_ref[h], preferred_element_type=F32).astype(BF16)
        qr_ref[h] = jnp.dot(qk, sel_ref[...], preferred_element_type=F32).astype(BF16)


def _q_sample(cq, tables, wq, wukt, sel, gq, gk):
    n = cq.shape[0]
    return pl.pallas_call(
        _q_sample_kernel,
        out_shape=[jax.ShapeDtypeStruct((N_HEADS, n, LANES), BF16), jax.ShapeDtypeStruct((N_HEADS, n, ROPE), BF16)],
        compiler_params=pltpu.CompilerParams(vmem_limit_bytes=VMEM_LIMIT),
    )(cq, *tables, wq, wukt, sel, gq, gk)


def _attn_sample_kernel(pt_ref, qa_ref, qr_ref, cnew_ref, krnew_ref, wukt_ref, wv_ref, ckv_hbm, kr_hbm,
                        o_ref, cbuf, kbuf, sems, m_scr, l_scr, acc_scr, *, pc, nc, npg, t_dec):
    b, c = pl.program_id(0), pl.program_id(1)
    g = b * nc + c
    slot = g % 2
    nkeys = pc * PAGE
    rows = N_HEADS * t_dec

    def page_copies(step, slot_):
        base = step * pc
        cps = []
        for j in range(pc):
            pg = pt_ref[base + j]
            dst = pl.ds(j * PAGE, PAGE)
            cps.append(pltpu.make_async_copy(ckv_hbm.at[pg], cbuf.at[slot_, dst, :], sems.at[0, slot_]))
            cps.append(pltpu.make_async_copy(kr_hbm.at[pg], kbuf.at[slot_, :, dst], sems.at[1, slot_]))
        return cps

    @pl.when(g == 0)
    def _():
        for cp in page_copies(g, slot):
            cp.start()

    @pl.when(g + 1 < pl.num_programs(0) * nc)
    def _():
        for cp in page_copies(g + 1, 1 - slot):
            cp.start()

    @pl.when(c == 0)
    def _():
        m_scr[...] = jnp.full(m_scr.shape, -jnp.inf, F32)
        l_scr[...] = jnp.zeros(l_scr.shape, F32)
        acc_scr[...] = jnp.zeros(acc_scr.shape, F32)

    qa = qa_ref[...].reshape(rows, LANES)
    qr = qr_ref[...].reshape(rows, ROPE)

    def attend(cf, krt, mask):
        nk = cf.shape[0]
        cb = cf.astype(BF16)
        knt = lax.dot_general(wukt_ref[...], cb, NT_DIMS, preferred_element_type=F32)
        n2 = jnp.sum((knt * knt).reshape(N_HEADS, NOPE, nk), axis=1)
        kr2 = jnp.sum(krt * krt, axis=0, keepdims=True)
        rinv = lax.rsqrt((n2 + kr2) * (1.0 / QK) + EPS)
        s = lax.dot_general(qa, cb, NT_DIMS, preferred_element_type=F32)
        s += jnp.dot(qr, krt.astype(BF16), preferred_element_type=F32)
        s = (s.reshape(N_HEADS, t_dec, nk) * rinv[:, None, :]).reshape(rows, nk)
        if mask is not None:
            s = jnp.where(mask, s, -jnp.inf)
        m = m_scr[...]
        m_new = jnp.maximum(m, jnp.max(s, axis=-1, keepdims=True))
        p = jnp.exp(s - m_new)
        corr = jnp.exp(m - m_new)
        l_scr[...] = l_scr[...] * corr + jnp.sum(p, axis=-1, keepdims=True)
        acc_scr[...] = acc_scr[...] * corr + jnp.dot(p.astype(BF16), cb, preferred_element_type=F32)
        m_scr[...] = m_new

    for cp in page_copies(g, slot):
        cp.wait()
    attend(cbuf[slot], kbuf[slot], None)

    @pl.when(c == nc - 1)
    def _():
        key = lax.broadcasted_iota(jnp.int32, (rows, PAGE), 1)
        step = lax.broadcasted_iota(jnp.int32, (rows, PAGE), 0) % t_dec
        attend(cnew_ref[0], krnew_ref[0], key <= step)
        o_lat = (acc_scr[...] / l_scr[...]).astype(BF16)
        lane = lax.broadcasted_iota(jnp.int32, (t_dec, LANES), 1)
        for hp in range(N_HEADS // 2):
            lo = jnp.dot(o_lat[(2 * hp) * t_dec:(2 * hp + 1) * t_dec], wv_ref[hp], preferred_element_type=F32)
            hi = jnp.dot(o_lat[(2 * hp + 1) * t_dec:(2 * hp + 2) * t_dec], wv_ref[hp], preferred_element_type=F32)
            o_ref[:, hp * LANES:(hp + 1) * LANES] = jnp.where(lane < V_DIM, lo, hi)


def _attn_sample(page_table, qa, qr, cnew_pad, krnew_pad, wukt_all, wvp, cache_ckv, cache_kr, t_dec, pc):
    nb, npg = page_table.shape
    nc = npg // pc
    kv_rank = cache_ckv.shape[-1]
    perb3 = lambda b, c, pt: (0, b, 0)
    new3 = lambda b, c, pt: (b, 0, 0)
    grid_spec = pltpu.PrefetchScalarGridSpec(
        num_scalar_prefetch=1,
        grid=(nb, nc),
        in_specs=[pl.BlockSpec((N_HEADS, t_dec, LANES), perb3), pl.BlockSpec((N_HEADS, t_dec, ROPE), perb3),
                  pl.BlockSpec((1, PAGE, kv_rank), new3), pl.BlockSpec((1, ROPE, PAGE), new3),
                  pl.BlockSpec(wukt_all.shape, lambda b, c, pt: (0, 0)),
                  pl.BlockSpec(wvp.shape, lambda b, c, pt: (0, 0, 0)),
                  pl.BlockSpec(memory_space=pl.ANY), pl.BlockSpec(memory_space=pl.ANY)],
        out_specs=pl.BlockSpec((t_dec, N_HEADS * V_DIM), lambda b, c, pt: (b, 0)),
        scratch_shapes=[pltpu.VMEM((2, pc * PAGE, kv_rank), F32), pltpu.VMEM((2, ROPE, pc * PAGE), F32),
                        pltpu.SemaphoreType.DMA((2, 2)),
                        pltpu.VMEM((N_HEADS * t_dec, 1), F32), pltpu.VMEM((N_HEADS * t_dec, 1), F32),
                        pltpu.VMEM((N_HEADS * t_dec, kv_rank), F32)],
    )
    return pl.pallas_call(
        functools.partial(_attn_sample_kernel, pc=pc, nc=nc, npg=npg, t_dec=t_dec),
        grid_spec=grid_spec,
        out_shape=jax.ShapeDtypeStruct((nb * t_dec, N_HEADS * V_DIM), F32),
        compiler_params=_cparams(("arbitrary", "arbitrary")),
    )(page_table.reshape(-1), qa, qr, cnew_pad, krnew_pad, wukt_all, wvp, cache_ckv, cache_kr)


def _merge_kernel(xp_ref, xs_ref, yp_ref, ys_ref, ap_ref, as_ref, wglu_ref, bglu_ref, gs_ref, ga_ref,
                  wo1_ref, wo2_ref, g2_ref, wr_ref, br_ref, tri_ref,
                  h_ref, xn_ref, meta_ref, cnt_ref, run_scr, *, n_prompt_tiles):
    i = pl.program_id(0)

    @pl.when(i == 0)
    def _():
        run_scr[...] = jnp.zeros(run_scr.shape, F32)

    is_p = i < n_prompt_tiles
    x = jnp.where(is_p, xp_ref[...], xs_ref[...])
    slabs = lambda ref: jnp.concatenate([ref[j] for j in range(ref.shape[0])], axis=1)
    y = jnp.where(is_p, slabs(yp_ref), slabs(ys_ref))
    att = jnp.where(is_p, ap_ref[...], as_ref[...])

    y = jax.nn.gelu(y)
    glu = jnp.dot(y.astype(BF16), wglu_ref[...], preferred_element_type=F32) + bglu_ref[...]
    ssm = y * jax.nn.sigmoid(glu)
    mix = jnp.dot(_rms(ssm, gs_ref[...]).astype(BF16), wo1_ref[...], preferred_element_type=F32)
    mix += jnp.dot(_rms(att, ga_ref[...]).astype(BF16), wo2_ref[...], preferred_element_type=F32)
    h = x + mix
    h_ref[...] = h
    xn = _rms(h, g2_ref[...])
    xn_ref[...] = xn

    tm = x.shape[0]
    logits = jnp.dot(xn, wr_ref[...], precision=HIGHEST, preferred_element_type=F32) + br_ref[...]
    lane_i = lax.broadcasted_iota(jnp.int32, (tm, LANES), 1)
    lane = lane_i.astype(F32)
    big = float(LANES)
    first = lambda hit: jnp.min(jnp.where(hit, lane, big), axis=-1, keepdims=True)
    gl = jnp.where(lane_i < N_GROUPS, logits, -jnp.inf)
    gmax = jnp.max(gl, axis=-1, keepdims=True)
    grp = first(gl == gmax)
    p_sel = 1.0 / jnp.sum(jnp.exp(gl - gmax), axis=-1, keepdims=True)
    lane_grp = ((lane_i - N_GROUPS) >> 3).astype(F32)
    in_grp = (lane_i >= N_GROUPS) & (lane_i < N_GROUPS + N_EXPERTS) & (lane_grp == grp)
    el = jnp.where(in_grp, logits, -jnp.inf)
    m1 = jnp.max(el, axis=-1, keepdims=True)
    i1 = first(el == m1)
    el2 = jnp.where(lane == i1, -jnp.inf, el)
    m2 = jnp.max(el2, axis=-1, keepdims=True)
    i2 = first(el2 == m2)
    e21 = jnp.exp(m2 - m1)
    w1 = p_sel / (1.0 + e21)
    w2 = p_sel * e21 / (1.0 + e21)
    e1, e2 = i1 - N_GROUPS, i2 - N_GROUPS

    oh1, oh2 = lane == e1, lane == e2
    oh = jnp.where(oh1 | oh2, 1.0, 0.0)
    before = jnp.dot(tri_ref[...], oh.astype(BF16), preferred_element_type=F32) + run_scr[...]
    r1 = jnp.sum(jnp.where(oh1, before, 0.0), axis=-1, keepdims=True)
    r2 = jnp.sum(jnp.where(oh2, before, 0.0), axis=-1, keepdims=True)
    run_scr[...] += jnp.sum(oh, axis=0, keepdims=True)
    cnt_ref[...] = run_scr[...]
    meta = jnp.zeros((tm, LANES), F32)
    for k, v in enumerate((e1, e2, w1, w2, r1, r2)):
        meta = jnp.where(lane_i == k, v, meta)
    meta_ref[...] = meta


def _merge(x_p, x_s, y_p, y_s, a_p, a_s, wglu, bglu, gs, ga, wo1, wo2, g2, wr, br, tm):
    n_p, d = x_p.shape
    n_s = x_s.shape[0]
    npt, nst = n_p // tm, n_s // tm
    n_all = n_p + n_s
    tri = (lax.broadcasted_iota(jnp.int32, (tm, tm), 0) > lax.broadcasted_iota(jnp.int32, (tm, tm), 1)).astype(BF16)
    prow = lambda i: (jnp.minimum(i, npt - 1), 0)
    srow = lambda i: (jnp.maximum(i - npt, 0), 0)
    row = lambda i: (i, 0)
    const = lambda i: (0, 0)
    w = a_p.shape[1]
    nsl = y_p.shape[0]
    full = lambda a: pl.BlockSpec(a.shape, const)
    return pl.pallas_call(
        functools.partial(_merge_kernel, n_prompt_tiles=npt),
        grid=(npt + nst,),
        in_specs=[pl.BlockSpec((tm, d), prow), pl.BlockSpec((tm, d), srow),
                  pl.BlockSpec((nsl, tm, LANES), lambda i: (0, jnp.minimum(i, npt - 1), 0)),
                  pl.BlockSpec((nsl, tm, LANES), lambda i: (0, jnp.maximum(i - npt, 0), 0)),
                  pl.BlockSpec((tm, w), prow), pl.BlockSpec((tm, w), srow),
                  full(wglu), full(bglu), full(gs), full(ga), full(wo1), full(wo2), full(g2), full(wr), full(br),
                  full(tri)],
        out_specs=[pl.BlockSpec((tm, d), row), pl.BlockSpec((tm, d), row), pl.BlockSpec((tm, LANES), row),
                   pl.BlockSpec((1, LANES), const)],
        out_shape=[jax.ShapeDtypeStruct((n_all, d), F32), jax.ShapeDtypeStruct((n_all, d), F32),
                   jax.ShapeDtypeStruct((n_all, LANES), F32), jax.ShapeDtypeStruct((1, LANES), F32)],
        scratch_shapes=[pltpu.VMEM((1, LANES), F32)],
        compiler_params=_cparams(("arbitrary",)),
    )(x_p, x_s, y_p, y_s, a_p, a_s, wglu, bglu, gs, ga, wo1, wo2, g2, wr, br, tri)


def _start_row_gather(idx_ref, base, src_hbm, dst, sem, nrows):
    def body(r, _):
        pltpu.make_async_copy(src_hbm.at[pl.ds(idx_ref[base + r], 1)], dst.at[pl.ds(r, 1)], sem).start()
        return 0
    lax.fori_loop(0, nrows, body, 0, unroll=8)


def _wait_row_gather(src_hbm, dst, sem, nrows):
    pltpu.make_async_copy(src_hbm.at[pl.ds(0, nrows)], dst, sem).wait()


def _pos_kernel(meta_ref, offs_ref, p1_ref, p2_ref):
    meta = meta_ref[...]
    tm = meta.shape[0]
    lane_i = lax.broadcasted_iota(jnp.int32, (tm, LANES), 1)
    lane = lane_i.astype(F32)
    diag = lax.broadcasted_iota(jnp.int32, (tm, LANES), 0) % LANES == lane_i

    def dense(e, r):
        pos = jnp.sum(jnp.where(lane == e, offs_ref[...], 0.0), axis=-1, keepdims=True) + r
        spread = jnp.where(diag, pos, 0.0).reshape(tm // LANES, LANES, LANES)
        return jnp.sum(spread, axis=1).astype(jnp.int32)

    p1_ref[...] = dense(meta[:, 0:1], meta[:, 4:5])
    p2_ref[...] = dense(meta[:, 1:2], meta[:, 5:6])


def _positions(meta, offs_vec, tm):
    n = meta.shape[0]
    rows = tm // LANES
    return pl.pallas_call(
        _pos_kernel,
        grid=(n // tm,),
        in_specs=[pl.BlockSpec((tm, LANES), lambda i: (i, 0)), pl.BlockSpec((1, LANES), lambda i: (0, 0))],
        out_specs=[pl.BlockSpec((rows, LANES), lambda i: (i, 0)), pl.BlockSpec((rows, LANES), lambda i: (i, 0))],
        out_shape=[jax.ShapeDtypeStruct((n // LANES, LANES), jnp.int32)] * 2,
        compiler_params=_cparams(("parallel",)),
    )(meta, offs_vec)


def _dispatch_kernel(p1_ref, p2_ref, tend_ref, xn_ref, xs_hbm, zbuf, sem, *, tmd, tme):
    base = pl.program_id(0) * tmd

    @pl.when(pl.program_id(0) == 0)
    def _():
        zbuf[...] = jnp.zeros(zbuf.shape, zbuf.dtype)

        def last_tile(e, carry, wait):
            end = tend_ref[e]
            begin = jnp.where(e == 0, 0, tend_ref[jnp.maximum(e - 1, 0)])

            @pl.when(end > begin)
            def _():
                cp = pltpu.make_async_copy(zbuf, xs_hbm.at[pl.ds((end - 1) * tme, tme)], sem)
                cp.wait() if wait else cp.start()
            return carry

        def spare_tile(t, carry, wait):
            cp = pltpu.make_async_copy(zbuf, xs_hbm.at[pl.ds(t * tme, tme)], sem)
            cp.wait() if wait else cp.start()
            return carry

        n_used, n_tiles = tend_ref[N_EXPERTS - 1], xs_hbm.shape[0] // tme
        for wait in (False, True):
            lax.fori_loop(0, N_EXPERTS, functools.partial(last_tile, wait=wait), 0)
            lax.fori_loop(n_used, n_tiles, functools.partial(spare_tile, wait=wait), 0)

    def body(r, _):
        src = xn_ref.at[pl.ds(r, 1)]
        pltpu.make_async_copy(src, xs_hbm.at[pl.ds(p1_ref[base + r], 1)], sem).start()
        pltpu.make_async_copy(src, xs_hbm.at[pl.ds(p2_ref[base + r], 1)], sem).start()
        return 0

    lax.fori_loop(0, tmd, body, 0, unroll=8)
    for _ in range(2):
        pltpu.make_async_copy(xn_ref, xs_hbm.at[pl.ds(0, tmd)], sem).wait()


def _dispatch(pos1, pos2, tile_end, xn, n_rows, tmd, tme):
    n, d = xn.shape
    grid_spec = pltpu.PrefetchScalarGridSpec(
        num_scalar_prefetch=3,
        grid=(n // tmd,),
        in_specs=[pl.BlockSpec((tmd, d), lambda i, p1, p2, te: (i, 0))],
        out_specs=pl.BlockSpec(memory_space=pl.ANY),
        scratch_shapes=[pltpu.VMEM((tme, d), xn.dtype), pltpu.SemaphoreType.DMA],
    )
    return pl.pallas_call(
        functools.partial(_dispatch_kernel, tmd=tmd, tme=tme),
        grid_spec=grid_spec,
        out_shape=jax.ShapeDtypeStruct((n_rows, d), xn.dtype),
        compiler_params=_cparams(("arbitrary",)),
    )(pos1, pos2, tile_end, xn)


def _experts_kernel(te_ref, nu_ref, x_ref, wg_ref, wu_ref, wd_ref, o_ref, wg_scr, wu_scr, wd_scr):
    i = pl.program_id(0)

    @pl.when((i == 0) | (te_ref[i] != te_ref[jnp.maximum(i - 1, 0)]))
    def _():
        wg_scr[...] = wg_ref[0].astype(BF16)
        wu_scr[...] = wu_ref[0].astype(BF16)
        wd_scr[...] = wd_ref[0].astype(BF16)

    @pl.when(i < nu_ref[0])
    def _():
        x = x_ref[...].astype(BF16)
        hg = jnp.dot(x, wg_scr[...], preferred_element_type=F32)
        hu = jnp.dot(x, wu_scr[...], preferred_element_type=F32)
        hh = (jax.nn.silu(hg) * hu).astype(BF16)
        o_ref[...] = jnp.dot(hh, wd_scr[...], preferred_element_type=F32)

    @pl.when(i >= nu_ref[0])
    def _():
        o_ref[...] = jnp.zeros(o_ref.shape, F32)


def _experts(tile_expert, n_used, xs, w_gate, w_up, w_down, tme):
    nt = tile_expert.shape[0]
    ne, d, de = w_gate.shape
    wmap = lambda i, te, nu: (te[i], 0, 0)
    grid_spec = pltpu.PrefetchScalarGridSpec(
        num_scalar_prefetch=2,
        grid=(nt,),
        in_specs=[pl.BlockSpec((tme, d), lambda i, te, nu: (jnp.minimum(i, nu[0] - 1), 0)),
                  pl.BlockSpec((1, d, de), wmap), pl.BlockSpec((1, d, de), wmap), pl.BlockSpec((1, de, d), wmap)],
        out_specs=pl.BlockSpec((tme, d), lambda i, te, nu: (i, 0)),
        scratch_shapes=[pltpu.VMEM((d, de), BF16), pltpu.VMEM((d, de), BF16), pltpu.VMEM((de, d), BF16)],
    )
    return pl.pallas_call(
        _experts_kernel,
        grid_spec=grid_spec,
        out_shape=jax.ShapeDtypeStruct((nt * tme, d), F32),
        compiler_params=_cparams(("arbitrary",)),
    )(tile_expert, n_used, xs, w_gate, w_up, w_down)


def _combine_kernel(p1_ref, p2_ref, h_ref, meta_ref, eo_hbm, y_ref, buf, sems, *, tmc, nsteps):
    i = pl.program_id(0)
    slot = i % 2

    def start(step, slot_):
        _start_row_gather(p1_ref, step * tmc, eo_hbm, buf.at[slot_, pl.ds(0, tmc)], sems.at[slot_], tmc)
        _start_row_gather(p2_ref, step * tmc, eo_hbm, buf.at[slot_, pl.ds(tmc, tmc)], sems.at[slot_], tmc)

    @pl.when(i == 0)
    def _():
        start(0, 0)

    @pl.when(i + 1 < nsteps)
    def _():
        start(i + 1, 1 - slot)

    _wait_row_gather(eo_hbm, buf.at[slot], sems.at[slot], 2 * tmc)
    meta = meta_ref[...]
    y_ref[...] = h_ref[...] + meta[:, 2:3] * buf[slot, :tmc, :] + meta[:, 3:4] * buf[slot, tmc:, :]


def _combine(pos1, pos2, h_all, meta_all, eo, row0, tmc):
    n = pos1.shape[0]
    d = h_all.shape[1]
    nsteps = n // tmc
    off = row0 // tmc
    grid_spec = pltpu.PrefetchScalarGridSpec(
        num_scalar_prefetch=2,
        grid=(nsteps,),
        in_specs=[pl.BlockSpec((tmc, d), lambda i, p1, p2: (i + off, 0)),
                  pl.BlockSpec((tmc, LANES), lambda i, p1, p2: (i + off, 0)),
                  pl.BlockSpec(memory_space=pl.ANY)],
        out_specs=pl.BlockSpec((tmc, d), lambda i, p1, p2: (i, 0)),
        scratch_shapes=[pltpu.VMEM((2, 2 * tmc, d), F32), pltpu.SemaphoreType.DMA((2,))],
    )
    return pl.pallas_call(
        functools.partial(_combine_kernel, tmc=tmc, nsteps=nsteps),
        grid_spec=grid_spec,
        out_shape=jax.ShapeDtypeStruct((n, d), F32),
        compiler_params=_cparams(("arbitrary",)),
    )(pos1, pos2, h_all, meta_all, eo)


def _pick_tile(n, pref):
    t = min(pref, n)
    while n % t:
        t //= 2
    return t


def _layer(x_p, x_s, cache_ckv, cache_kr, st_re, st_im, page_table, p):
    nb_p, t_p, d = x_p.shape
    nb_s, t_s, _ = x_s.shape
    n_p, n_s = nb_p * t_p, nb_s * t_s
    past_len = page_table.shape[1] * PAGE
    g = p['ssm_a_re'].shape[0]
    ssm_w = g * SSM_CH
    q_rank = p['w_uq'].shape[0]
    kv_rank = p['w_uk'].shape[0]

    w_in = p['w_in']
    o3 = ssm_w + q_rank + kv_rank
    w_pad = jnp.zeros((d, o3 + LANES), F32).at[:, :o3].set(w_in[:, :o3])
    w_pad = w_pad.at[:, o3 + NOPE:o3 + QK].set(w_in[:, o3:]).astype(BF16)
    row2 = lambda v: v.reshape(1, -1).astype(F32)
    padq = lambda v: jnp.pad(v, (0, LANES - QK)).reshape(1, LANES)
    wq = jnp.pad(p['w_uq'].transpose(1, 0, 2), ((0, 0), (0, 0), (0, LANES - QK))).astype(BF16)
    wk = jnp.pad(p['w_uk'].transpose(1, 0, 2), ((0, 0), (0, 0), (0, LANES - NOPE))).astype(BF16)
    wvp = p['w_uv'].reshape(kv_rank, N_HEADS // 2, 2 * V_DIM).transpose(1, 0, 2).astype(BF16)
    wukt_all = p['w_uk'].reshape(kv_rank, N_HEADS * NOPE).T.astype(BF16)
    wukt = jnp.pad(p['w_uk'].transpose(1, 2, 0), ((0, 0), (0, LANES - NOPE), (0, 0))).astype(BF16)
    sel = (jnp.arange(LANES)[:, None] == NOPE + jnp.arange(ROPE)[None, :]).astype(BF16)
    gq, gk = padq(p['qk_norm_q']), padq(p['qk_norm_k'])
    ssm_params = (p['ssm_a_re'], p['ssm_a_im'], p['ssm_log_dt'], p['ssm_b_re'], p['ssm_b_im'],
                  p['ssm_c_re'], p['ssm_c_im'], p['ssm_d'])

    tm_p = _pick_tile(t_p, 512)
    tab_p = _rope_tables(jnp.arange(t_p))
    u, cq, ckv_p, krp_p = _in_proj(x_p.reshape(n_p, d), row2(p['norm1_g']), w_pad, row2(p['q_norm_g']),
                                   row2(p['kv_norm_g']), tab_p, tm_p, ssm_w, q_rank, kv_rank)
    lc_p = _pick_tile(t_p, 8)
    nsl = ssm_w // LANES
    y_ssm_p, hf_p = _s5(u, jnp.zeros((nsl, nb_p, 2 * GPB * SSM_P), F32), _s5_weights(*ssm_params, lc_p),
                        nb_p, t_p, lc_p)
    tq = _pick_tile(t_p, 256)
    att_p = _attn_prompt(cq, ckv_p, krp_p, tab_p, wq, wk, wvp, gq, gk, nb_p, t_p, tq, tq)

    tm_s = _pick_tile(n_s, 512)
    pos_s = past_len + jnp.arange(t_s)
    tab_s = tuple(jnp.tile(a, (tm_s // t_s, 1)) for a in _rope_tables(pos_s))
    u, cq, ckv_s, krp_s = _in_proj(x_s.reshape(n_s, d), row2(p['norm1_g']), w_pad, row2(p['q_norm_g']),
                                   row2(p['kv_norm_g']), tab_s, tm_s, ssm_w, q_rank, kv_rank)
    slab_state = lambda s: s.astype(F32).reshape(nb_s, nsl, GPB * SSM_P).transpose(1, 0, 2)
    h0 = jnp.concatenate([slab_state(st_re), slab_state(st_im)], axis=-1)
    y_ssm_s, hf_s = _s5(u, h0, _s5_weights(*ssm_params, t_s), nb_s, t_s, t_s)
    tab_q = tuple(jnp.tile(a, (n_s // tm_s, 1)) for a in tab_s)
    qa, qr = _q_sample(cq, tab_q, wq, wukt, sel, gq, gk)
    kr_s = krp_s[:, NOPE:QK]
    cnew = jnp.pad(ckv_s.reshape(nb_s, t_s, kv_rank), ((0, 0), (0, PAGE - t_s), (0, 0)))
    krnew = jnp.pad(kr_s.reshape(nb_s, t_s, ROPE), ((0, 0), (0, PAGE - t_s), (0, 0))).swapaxes(1, 2)
    pc = _pick_tile(page_table.shape[1], 16)
    att_s = _attn_sample(page_table, qa, qr, cnew, krnew, wukt_all, wvp, cache_ckv, jnp.swapaxes(cache_kr, 1, 2),
                         t_s, pc)

    tm = _pick_tile(math.gcd(n_p, n_s), 512)
    w_out = p['w_out'].astype(BF16)
    wr = jnp.zeros((d, LANES), F32).at[:, :N_GROUPS].set(p['w_router_group'])
    wr = wr.at[:, N_GROUPS:N_GROUPS + N_EXPERTS].set(p['w_router_expert'])
    br = jnp.zeros((1, LANES), F32).at[0, :N_GROUPS].set(p['b_router_group'])
    br = br.at[0, N_GROUPS:N_GROUPS + N_EXPERTS].set(p['b_router_expert'])
    h_all, xn_all, meta, cnt = _merge(
        x_p.reshape(n_p, d), x_s.reshape(n_s, d), y_ssm_p, y_ssm_s, att_p, att_s,
        p['ssm_w_glu'].astype(BF16), row2(p['ssm_b_glu']), row2(p['out_norm_ssm']), row2(p['out_norm_attn']),
        w_out[:ssm_w], w_out[ssm_w:], row2(p['norm2_g']), wr, br, tm)

    n_all = n_p + n_s
    tme = 256
    nt = (2 * n_all) // tme + N_EXPERTS
    counts = cnt[0, :N_EXPERTS].astype(jnp.int32)
    tiles_per = (counts + tme - 1) // tme
    tile_end = jnp.cumsum(tiles_per)
    tile_start = tile_end - tiles_per
    tile_id = jnp.arange(nt, dtype=jnp.int32)
    tile_expert = jnp.minimum(jnp.sum(tile_end[None, :] <= tile_id[:, None], axis=1), N_EXPERTS - 1).astype(jnp.int32)
    n_used = tile_end[-1:].astype(jnp.int32)
    offs_vec = jnp.zeros((1, LANES), F32).at[0, :N_EXPERTS].set((tile_start * tme).astype(F32))
    tmd = _pick_tile(n_all, 1024)
    pos1, pos2 = (a.reshape(-1) for a in _positions(meta, offs_vec, tmd))
    xs = _dispatch(pos1, pos2, tile_end.astype(jnp.int32), xn_all, nt * tme, tmd, tme)
    eo = _experts(tile_expert, n_used, xs, p['w_gate'], p['w_up'], p['w_down'], tme)

    tmc = _pick_tile(math.gcd(n_p, n_s), 256)
    y_p = _combine(pos1[:n_p], pos2[:n_p], h_all, meta, eo, 0, tmc)
    y_s = _combine(pos1[n_p:], pos2[n_p:], h_all, meta, eo, n_p, tmc)

    n_pg = t_p // PAGE

    def split(hf):
        nb = hf.shape[1]
        half = GPB * SSM_P
        unslab = lambda a: a.reshape(nsl, nb, GPB, SSM_P).transpose(1, 0, 2, 3).reshape(nb, g, SSM_P)
        return unslab(hf[..., :half]), unslab(hf[..., half:])

    hp_re, hp_im = split(hf_p)
    hs_re, hs_im = split(hf_s)
    return (y_p.reshape(nb_p, t_p, d), y_s.reshape(nb_s, t_s, d),
            ckv_p.reshape(nb_p, n_pg, PAGE, kv_rank), krp_p[:, NOPE:QK].reshape(nb_p, n_pg, PAGE, ROPE),
            hp_re, hp_im, ckv_s.reshape(nb_s, t_s, kv_rank), kr_s.reshape(nb_s, t_s, ROPE), hs_re, hs_im)


_PARAM_NAMES = ('norm1_g', 'w_in', 'ssm_a_re', 'ssm_a_im', 'ssm_log_dt', 'ssm_b_re', 'ssm_b_im', 'ssm_c_re',
                'ssm_c_im', 'ssm_d', 'ssm_w_glu', 'ssm_b_glu', 'q_norm_g', 'w_uq', 'kv_norm_g', 'w_uk', 'w_uv',
                'qk_norm_q', 'qk_norm_k', 'out_norm_ssm', 'out_norm_attn', 'w_out', 'norm2_g', 'w_router_group',
                'b_router_group', 'w_router_expert', 'b_router_expert', 'w_gate', 'w_up', 'w_down')


def kernel(x_prompt, x_sample, cache_ckv, cache_krope, state_ssm_re, state_ssm_im, page_table, norm1_g, w_in, ssm_a_re, ssm_a_im, ssm_log_dt, ssm_b_re, ssm_b_im, ssm_c_re, ssm_c_im, ssm_d, ssm_w_glu, ssm_b_glu, q_norm_g, w_uq, kv_norm_g, w_uk, w_uv, qk_norm_q, qk_norm_k, out_norm_ssm, out_norm_attn, w_out, norm2_g, w_router_group, b_router_group, w_router_expert, b_router_expert, w_gate, w_up, w_down):
    params = (norm1_g, w_in, ssm_a_re, ssm_a_im, ssm_log_dt, ssm_b_re, ssm_b_im, ssm_c_re, ssm_c_im, ssm_d,
              ssm_w_glu, ssm_b_glu, q_norm_g, w_uq, kv_norm_g, w_uk, w_uv, qk_norm_q, qk_norm_k, out_norm_ssm,
              out_norm_attn, w_out, norm2_g, w_router_group, b_router_group, w_router_expert, b_router_expert,
              w_gate, w_up, w_down)
    depth = w_in.shape[0]
    h_p, h_s = x_prompt, x_sample
    outs = [[] for _ in range(8)]
    for layer in range(depth):
        p = {k: v[layer] for k, v in zip(_PARAM_NAMES, params)}
        res = _layer(h_p, h_s, cache_ckv[layer], cache_krope[layer], state_ssm_re[layer], state_ssm_im[layer],
                     page_table, p)
        h_p, h_s = res[0], res[1]
        for acc, r in zip(outs, res[2:]):
            acc.append(r)
    return (h_p, h_s) + tuple(jnp.stack(o) for o in outs)
```

```python
import functools
import math

import jax
import jax.numpy as jnp
from jax import lax
from jax.experimental import pallas as pl
from jax.experimental.pallas import tpu as pltpu

F32 = jnp.float32
BF16 = jnp.bfloat16
HIGHEST = lax.Precision.HIGHEST

LANES = 128
SUBLANES = 8
VMEM_LIMIT = 56 * 1024 * 1024

EPS = 1e-6
ROPE_BASE = 10000.0
PAGE = 128
SSM_CH = 16
SSM_P = 64
GPB = LANES // SSM_CH
S5_MAX_TOKENS = 8192
N_HEADS = 8
NOPE = 64
ROPE = 32
QK = NOPE + ROPE
V_DIM = 64
N_GROUPS = 4
EXP_PER_GROUP = 8
N_EXPERTS = N_GROUPS * EXP_PER_GROUP
ATTN_HEADS_PER_BODY = 8

NT_DIMS = (((1,), (1,)), ((), ()))
LOG2E = math.log2(math.e)


def _cparams(sem, vmem=VMEM_LIMIT):
    return pltpu.CompilerParams(dimension_semantics=sem, vmem_limit_bytes=vmem)


def _rms(x, g):
    return x * lax.rsqrt(jnp.mean(x * x, axis=-1, keepdims=True) + EPS) * g


def _rms_qk(x, g):
    return x * lax.rsqrt(jnp.sum(x * x, axis=-1, keepdims=True) * (1.0 / QK) + EPS) * g


def _rope(x, cos, sinp, sinm):
    return x * cos + pltpu.roll(x, 16, 1) * sinp + pltpu.roll(x, LANES - 16, 1) * sinm


def _rope_tables(pos):
    inv_freq = ROPE_BASE ** (-jnp.arange(0, ROPE, 2, dtype=F32) / ROPE)
    ang = pos.astype(F32)[:, None] * inv_freq[None, :]
    c, s = jnp.cos(ang), jnp.sin(ang)
    t = pos.shape[0]
    z = lambda n: jnp.zeros((t, n), F32)
    cosf = jnp.concatenate([jnp.ones((t, NOPE), F32), c, c, z(LANES - QK)], axis=1)
    sinp = jnp.concatenate([z(NOPE + ROPE // 2), s, z(LANES - QK)], axis=1)
    sinm = jnp.concatenate([z(NOPE), -s, z(LANES - QK + ROPE // 2)], axis=1)
    return cosf, sinp, sinm


def _in_proj_kernel(x_ref, g1_ref, w_ref, gq_ref, gkv_ref, cos_ref, sinp_ref, sinm_ref,
                    u_ref, cq_ref, ckv_ref, kr_ref, *, ssm_w, q_rank, kv_rank):
    n = _rms(x_ref[...], g1_ref[...])
    z = jnp.dot(n.astype(BF16), w_ref[...], preferred_element_type=F32)
    o1, o2, o3 = ssm_w, ssm_w + q_rank, ssm_w + q_rank + kv_rank
    for j in range(ssm_w // LANES):
        u_ref[j] = z[:, j * LANES:(j + 1) * LANES]
    cq_ref[...] = _rms(z[:, o1:o2], gq_ref[...]).astype(cq_ref.dtype)
    ckv_ref[...] = _rms(z[:, o2:o3], gkv_ref[...])
    kr_ref[...] = _rope(z[:, o3:o3 + LANES], cos_ref[...], sinp_ref[...], sinm_ref[...])


def _in_proj(x, g1, w_pad, gq, gkv, tables, tm, ssm_w, q_rank, kv_rank):
    n, d = x.shape
    period = tables[0].shape[0]
    nper = period // tm
    row = lambda i: (i, 0)
    const = lambda i: (0, 0)
    tab = pl.BlockSpec((tm, LANES), lambda i: (i % nper, 0))
    return pl.pallas_call(
        functools.partial(_in_proj_kernel, ssm_w=ssm_w, q_rank=q_rank, kv_rank=kv_rank),
        grid=(n // tm,),
        in_specs=[pl.BlockSpec((tm, d), row), pl.BlockSpec((1, d), const),
                  pl.BlockSpec(w_pad.shape, const), pl.BlockSpec((1, q_rank), const),
                  pl.BlockSpec((1, kv_rank), const), tab, tab, tab],
        out_specs=[pl.BlockSpec((ssm_w // LANES, tm, LANES), lambda i: (0, i, 0)), pl.BlockSpec((tm, q_rank), row),
                   pl.BlockSpec((tm, kv_rank), row), pl.BlockSpec((tm, LANES), row)],
        out_shape=[jax.ShapeDtypeStruct((ssm_w // LANES, n, LANES), F32), jax.ShapeDtypeStruct((n, q_rank), BF16),
                   jax.ShapeDtypeStruct((n, kv_rank), F32), jax.ShapeDtypeStruct((n, LANES), F32)],
        compiler_params=_cparams(("parallel",)), name="in_proj",
    )(x, g1, w_pad, gq, gkv, *tables)


def _s5_weights(a_re, a_im, log_dt, b_re, b_im, c_re, c_im, d, lc):
    g, p = a_re.shape
    dt = jnp.exp(log_dt)[:, None]
    den = a_re * a_re + a_im * a_im
    mag, ang = jnp.exp(a_re * dt), a_im * dt
    lb_re, lb_im = mag * jnp.cos(ang), mag * jnp.sin(ang)
    num_re = lb_re - 1.0
    coef_re = (num_re * a_re + lb_im * a_im) / den
    coef_im = (lb_im * a_re - num_re * a_im) / den
    bb_re = coef_re[..., None] * b_re - coef_im[..., None] * b_im
    bb_im = coef_re[..., None] * b_im + coef_im[..., None] * b_re

    def lam_pow(k):
        kk = k.astype(F32)[:, None, None]
        m = jnp.exp(a_re * dt * kk)
        return m * jnp.cos(ang * kk), m * jnp.sin(ang * kk)

    steps = jnp.arange(lc)
    pr, pi = lam_pow(lc - 1 - steps)
    bz_re = pr[..., None] * bb_re[None] - pi[..., None] * bb_im[None]
    bz_im = pr[..., None] * bb_im[None] + pi[..., None] * bb_re[None]
    bz = jnp.concatenate([bz_re, bz_im], axis=2)
    bz = bz.transpose(1, 0, 3, 2).reshape(g, lc * SSM_CH, 2 * p)
    qr, qi = lam_pow(steps + 1)
    cl_re = c_re[None] * qr[:, :, None, :] - c_im[None] * qi[:, :, None, :]
    cl_im = c_re[None] * qi[:, :, None, :] + c_im[None] * qr[:, :, None, :]
    cz = jnp.concatenate([cl_re, -cl_im], axis=3)
    cz = cz.transpose(1, 3, 0, 2).reshape(g, 2 * p, lc * SSM_CH)
    tr, ti = lam_pow(steps)
    cb_re = jnp.einsum('gcp,kgp,gpd->gkcd', c_re, tr, bb_re, precision=HIGHEST)
    cb_re -= jnp.einsum('gcp,kgp,gpd->gkcd', c_re, ti, bb_im, precision=HIGHEST)
    cb_re -= jnp.einsum('gcp,kgp,gpd->gkcd', c_im, tr, bb_im, precision=HIGHEST)
    cb_re -= jnp.einsum('gcp,kgp,gpd->gkcd', c_im, ti, bb_re, precision=HIGHEST)
    tau = steps[None, :] - steps[:, None]
    ksel = cb_re[:, jnp.clip(tau, 0, lc - 1)]
    ksel = jnp.where((tau >= 0)[None, :, :, None, None], ksel, 0.0)
    tz = ksel.transpose(0, 1, 4, 2, 3).reshape(g, lc * SSM_CH, lc * SSM_CH)
    nsl = g // GPB
    eye = jnp.eye(GPB, dtype=F32)
    bd = lambda a: a[..., None, :] * eye.reshape(1, 1, GPB, 1, 1, GPB, 1)
    bzg = bz.reshape(nsl, GPB, lc, SSM_CH, 2, p).transpose(0, 2, 1, 3, 4, 5)
    bz_s = bd(bzg).reshape(nsl, lc * LANES, 2 * GPB * p)
    tzg = tz.reshape(nsl, GPB, lc, SSM_CH, lc, SSM_CH).transpose(0, 2, 1, 3, 4, 5)
    wu = bd(tzg).reshape(nsl, lc * LANES, lc * LANES)
    czg = cz.reshape(nsl, GPB, 2, p, lc, SSM_CH).transpose(0, 2, 1, 3, 4, 5)
    wh = bd(czg).reshape(nsl, 2 * GPB * p, lc * LANES)
    wy = jnp.concatenate([wh, wu], axis=1)
    lr, li = lam_pow(jnp.array([lc]))
    lam_re = lr[0].reshape(nsl, 1, GPB * p)
    lam_im = li[0].reshape(nsl, 1, GPB * p)
    dvec = jnp.tile(d.reshape(nsl, 1, LANES), (1, 1, lc))
    return bz_s.astype(BF16), wy.astype(BF16), lam_re, lam_im, dvec


def _s5_kernel(u_ref, bz_ref, wy_ref, lr_ref, li_ref, d_ref, h0_ref, y_ref, hf_ref, s_scr, *, nk, bs, lc, rb):
    nh = s_scr.shape[0] // 2
    blk = lambda a, c: a[:, c * LANES:(c + 1) * LANES]

    def ucat(r0):
        return jnp.concatenate([u_ref[0, pl.ds(r0 * lc + t, rb, stride=lc), :] for t in range(lc)], axis=1)

    def phase1(i, _):
        r0 = pl.multiple_of(i * rb, rb)
        s = jnp.dot(ucat(r0).astype(BF16), bz_ref[0], preferred_element_type=F32)
        for c in range(2 * nh):
            s_scr[c, pl.ds(r0, rb), :] = blk(s, c)
        return 0

    lax.fori_loop(0, (bs * nk) // rb, phase1, 0)
    lr, li = lr_ref[0], li_ref[0]

    def step(k, h):
        rows = pl.ds(k, bs, stride=nk)
        new = [None] * (2 * nh)
        for c in range(nh):
            h_re, h_im = h[c], h[nh + c]
            s_re, s_im = s_scr[c, rows, :], s_scr[nh + c, rows, :]
            s_scr[c, rows, :] = h_re
            s_scr[nh + c, rows, :] = h_im
            new[c] = blk(lr, c) * h_re - blk(li, c) * h_im + s_re
            new[nh + c] = blk(lr, c) * h_im + blk(li, c) * h_re + s_im
        return tuple(new)

    h0 = h0_ref[0, 0]
    hf = lax.fori_loop(0, nk, step, tuple(blk(h0, c) for c in range(2 * nh)))
    hf_ref[0, 0] = jnp.concatenate(hf, axis=1)

    def phase3(i, _):
        r0 = pl.multiple_of(i * rb, rb)
        uc = ucat(r0)
        hprev = jnp.concatenate([s_scr[c, pl.ds(r0, rb), :] for c in range(2 * nh)], axis=1)
        lhs = jnp.concatenate([hprev.astype(BF16), uc.astype(BF16)], axis=1)
        y = jnp.dot(lhs, wy_ref[0], preferred_element_type=F32) + d_ref[0] * uc
        for t in range(lc):
            y_ref[0, pl.ds(r0 * lc + t, rb, stride=lc), :] = y[:, t * LANES:(t + 1) * LANES]
        return 0

    lax.fori_loop(0, (bs * nk) // rb, phase3, 0)


def _s5(u, h0, weights, nb, t, lc):
    bz, wy, lam_re, lam_im, dvec = weights
    nsl = u.shape[0]
    nk = t // lc
    sw = h0.shape[-1]
    bs = nb
    while bs * t > S5_MAX_TOKENS and bs % 2 == 0:
        bs //= 2
    rb = _pick_tile(bs * nk, 256)
    h0 = h0.reshape(nsl, nb // bs, bs, sw)
    slab = lambda j, b: (j, 0, 0)
    tok = lambda j, b: (j, b, 0)
    st = lambda j, b: (j, b, 0, 0)
    y, hf = pl.pallas_call(
        functools.partial(_s5_kernel, nk=nk, bs=bs, lc=lc, rb=rb),
        grid=(nsl, nb // bs),
        in_specs=[pl.BlockSpec((1, bs * t, LANES), tok), pl.BlockSpec((1,) + bz.shape[1:], slab),
                  pl.BlockSpec((1,) + wy.shape[1:], slab), pl.BlockSpec((1, 1, sw // 2), slab),
                  pl.BlockSpec((1, 1, sw // 2), slab), pl.BlockSpec((1, 1, lc * LANES), slab),
                  pl.BlockSpec((1, 1, bs, sw), st)],
        out_specs=[pl.BlockSpec((1, bs * t, LANES), tok), pl.BlockSpec((1, 1, bs, sw), st)],
        out_shape=[jax.ShapeDtypeStruct(u.shape, F32), jax.ShapeDtypeStruct(h0.shape, F32)],
        scratch_shapes=[pltpu.VMEM((sw // LANES, bs * nk, LANES), F32)],
        compiler_params=_cparams(("parallel", "parallel")), name="s5",
    )(u, bz, wy, lam_re, lam_im, dvec, h0)
    return y, hf.reshape(nsl, nb, sw)


def _ones_lane(h):
    return V_DIM if h % 2 == 0 else 0


def _attn_prompt_kernel(cq_ref, ckv_ref, krp_ref, cos_ref, sinp_ref, sinm_ref, wq_ref, wk_ref, wv_ref,
                        gq_ref, gk_ref, o_ref, k_scr, v_scr, q_scr, *, tq, tk, hb):
    qi = pl.program_id(1)

    @pl.when(qi == 0)
    def _():
        c = ckv_ref[...].astype(BF16)
        krp = krp_ref[...]
        for h in range(N_HEADS):
            kh = jnp.dot(c, wk_ref[h], preferred_element_type=F32) + krp
            k_scr[h] = _rms_qk(kh, gk_ref[...]).astype(BF16)
        vlane = lax.broadcasted_iota(jnp.int32, (1, LANES), 1)
        for h in range(N_HEADS):
            one = jnp.where(vlane == _ones_lane(h), 1.0, 0.0)
            v_scr[h] = (jnp.dot(c, wv_ref[h], preferred_element_type=F32) + one).astype(BF16)

    cq = cq_ref[...]
    cos, sinp, sinm = cos_ref[...], sinp_ref[...], sinm_ref[...]
    for h in range(N_HEADS):
        q = _rope(jnp.dot(cq, wq_ref[h], preferred_element_type=F32), cos, sinp, sinm)
        q_scr[h] = (_rms_qk(q, gq_ref[...]) * (LOG2E / math.sqrt(QK))).astype(BF16)

    row = qi * tq + lax.broadcasted_iota(jnp.int32, (tq, tk), 0)
    col = lax.broadcasted_iota(jnp.int32, (tq, tk), 1)
    lane = lax.broadcasted_iota(jnp.int32, (tq, LANES), 1)
    n_full = qi * (tq // tk)

    def kv_step(j, carry, masked, heads):
        ks = pl.ds(pl.multiple_of(j * tk, tk), tk)
        out = []
        for (m, acc), h in zip(carry, heads):
            s = lax.dot_general(q_scr[h], k_scr[h, ks, :], NT_DIMS, preferred_element_type=F32)
            if masked:
                s = jnp.where(col + j * tk <= row, s, -jnp.inf)
            m_new = jnp.maximum(m, jnp.max(s, axis=-1, keepdims=True))
            p = jnp.exp2(s - m_new)
            acc = acc * jnp.exp2(m - m_new) + jnp.dot(p.astype(BF16), v_scr[h, ks, :], preferred_element_type=F32)
            out.append((m_new, acc))
        return tuple(out)

    for h0 in range(0, N_HEADS, hb):
        heads = tuple(range(h0, h0 + hb))
        init = (jnp.full((tq, 1), -jnp.inf, F32), jnp.zeros((tq, LANES), F32))
        carry = lax.fori_loop(0, n_full, functools.partial(kv_step, masked=False, heads=heads), (init,) * hb)
        for jj in range(tq // tk):
            carry = kv_step(n_full + jj, carry, True, heads)
        outs = [acc / acc[:, _ones_lane(h):_ones_lane(h) + 1] for (_, acc), h in zip(carry, heads)]
        for k in range(0, hb, 2):
            hp = (h0 + k) // 2
            o_ref[:, hp * LANES:(hp + 1) * LANES] = jnp.where(lane < V_DIM, outs[k], outs[k + 1])


def _attn_prompt(cq, ckv, krp, tables, wq, wk, wvp, gq, gk, nb, t, tq, tk, hb=ATTN_HEADS_PER_BODY):
    n = cq.shape[0]
    nq = t // tq
    qrow = lambda b, i: (b * nq + i, 0)
    seq = lambda b, i: (b, 0)
    tab = pl.BlockSpec((tq, LANES), lambda b, i: (i, 0))
    c3 = lambda b, i: (0, 0, 0)
    c2 = lambda b, i: (0, 0)
    return pl.pallas_call(
        functools.partial(_attn_prompt_kernel, tq=tq, tk=tk, hb=hb),
        grid=(nb, nq),
        in_specs=[pl.BlockSpec((tq, cq.shape[1]), qrow), pl.BlockSpec((t, ckv.shape[1]), seq),
                  pl.BlockSpec((t, LANES), seq), tab, tab, tab,
                  pl.BlockSpec(wq.shape, c3), pl.BlockSpec(wk.shape, c3), pl.BlockSpec(wvp.shape, c3),
                  pl.BlockSpec((1, LANES), c2), pl.BlockSpec((1, LANES), c2)],
        out_specs=pl.BlockSpec((tq, N_HEADS * V_DIM), qrow),
        out_shape=jax.ShapeDtypeStruct((n, N_HEADS * V_DIM), F32),
        scratch_shapes=[pltpu.VMEM((N_HEADS, t, LANES), BF16), pltpu.VMEM((N_HEADS, t, LANES), BF16),
                        pltpu.VMEM((N_HEADS, tq, LANES), BF16)],
        compiler_params=_cparams(("parallel", "arbitrary")), name="attn_prompt",
    )(cq, ckv, krp, *tables, wq, wk, wvp, gq, gk)


def _q_sample_kernel(cq_ref, cos_ref, sinp_ref, sinm_ref, wq_ref, wukt_ref, sel_ref, gq_ref, gk_ref,
                     qa_ref, qr_ref):
    cq = cq_ref[...]
    for h in range(N_HEADS):
        q = _rope(jnp.dot(cq, wq_ref[h], preferred_element_type=F32), cos_ref[...], sinp_ref[...], sinm_ref[...])
        q = _rms_qk(q, gq_ref[...]) * (1.0 / math.sqrt(QK))
        qk = (q * gk_ref[...]).astype(BF16)
        qa_ref[h] = jnp.dot(qk, wukt_ref[h], preferred_element_type=F32).astype(BF16)
        qr_ref[h] = jnp.dot(qk, sel_ref[...], preferred_element_type=F32).astype(BF16)


def _q_sample(cq, tables, wq, wukt, sel, gq, gk):
    n = cq.shape[0]
    return pl.pallas_call(
        _q_sample_kernel,
        out_shape=[jax.ShapeDtypeStruct((N_HEADS, n, LANES), BF16), jax.ShapeDtypeStruct((N_HEADS, n, ROPE), BF16)],
        compiler_params=pltpu.CompilerParams(vmem_limit_bytes=VMEM_LIMIT), name="q_sample",
    )(cq, *tables, wq, wukt, sel, gq, gk)


def _attn_sample_kernel(pt_ref, qa_ref, qr_ref, cnew_ref, krnew_ref, wukt_ref, wv_ref, ckv_hbm, kr_hbm,
                        o_ref, cbuf, kbuf, sems, m_scr, l_scr, acc_scr, *, pc, nc, npg, t_dec):
    b, c = pl.program_id(0), pl.program_id(1)
    g = b * nc + c
    slot = g % 2
    nkeys = pc * PAGE
    rows = N_HEADS * t_dec

    def page_copies(step, slot_):
        base = step * pc
        cps = []
        for j in range(pc):
            pg = pt_ref[base + j]
            dst = pl.ds(j * PAGE, PAGE)
            cps.append(pltpu.make_async_copy(ckv_hbm.at[pg], cbuf.at[slot_, dst, :], sems.at[0, slot_]))
            cps.append(pltpu.make_async_copy(kr_hbm.at[pg], kbuf.at[slot_, :, dst], sems.at[1, slot_]))
        return cps

    @pl.when(g == 0)
    def _():
        for cp in page_copies(g, slot):
            cp.start()

    @pl.when(g + 1 < pl.num_programs(0) * nc)
    def _():
        for cp in page_copies(g + 1, 1 - slot):
            cp.start()

    @pl.when(c == 0)
    def _():
        m_scr[...] = jnp.full(m_scr.shape, -jnp.inf, F32)
        l_scr[...] = jnp.zeros(l_scr.shape, F32)
        acc_scr[...] = jnp.zeros(acc_scr.shape, F32)

    qa = qa_ref[...].reshape(rows, LANES)
    qr = qr_ref[...].reshape(rows, ROPE)

    def attend(cf, krt, mask):
        nk = cf.shape[0]
        cb = cf.astype(BF16)
        knt = lax.dot_general(wukt_ref[...], cb, NT_DIMS, preferred_element_type=F32)
        n2 = jnp.sum((knt * knt).reshape(N_HEADS, NOPE, nk), axis=1)
        kr2 = jnp.sum(krt * krt, axis=0, keepdims=True)
        rinv = lax.rsqrt((n2 + kr2) * (1.0 / QK) + EPS)
        s = lax.dot_general(qa, cb, NT_DIMS, preferred_element_type=F32)
        s += jnp.dot(qr, krt.astype(BF16), preferred_element_type=F32)
        s = (s.reshape(N_HEADS, t_dec, nk) * rinv[:, None, :]).reshape(rows, nk)
        if mask is not None:
            s = jnp.where(mask, s, -jnp.inf)
        m = m_scr[...]
        m_new = jnp.maximum(m, jnp.max(s, axis=-1, keepdims=True))
        p = jnp.exp(s - m_new)
        corr = jnp.exp(m - m_new)
        l_scr[...] = l_scr[...] * corr + jnp.sum(p, axis=-1, keepdims=True)
        acc_scr[...] = acc_scr[...] * corr + jnp.dot(p.astype(BF16), cb, preferred_element_type=F32)
        m_scr[...] = m_new

    for cp in page_copies(g, slot):
        cp.wait()
    attend(cbuf[slot], kbuf[slot], None)

    @pl.when(c == nc - 1)
    def _():
        key = lax.broadcasted_iota(jnp.int32, (rows, PAGE), 1)
        step = lax.broadcasted_iota(jnp.int32, (rows, PAGE), 0) % t_dec
        attend(cnew_ref[0], krnew_ref[0], key <= step)
        o_lat = (acc_scr[...] / l_scr[...]).astype(BF16)
        lane = lax.broadcasted_iota(jnp.int32, (t_dec, LANES), 1)
        for hp in range(N_HEADS // 2):
            lo = jnp.dot(o_lat[(2 * hp) * t_dec:(2 * hp + 1) * t_dec], wv_ref[hp], preferred_element_type=F32)
            hi = jnp.dot(o_lat[(2 * hp + 1) * t_dec:(2 * hp + 2) * t_dec], wv_ref[hp], preferred_element_type=F32)
            o_ref[:, hp * LANES:(hp + 1) * LANES] = jnp.where(lane < V_DIM, lo, hi)


def _attn_sample(page_table, qa, qr, cnew_pad, krnew_pad, wukt_all, wvp, cache_ckv, cache_kr, t_dec, pc):
    nb, npg = page_table.shape
    nc = npg // pc
    kv_rank = cache_ckv.shape[-1]
    perb3 = lambda b, c, pt: (0, b, 0)
    new3 = lambda b, c, pt: (b, 0, 0)
    grid_spec = pltpu.PrefetchScalarGridSpec(
        num_scalar_prefetch=1,
        grid=(nb, nc),
        in_specs=[pl.BlockSpec((N_HEADS, t_dec, LANES), perb3), pl.BlockSpec((N_HEADS, t_dec, ROPE), perb3),
                  pl.BlockSpec((1, PAGE, kv_rank), new3), pl.BlockSpec((1, ROPE, PAGE), new3),
                  pl.BlockSpec(wukt_all.shape, lambda b, c, pt: (0, 0)),
                  pl.BlockSpec(wvp.shape, lambda b, c, pt: (0, 0, 0)),
                  pl.BlockSpec(memory_space=pl.ANY), pl.BlockSpec(memory_space=pl.ANY)],
        out_specs=pl.BlockSpec((t_dec, N_HEADS * V_DIM), lambda b, c, pt: (b, 0)),
        scratch_shapes=[pltpu.VMEM((2, pc * PAGE, kv_rank), F32), pltpu.VMEM((2, ROPE, pc * PAGE), F32),
                        pltpu.SemaphoreType.DMA((2, 2)),
                        pltpu.VMEM((N_HEADS * t_dec, 1), F32), pltpu.VMEM((N_HEADS * t_dec, 1), F32),
                        pltpu.VMEM((N_HEADS * t_dec, kv_rank), F32)],
    )
    return pl.pallas_call(
        functools.partial(_attn_sample_kernel, pc=pc, nc=nc, npg=npg, t_dec=t_dec),
        grid_spec=grid_spec,
        out_shape=jax.ShapeDtypeStruct((nb * t_dec, N_HEADS * V_DIM), F32),
        compiler_params=_cparams(("arbitrary", "arbitrary")), name="attn_sample",
    )(page_table.reshape(-1), qa, qr, cnew_pad, krnew_pad, wukt_all, wvp, cache_ckv, cache_kr)


def _merge_kernel(xp_ref, xs_ref, yp_ref, ys_ref, ap_ref, as_ref, wglu_ref, bglu_ref, gs_ref, ga_ref,
                  wo1_ref, wo2_ref, g2_ref, wr_ref, br_ref, tri_ref,
                  h_ref, xn_ref, meta_ref, cnt_ref, run_scr, *, n_prompt_tiles):
    i = pl.program_id(0)

    @pl.when(i == 0)
    def _():
        run_scr[...] = jnp.zeros(run_scr.shape, F32)

    is_p = i < n_prompt_tiles
    x = jnp.where(is_p, xp_ref[...], xs_ref[...])
    slabs = lambda ref: jnp.concatenate([ref[j] for j in range(ref.shape[0])], axis=1)
    y = jnp.where(is_p, slabs(yp_ref), slabs(ys_ref))
    att = jnp.where(is_p, ap_ref[...], as_ref[...])

    y = jax.nn.gelu(y)
    glu = jnp.dot(y.astype(BF16), wglu_ref[...], preferred_element_type=F32) + bglu_ref[...]
    ssm = y * jax.nn.sigmoid(glu)
    mix = jnp.dot(_rms(ssm, gs_ref[...]).astype(BF16), wo1_ref[...], preferred_element_type=F32)
    mix += jnp.dot(_rms(att, ga_ref[...]).astype(BF16), wo2_ref[...], preferred_element_type=F32)
    h = x + mix
    h_ref[...] = h
    xn = _rms(h, g2_ref[...])
    xn_ref[...] = xn

    tm = x.shape[0]
    logits = jnp.dot(xn, wr_ref[...], precision=HIGHEST, preferred_element_type=F32) + br_ref[...]
    lane_i = lax.broadcasted_iota(jnp.int32, (tm, LANES), 1)
    lane = lane_i.astype(F32)
    big = float(LANES)
    first = lambda hit: jnp.min(jnp.where(hit, lane, big), axis=-1, keepdims=True)
    gl = jnp.where(lane_i < N_GROUPS, logits, -jnp.inf)
    gmax = jnp.max(gl, axis=-1, keepdims=True)
    grp = first(gl == gmax)
    p_sel = 1.0 / jnp.sum(jnp.exp(gl - gmax), axis=-1, keepdims=True)
    lane_grp = ((lane_i - N_GROUPS) >> 3).astype(F32)
    in_grp = (lane_i >= N_GROUPS) & (lane_i < N_GROUPS + N_EXPERTS) & (lane_grp == grp)
    el = jnp.where(in_grp, logits, -jnp.inf)
    m1 = jnp.max(el, axis=-1, keepdims=True)
    i1 = first(el == m1)
    el2 = jnp.where(lane == i1, -jnp.inf, el)
    m2 = jnp.max(el2, axis=-1, keepdims=True)
    i2 = first(el2 == m2)
    e21 = jnp.exp(m2 - m1)
    w1 = p_sel / (1.0 + e21)
    w2 = p_sel * e21 / (1.0 + e21)
    e1, e2 = i1 - N_GROUPS, i2 - N_GROUPS

    oh1, oh2 = lane == e1, lane == e2
    oh = jnp.where(oh1 | oh2, 1.0, 0.0)
    before = jnp.dot(tri_ref[...], oh.astype(BF16), preferred_element_type=F32) + run_scr[...]
    r1 = jnp.sum(jnp.where(oh1, before, 0.0), axis=-1, keepdims=True)
    r2 = jnp.sum(jnp.where(oh2, before, 0.0), axis=-1, keepdims=True)
    run_scr[...] += jnp.sum(oh, axis=0, keepdims=True)
    cnt_ref[...] = run_scr[...]
    meta = jnp.zeros((tm, LANES), F32)
    for k, v in enumerate((e1, e2, w1, w2, r1, r2)):
        meta = jnp.where(lane_i == k, v, meta)
    meta_ref[...] = meta


def _merge(x_p, x_s, y_p, y_s, a_p, a_s, wglu, bglu, gs, ga, wo1, wo2, g2, wr, br, tm):
    n_p, d = x_p.shape
    n_s = x_s.shape[0]
    npt, nst = n_p // tm, n_s // tm
    n_all = n_p + n_s
    tri = (lax.broadcasted_iota(jnp.int32, (tm, tm), 0) > lax.broadcasted_iota(jnp.int32, (tm, tm), 1)).astype(BF16)
    prow = lambda i: (jnp.minimum(i, npt - 1), 0)
    srow = lambda i: (jnp.maximum(i - npt, 0), 0)
    row = lambda i: (i, 0)
    const = lambda i: (0, 0)
    w = a_p.shape[1]
    nsl = y_p.shape[0]
    full = lambda a: pl.BlockSpec(a.shape, const)
    return pl.pallas_call(
        functools.partial(_merge_kernel, n_prompt_tiles=npt),
        grid=(npt + nst,),
        in_specs=[pl.BlockSpec((tm, d), prow), pl.BlockSpec((tm, d), srow),
                  pl.BlockSpec((nsl, tm, LANES), lambda i: (0, jnp.minimum(i, npt - 1), 0)),
                  pl.BlockSpec((nsl, tm, LANES), lambda i: (0, jnp.maximum(i - npt, 0), 0)),
                  pl.BlockSpec((tm, w), prow), pl.BlockSpec((tm, w), srow),
                  full(wglu), full(bglu), full(gs), full(ga), full(wo1), full(wo2), full(g2), full(wr), full(br),
                  full(tri)],
        out_specs=[pl.BlockSpec((tm, d), row), pl.BlockSpec((tm, d), row), pl.BlockSpec((tm, LANES), row),
                   pl.BlockSpec((1, LANES), const)],
        out_shape=[jax.ShapeDtypeStruct((n_all, d), F32), jax.ShapeDtypeStruct((n_all, d), F32),
                   jax.ShapeDtypeStruct((n_all, LANES), F32), jax.ShapeDtypeStruct((1, LANES), F32)],
        scratch_shapes=[pltpu.VMEM((1, LANES), F32)],
        compiler_params=_cparams(("arbitrary",)), name="merge",
    )(x_p, x_s, y_p, y_s, a_p, a_s, wglu, bglu, gs, ga, wo1, wo2, g2, wr, br, tri)


def _start_row_gather(idx_ref, base, src_hbm, dst, sem, nrows):
    def body(r, _):
        pltpu.make_async_copy(src_hbm.at[pl.ds(idx_ref[base + r], 1)], dst.at[pl.ds(r, 1)], sem).start()
        return 0
    lax.fori_loop(0, nrows, body, 0, unroll=8)


def _wait_row_gather(src_hbm, dst, sem, nrows):
    pltpu.make_async_copy(src_hbm.at[pl.ds(0, nrows)], dst, sem).wait()


def _pos_kernel(meta_ref, offs_ref, p1_ref, p2_ref):
    meta = meta_ref[...]
    tm = meta.shape[0]
    lane_i = lax.broadcasted_iota(jnp.int32, (tm, LANES), 1)
    lane = lane_i.astype(F32)
    diag = lax.broadcasted_iota(jnp.int32, (tm, LANES), 0) % LANES == lane_i

    def dense(e, r):
        pos = jnp.sum(jnp.where(lane == e, offs_ref[...], 0.0), axis=-1, keepdims=True) + r
        spread = jnp.where(diag, pos, 0.0).reshape(tm // LANES, LANES, LANES)
        return jnp.sum(spread, axis=1).astype(jnp.int32)

    p1_ref[...] = dense(meta[:, 0:1], meta[:, 4:5])
    p2_ref[...] = dense(meta[:, 1:2], meta[:, 5:6])


def _positions(meta, offs_vec, tm):
    n = meta.shape[0]
    rows = tm // LANES
    return pl.pallas_call(
        _pos_kernel,
        grid=(n // tm,),
        in_specs=[pl.BlockSpec((tm, LANES), lambda i: (i, 0)), pl.BlockSpec((1, LANES), lambda i: (0, 0))],
        out_specs=[pl.BlockSpec((rows, LANES), lambda i: (i, 0)), pl.BlockSpec((rows, LANES), lambda i: (i, 0))],
        out_shape=[jax.ShapeDtypeStruct((n // LANES, LANES), jnp.int32)] * 2,
        compiler_params=_cparams(("parallel",)), name="positions",
    )(meta, offs_vec)


def _dispatch_kernel(p1_ref, p2_ref, tend_ref, xn_ref, xs_hbm, zbuf, sem, *, tmd, tme):
    base = pl.program_id(0) * tmd

    @pl.when(pl.program_id(0) == 0)
    def _():
        zbuf[...] = jnp.zeros(zbuf.shape, zbuf.dtype)

        def last_tile(e, carry, wait):
            end = tend_ref[e]
            begin = jnp.where(e == 0, 0, tend_ref[jnp.maximum(e - 1, 0)])

            @pl.when(end > begin)
            def _():
                cp = pltpu.make_async_copy(zbuf, xs_hbm.at[pl.ds((end - 1) * tme, tme)], sem)
                cp.wait() if wait else cp.start()
            return carry

        def spare_tile(t, carry, wait):
            cp = pltpu.make_async_copy(zbuf, xs_hbm.at[pl.ds(t * tme, tme)], sem)
            cp.wait() if wait else cp.start()
            return carry

        n_used, n_tiles = tend_ref[N_EXPERTS - 1], xs_hbm.shape[0] // tme
        for wait in (False, True):
            lax.fori_loop(0, N_EXPERTS, functools.partial(last_tile, wait=wait), 0)
            lax.fori_loop(n_used, n_tiles, functools.partial(spare_tile, wait=wait), 0)

    def body(r, _):
        src = xn_ref.at[pl.ds(r, 1)]
        pltpu.make_async_copy(src, xs_hbm.at[pl.ds(p1_ref[base + r], 1)], sem).start()
        pltpu.make_async_copy(src, xs_hbm.at[pl.ds(p2_ref[base + r], 1)], sem).start()
        return 0

    lax.fori_loop(0, tmd, body, 0, unroll=8)
    for _ in range(2):
        pltpu.make_async_copy(xn_ref, xs_hbm.at[pl.ds(0, tmd)], sem).wait()


def _dispatch(pos1, pos2, tile_end, xn, n_rows, tmd, tme):
    n, d = xn.shape
    grid_spec = pltpu.PrefetchScalarGridSpec(
        num_scalar_prefetch=3,
        grid=(n // tmd,),
        in_specs=[pl.BlockSpec((tmd, d), lambda i, p1, p2, te: (i, 0))],
        out_specs=pl.BlockSpec(memory_space=pl.ANY),
        scratch_shapes=[pltpu.VMEM((tme, d), xn.dtype), pltpu.SemaphoreType.DMA],
    )
    return pl.pallas_call(
        functools.partial(_dispatch_kernel, tmd=tmd, tme=tme),
        grid_spec=grid_spec,
        out_shape=jax.ShapeDtypeStruct((n_rows, d), xn.dtype),
        compiler_params=_cparams(("arbitrary",)), name="dispatch",
    )(pos1, pos2, tile_end, xn)


def _experts_kernel(te_ref, nu_ref, x_ref, wg_ref, wu_ref, wd_ref, o_ref, wg_scr, wu_scr, wd_scr):
    i = pl.program_id(0)

    @pl.when((i == 0) | (te_ref[i] != te_ref[jnp.maximum(i - 1, 0)]))
    def _():
        wg_scr[...] = wg_ref[0].astype(BF16)
        wu_scr[...] = wu_ref[0].astype(BF16)
        wd_scr[...] = wd_ref[0].astype(BF16)

    @pl.when(i < nu_ref[0])
    def _():
        x = x_ref[...].astype(BF16)
        hg = jnp.dot(x, wg_scr[...], preferred_element_type=F32)
        hu = jnp.dot(x, wu_scr[...], preferred_element_type=F32)
        hh = (jax.nn.silu(hg) * hu).astype(BF16)
        o_ref[...] = jnp.dot(hh, wd_scr[...], preferred_element_type=F32)

    @pl.when(i >= nu_ref[0])
    def _():
        o_ref[...] = jnp.zeros(o_ref.shape, F32)


def _experts(tile_expert, n_used, xs, w_gate, w_up, w_down, tme):
    nt = tile_expert.shape[0]
    ne, d, de = w_gate.shape
    wmap = lambda i, te, nu: (te[i], 0, 0)
    grid_spec = pltpu.PrefetchScalarGridSpec(
        num_scalar_prefetch=2,
        grid=(nt,),
        in_specs=[pl.BlockSpec((tme, d), lambda i, te, nu: (jnp.minimum(i, nu[0] - 1), 0)),
                  pl.BlockSpec((1, d, de), wmap), pl.BlockSpec((1, d, de), wmap), pl.BlockSpec((1, de, d), wmap)],
        out_specs=pl.BlockSpec((tme, d), lambda i, te, nu: (i, 0)),
        scratch_shapes=[pltpu.VMEM((d, de), BF16), pltpu.VMEM((d, de), BF16), pltpu.VMEM((de, d), BF16)],
    )
    return pl.pallas_call(
        _experts_kernel,
        grid_spec=grid_spec,
        out_shape=jax.ShapeDtypeStruct((nt * tme, d), F32),
        compiler_params=_cparams(("arbitrary",)), name="experts",
    )(tile_expert, n_used, xs, w_gate, w_up, w_down)


def _combine_kernel(p1_ref, p2_ref, h_ref, meta_ref, eo_hbm, y_ref, buf, sems, *, tmc, nsteps):
    i = pl.program_id(0)
    slot = i % 2

    def start(step, slot_):
        _start_row_gather(p1_ref, step * tmc, eo_hbm, buf.at[slot_, pl.ds(0, tmc)], sems.at[slot_], tmc)
        _start_row_gather(p2_ref, step * tmc, eo_hbm, buf.at[slot_, pl.ds(tmc, tmc)], sems.at[slot_], tmc)

    @pl.when(i == 0)
    def _():
        start(0, 0)

    @pl.when(i + 1 < nsteps)
    def _():
        start(i + 1, 1 - slot)

    _wait_row_gather(eo_hbm, buf.at[slot], sems.at[slot], 2 * tmc)
    meta = meta_ref[...]
    y_ref[...] = h_ref[...] + meta[:, 2:3] * buf[slot, :tmc, :] + meta[:, 3:4] * buf[slot, tmc:, :]


def _combine(pos1, pos2, h_all, meta_all, eo, row0, tmc):
    n = pos1.shape[0]
    d = h_all.shape[1]
    nsteps = n // tmc
    off = row0 // tmc
    grid_spec = pltpu.PrefetchScalarGridSpec(
        num_scalar_prefetch=2,
        grid=(nsteps,),
        in_specs=[pl.BlockSpec((tmc, d), lambda i, p1, p2: (i + off, 0)),
                  pl.BlockSpec((tmc, LANES), lambda i, p1, p2: (i + off, 0)),
                  pl.BlockSpec(memory_space=pl.ANY)],
        out_specs=pl.BlockSpec((tmc, d), lambda i, p1, p2: (i, 0)),
        scratch_shapes=[pltpu.VMEM((2, 2 * tmc, d), F32), pltpu.SemaphoreType.DMA((2,))],
    )
    return pl.pallas_call(
        functools.partial(_combine_kernel, tmc=tmc, nsteps=nsteps),
        grid_spec=grid_spec,
        out_shape=jax.ShapeDtypeStruct((n, d), F32),
        compiler_params=_cparams(("arbitrary",)), name="combine",
    )(pos1, pos2, h_all, meta_all, eo)


def _pick_tile(n, pref):
    t = min(pref, n)
    while n % t:
        t //= 2
    return t


def _layer(x_p, x_s, cache_ckv, cache_kr, st_re, st_im, page_table, p):
    nb_p, t_p, d = x_p.shape
    nb_s, t_s, _ = x_s.shape
    n_p, n_s = nb_p * t_p, nb_s * t_s
    past_len = page_table.shape[1] * PAGE
    g = p['ssm_a_re'].shape[0]
    ssm_w = g * SSM_CH
    q_rank = p['w_uq'].shape[0]
    kv_rank = p['w_uk'].shape[0]

    w_in = p['w_in']
    o3 = ssm_w + q_rank + kv_rank
    w_pad = jnp.zeros((d, o3 + LANES), F32).at[:, :o3].set(w_in[:, :o3])
    w_pad = w_pad.at[:, o3 + NOPE:o3 + QK].set(w_in[:, o3:]).astype(BF16)
    row2 = lambda v: v.reshape(1, -1).astype(F32)
    padq = lambda v: jnp.pad(v, (0, LANES - QK)).reshape(1, LANES)
    wq = jnp.pad(p['w_uq'].transpose(1, 0, 2), ((0, 0), (0, 0), (0, LANES - QK))).astype(BF16)
    wk = jnp.pad(p['w_uk'].transpose(1, 0, 2), ((0, 0), (0, 0), (0, LANES - NOPE))).astype(BF16)
    wvp = p['w_uv'].reshape(kv_rank, N_HEADS // 2, 2 * V_DIM).transpose(1, 0, 2).astype(BF16)
    wv_h = p['w_uv'].transpose(1, 0, 2)
    wv8 = jnp.stack([jnp.pad(wv_h[h], ((0, 0), (V_DIM - _ones_lane(h), _ones_lane(h)))) for h in range(N_HEADS)])
    wv8 = wv8.astype(BF16)
    wukt_all = p['w_uk'].reshape(kv_rank, N_HEADS * NOPE).T.astype(BF16)
    wukt = jnp.pad(p['w_uk'].transpose(1, 2, 0), ((0, 0), (0, LANES - NOPE), (0, 0))).astype(BF16)
    sel = (jnp.arange(LANES)[:, None] == NOPE + jnp.arange(ROPE)[None, :]).astype(BF16)
    gq, gk = padq(p['qk_norm_q']), padq(p['qk_norm_k'])
    ssm_params = (p['ssm_a_re'], p['ssm_a_im'], p['ssm_log_dt'], p['ssm_b_re'], p['ssm_b_im'],
                  p['ssm_c_re'], p['ssm_c_im'], p['ssm_d'])

    tm_p = _pick_tile(t_p, 512)
    tab_p = _rope_tables(jnp.arange(t_p))
    u, cq, ckv_p, krp_p = _in_proj(x_p.reshape(n_p, d), row2(p['norm1_g']), w_pad, row2(p['q_norm_g']),
                                   row2(p['kv_norm_g']), tab_p, tm_p, ssm_w, q_rank, kv_rank)
    lc_p = _pick_tile(t_p, 8)
    nsl = ssm_w // LANES
    y_ssm_p, hf_p = _s5(u, jnp.zeros((nsl, nb_p, 2 * GPB * SSM_P), F32), _s5_weights(*ssm_params, lc_p),
                        nb_p, t_p, lc_p)
    tq = _pick_tile(t_p, 512)
    att_p = _attn_prompt(cq, ckv_p, krp_p, tab_p, wq, wk, wv8, gq, gk, nb_p, t_p, tq, tq)

    tm_s = _pick_tile(n_s, 512)
    pos_s = past_len + jnp.arange(t_s)
    tab_s = tuple(jnp.tile(a, (tm_s // t_s, 1)) for a in _rope_tables(pos_s))
    u, cq, ckv_s, krp_s = _in_proj(x_s.reshape(n_s, d), row2(p['norm1_g']), w_pad, row2(p['q_norm_g']),
                                   row2(p['kv_norm_g']), tab_s, tm_s, ssm_w, q_rank, kv_rank)
    slab_state = lambda s: s.astype(F32).reshape(nb_s, nsl, GPB * SSM_P).transpose(1, 0, 2)
    h0 = jnp.concatenate([slab_state(st_re), slab_state(st_im)], axis=-1)
    y_ssm_s, hf_s = _s5(u, h0, _s5_weights(*ssm_params, t_s), nb_s, t_s, t_s)
    tab_q = tuple(jnp.tile(a, (n_s // tm_s, 1)) for a in tab_s)
    qa, qr = _q_sample(cq, tab_q, wq, wukt, sel, gq, gk)
    kr_s = krp_s[:, NOPE:QK]
    cnew = jnp.pad(ckv_s.reshape(nb_s, t_s, kv_rank), ((0, 0), (0, PAGE - t_s), (0, 0)))
    krnew = jnp.pad(kr_s.reshape(nb_s, t_s, ROPE), ((0, 0), (0, PAGE - t_s), (0, 0))).swapaxes(1, 2)
    pc = _pick_tile(page_table.shape[1], 16)
    att_s = _attn_sample(page_table, qa, qr, cnew, krnew, wukt_all, wvp, cache_ckv, jnp.swapaxes(cache_kr, 1, 2),
                         t_s, pc)

    tm = _pick_tile(math.gcd(n_p, n_s), 512)
    w_out = p['w_out'].astype(BF16)
    wr = jnp.zeros((d, LANES), F32).at[:, :N_GROUPS].set(p['w_router_group'])
    wr = wr.at[:, N_GROUPS:N_GROUPS + N_EXPERTS].set(p['w_router_expert'])
    br = jnp.zeros((1, LANES), F32).at[0, :N_GROUPS].set(p['b_router_group'])
    br = br.at[0, N_GROUPS:N_GROUPS + N_EXPERTS].set(p['b_router_expert'])
    h_all, xn_all, meta, cnt = _merge(
        x_p.reshape(n_p, d), x_s.reshape(n_s, d), y_ssm_p, y_ssm_s, att_p, att_s,
        p['ssm_w_glu'].astype(BF16), row2(p['ssm_b_glu']), row2(p['out_norm_ssm']), row2(p['out_norm_attn']),
        w_out[:ssm_w], w_out[ssm_w:], row2(p['norm2_g']), wr, br, tm)

    n_all = n_p + n_s
    tme = 256
    nt = (2 * n_all) // tme + N_EXPERTS
    counts = cnt[0, :N_EXPERTS].astype(jnp.int32)
    tiles_per = (counts + tme - 1) // tme
    tile_end = jnp.cumsum(tiles_per)
    tile_start = tile_end - tiles_per
    tile_id = jnp.arange(nt, dtype=jnp.int32)
    tile_expert = jnp.minimum(jnp.sum(tile_end[None, :] <= tile_id[:, None], axis=1), N_EXPERTS - 1).astype(jnp.int32)
    n_used = tile_end[-1:].astype(jnp.int32)
    offs_vec = jnp.zeros((1, LANES), F32).at[0, :N_EXPERTS].set((tile_start * tme).astype(F32))
    tmd = _pick_tile(n_all, 1024)
    pos1, pos2 = (a.reshape(-1) for a in _positions(meta, offs_vec, tmd))
    xs = _dispatch(pos1, pos2, tile_end.astype(jnp.int32), xn_all, nt * tme, tmd, tme)
    eo = _experts(tile_expert, n_used, xs, p['w_gate'], p['w_up'], p['w_down'], tme)

    tmc = _pick_tile(math.gcd(n_p, n_s), 256)
    y_p = _combine(pos1[:n_p], pos2[:n_p], h_all, meta, eo, 0, tmc)
    y_s = _combine(pos1[n_p:], pos2[n_p:], h_all, meta, eo, n_p, tmc)

    n_pg = t_p // PAGE

    def split(hf):
        nb = hf.shape[1]
        half = GPB * SSM_P
        unslab = lambda a: a.reshape(nsl, nb, GPB, SSM_P).transpose(1, 0, 2, 3).reshape(nb, g, SSM_P)
        return unslab(hf[..., :half]), unslab(hf[..., half:])

    hp_re, hp_im = split(hf_p)
    hs_re, hs_im = split(hf_s)
    return (y_p.reshape(nb_p, t_p, d), y_s.reshape(nb_s, t_s, d),
            ckv_p.reshape(nb_p, n_pg, PAGE, kv_rank), krp_p[:, NOPE:QK].reshape(nb_p, n_pg, PAGE, ROPE),
            hp_re, hp_im, ckv_s.reshape(nb_s, t_s, kv_rank), kr_s.reshape(nb_s, t_s, ROPE), hs_re, hs_im)


_PARAM_NAMES = ('norm1_g', 'w_in', 'ssm_a_re', 'ssm_a_im', 'ssm_log_dt', 'ssm_b_re', 'ssm_b_im', 'ssm_c_re',
                'ssm_c_im', 'ssm_d', 'ssm_w_glu', 'ssm_b_glu', 'q_norm_g', 'w_uq', 'kv_norm_g', 'w_uk', 'w_uv',
                'qk_norm_q', 'qk_norm_k', 'out_norm_ssm', 'out_norm_attn', 'w_out', 'norm2_g', 'w_router_group',
                'b_router_group', 'w_router_expert', 'b_router_expert', 'w_gate', 'w_up', 'w_down')


def kernel(x_prompt, x_sample, cache_ckv, cache_krope, state_ssm_re, state_ssm_im, page_table, norm1_g, w_in, ssm_a_re, ssm_a_im, ssm_log_dt, ssm_b_re, ssm_b_im, ssm_c_re, ssm_c_im, ssm_d, ssm_w_glu, ssm_b_glu, q_norm_g, w_uq, kv_norm_g, w_uk, w_uv, qk_norm_q, qk_norm_k, out_norm_ssm, out_norm_attn, w_out, norm2_g, w_router_group, b_router_group, w_router_expert, b_router_expert, w_gate, w_up, w_down):
    params = (norm1_g, w_in, ssm_a_re, ssm_a_im, ssm_log_dt, ssm_b_re, ssm_b_im, ssm_c_re, ssm_c_im, ssm_d,
              ssm_w_glu, ssm_b_glu, q_norm_g, w_uq, kv_norm_g, w_uk, w_uv, qk_norm_q, qk_norm_k, out_norm_ssm,
              out_norm_attn, w_out, norm2_g, w_router_group, b_router_group, w_router_expert, b_router_expert,
              w_gate, w_up, w_down)
    depth = w_in.shape[0]
    h_p, h_s = x_prompt, x_sample
    outs = [[] for _ in range(8)]
    for layer in range(depth):
        p = {k: v[layer] for k, v in zip(_PARAM_NAMES, params)}
        res = _layer(h_p, h_s, cache_ckv[layer], cache_krope[layer], state_ssm_re[layer], state_ssm_im[layer],
                     page_table, p)
        h_p, h_s = res[0], res[1]
        for acc, r in zip(outs, res[2:]):
            acc.append(r)
    return (h_p, h_s) + tuple(jnp.stack(o) for o in outs)
```

```python
import functools
import math

import jax
import jax.numpy as jnp
from jax import lax
from jax.experimental import pallas as pl
from jax.experimental.pallas import tpu as pltpu

F32 = jnp.float32
BF16 = jnp.bfloat16
HIGHEST = lax.Precision.HIGHEST

LANES = 128
SUBLANES = 8
VMEM_LIMIT = 56 * 1024 * 1024

EPS = 1e-6
ROPE_BASE = 10000.0
PAGE = 128
SSM_CH = 16
SSM_P = 64
GPB = LANES // SSM_CH
S5_MAX_TOKENS = 8192
N_HEADS = 8
NOPE = 64
ROPE = 32
QK = NOPE + ROPE
V_DIM = 64
N_GROUPS = 4
EXP_PER_GROUP = 8
N_EXPERTS = N_GROUPS * EXP_PER_GROUP
ATTN_HEADS_PER_BODY = 8

NT_DIMS = (((1,), (1,)), ((), ()))
LOG2E = math.log2(math.e)


def _cparams(sem, vmem=VMEM_LIMIT):
    return pltpu.CompilerParams(dimension_semantics=sem, vmem_limit_bytes=vmem)


def _rms(x, g):
    return x * lax.rsqrt(jnp.mean(x * x, axis=-1, keepdims=True) + EPS) * g


def _rms_qk(x, g):
    return x * lax.rsqrt(jnp.sum(x * x, axis=-1, keepdims=True) * (1.0 / QK) + EPS) * g


def _rope(x, cos, sinp, sinm):
    return x * cos + pltpu.roll(x, 16, 1) * sinp + pltpu.roll(x, LANES - 16, 1) * sinm


def _rope_tables(pos):
    inv_freq = ROPE_BASE ** (-jnp.arange(0, ROPE, 2, dtype=F32) / ROPE)
    ang = pos.astype(F32)[:, None] * inv_freq[None, :]
    c, s = jnp.cos(ang), jnp.sin(ang)
    t = pos.shape[0]
    z = lambda n: jnp.zeros((t, n), F32)
    cosf = jnp.concatenate([jnp.ones((t, NOPE), F32), c, c, z(LANES - QK)], axis=1)
    sinp = jnp.concatenate([z(NOPE + ROPE // 2), s, z(LANES - QK)], axis=1)
    sinm = jnp.concatenate([z(NOPE), -s, z(LANES - QK + ROPE // 2)], axis=1)
    return cosf, sinp, sinm


def _in_proj_kernel(x_ref, g1_ref, w_ref, gq_ref, gkv_ref, cos_ref, sinp_ref, sinm_ref,
                    u_ref, cq_ref, ckv_ref, kr_ref, *, ssm_w, q_rank, kv_rank):
    n = _rms(x_ref[...], g1_ref[...])
    z = jnp.dot(n.astype(BF16), w_ref[...], preferred_element_type=F32)
    o1, o2, o3 = ssm_w, ssm_w + q_rank, ssm_w + q_rank + kv_rank
    for j in range(ssm_w // LANES):
        u_ref[j] = z[:, j * LANES:(j + 1) * LANES]
    cq_ref[...] = _rms(z[:, o1:o2], gq_ref[...]).astype(cq_ref.dtype)
    ckv_ref[...] = _rms(z[:, o2:o3], gkv_ref[...])
    kr_ref[...] = _rope(z[:, o3:o3 + LANES], cos_ref[...], sinp_ref[...], sinm_ref[...])


def _in_proj(x, g1, w_pad, gq, gkv, tables, tm, ssm_w, q_rank, kv_rank):
    n, d = x.shape
    period = tables[0].shape[0]
    nper = period // tm
    row = lambda i: (i, 0)
    const = lambda i: (0, 0)
    tab = pl.BlockSpec((tm, LANES), lambda i: (i % nper, 0))
    return pl.pallas_call(
        functools.partial(_in_proj_kernel, ssm_w=ssm_w, q_rank=q_rank, kv_rank=kv_rank),
        grid=(n // tm,),
        in_specs=[pl.BlockSpec((tm, d), row), pl.BlockSpec((1, d), const),
                  pl.BlockSpec(w_pad.shape, const), pl.BlockSpec((1, q_rank), const),
                  pl.BlockSpec((1, kv_rank), const), tab, tab, tab],
        out_specs=[pl.BlockSpec((ssm_w // LANES, tm, LANES), lambda i: (0, i, 0)), pl.BlockSpec((tm, q_rank), row),
                   pl.BlockSpec((tm, kv_rank), row), pl.BlockSpec((tm, LANES), row)],
        out_shape=[jax.ShapeDtypeStruct((ssm_w // LANES, n, LANES), F32), jax.ShapeDtypeStruct((n, q_rank), BF16),
                   jax.ShapeDtypeStruct((n, kv_rank), F32), jax.ShapeDtypeStruct((n, LANES), F32)],
        compiler_params=_cparams(("parallel",)), name="in_proj",
    )(x, g1, w_pad, gq, gkv, *tables)


def _s5_weights(a_re, a_im, log_dt, b_re, b_im, c_re, c_im, d, lc):
    g, p = a_re.shape
    dt = jnp.exp(log_dt)[:, None]
    den = a_re * a_re + a_im * a_im
    mag, ang = jnp.exp(a_re * dt), a_im * dt
    lb_re, lb_im = mag * jnp.cos(ang), mag * jnp.sin(ang)
    num_re = lb_re - 1.0
    coef_re = (num_re * a_re + lb_im * a_im) / den
    coef_im = (lb_im * a_re - num_re * a_im) / den
    bb_re = coef_re[..., None] * b_re - coef_im[..., None] * b_im
    bb_im = coef_re[..., None] * b_im + coef_im[..., None] * b_re

    def lam_pow(k):
        kk = k.astype(F32)[:, None, None]
        m = jnp.exp(a_re * dt * kk)
        return m * jnp.cos(ang * kk), m * jnp.sin(ang * kk)

    steps = jnp.arange(lc)
    pr, pi = lam_pow(lc - 1 - steps)
    bz_re = pr[..., None] * bb_re[None] - pi[..., None] * bb_im[None]
    bz_im = pr[..., None] * bb_im[None] + pi[..., None] * bb_re[None]
    bz = jnp.concatenate([bz_re, bz_im], axis=2)
    bz = bz.transpose(1, 0, 3, 2).reshape(g, lc * SSM_CH, 2 * p)
    qr, qi = lam_pow(steps + 1)
    cl_re = c_re[None] * qr[:, :, None, :] - c_im[None] * qi[:, :, None, :]
    cl_im = c_re[None] * qi[:, :, None, :] + c_im[None] * qr[:, :, None, :]
    cz = jnp.concatenate([cl_re, -cl_im], axis=3)
    cz = cz.transpose(1, 3, 0, 2).reshape(g, 2 * p, lc * SSM_CH)
    tr, ti = lam_pow(steps)
    cb_re = jnp.einsum('gcp,kgp,gpd->gkcd', c_re, tr, bb_re, precision=HIGHEST)
    cb_re -= jnp.einsum('gcp,kgp,gpd->gkcd', c_re, ti, bb_im, precision=HIGHEST)
    cb_re -= jnp.einsum('gcp,kgp,gpd->gkcd', c_im, tr, bb_im, precision=HIGHEST)
    cb_re -= jnp.einsum('gcp,kgp,gpd->gkcd', c_im, ti, bb_re, precision=HIGHEST)
    tau = steps[None, :] - steps[:, None]
    ksel = cb_re[:, jnp.clip(tau, 0, lc - 1)]
    ksel = jnp.where((tau >= 0)[None, :, :, None, None], ksel, 0.0)
    tz = ksel.transpose(0, 1, 4, 2, 3).reshape(g, lc * SSM_CH, lc * SSM_CH)
    assert lc * SSM_CH == LANES and 2 * p == LANES
    nsl = g // GPB
    sw = GPB * LANES
    col = jnp.arange(sw)
    rep_state = (jnp.arange(LANES)[:, None] == (col // (sw // 2)) * p + col % p).astype(BF16)
    rep_token = (jnp.arange(LANES)[:, None] == (col // LANES) * SSM_CH + col % SSM_CH).astype(BF16)
    grp_state = (col // p) % GPB
    grp_token = (col // SSM_CH) % GPB

    def slab(rows, rep, row_grp, col_grp):
        full = jnp.einsum('jrk,kc->jrc', rows.astype(BF16), rep, preferred_element_type=F32)
        return jnp.where(row_grp[:, None] == col_grp[None, :], full, 0.0).astype(BF16)

    bz_rows = bz.reshape(nsl, GPB, lc, SSM_CH, LANES).transpose(0, 2, 1, 3, 4).reshape(nsl, sw, LANES)
    tz_rows = tz.reshape(nsl, GPB, lc, SSM_CH, LANES).transpose(0, 2, 1, 3, 4).reshape(nsl, sw, LANES)
    cz_rows = cz.reshape(nsl, GPB, 2, p, LANES).transpose(0, 2, 1, 3, 4).reshape(nsl, sw, LANES)
    bz_s = slab(bz_rows, rep_state, grp_token, grp_state)
    wy = jnp.concatenate([slab(cz_rows, rep_token, grp_state, grp_token),
                          slab(tz_rows, rep_token, grp_token, grp_token)], axis=1)
    lr, li = lam_pow(jnp.array([lc]))
    lam_re = lr[0].reshape(nsl, 1, GPB * p)
    lam_im = li[0].reshape(nsl, 1, GPB * p)
    dvec = jnp.tile(d.reshape(nsl, 1, LANES), (1, 1, lc))
    return bz_s, wy, lam_re, lam_im, dvec


def _s5_kernel(u_ref, bz_ref, wy_ref, lr_ref, li_ref, d_ref, h0_ref, y_ref, hf_ref, s_scr, *, nk, bs, lc, rb):
    nh = s_scr.shape[0] // 2
    blk = lambda a, c: a[:, c * LANES:(c + 1) * LANES]

    def ucat(r0):
        return jnp.concatenate([u_ref[0, pl.ds(r0 * lc + t, rb, stride=lc), :] for t in range(lc)], axis=1)

    def phase1(i, _):
        r0 = pl.multiple_of(i * rb, rb)
        s = jnp.dot(ucat(r0).astype(BF16), bz_ref[0], preferred_element_type=F32)
        for c in range(2 * nh):
            s_scr[c, pl.ds(r0, rb), :] = blk(s, c)
        return 0

    lax.fori_loop(0, (bs * nk) // rb, phase1, 0)
    lr, li = lr_ref[0], li_ref[0]

    def step(k, h):
        rows = pl.ds(k, bs, stride=nk)
        new = [None] * (2 * nh)
        for c in range(nh):
            h_re, h_im = h[c], h[nh + c]
            s_re, s_im = s_scr[c, rows, :], s_scr[nh + c, rows, :]
            s_scr[c, rows, :] = h_re
            s_scr[nh + c, rows, :] = h_im
            new[c] = blk(lr, c) * h_re - blk(li, c) * h_im + s_re
            new[nh + c] = blk(lr, c) * h_im + blk(li, c) * h_re + s_im
        return tuple(new)

    h0 = h0_ref[0, 0]
    hf = lax.fori_loop(0, nk, step, tuple(blk(h0, c) for c in range(2 * nh)))
    hf_ref[0, 0] = jnp.concatenate(hf, axis=1)

    def phase3(i, _):
        r0 = pl.multiple_of(i * rb, rb)
        uc = ucat(r0)
        hprev = jnp.concatenate([s_scr[c, pl.ds(r0, rb), :] for c in range(2 * nh)], axis=1)
        lhs = jnp.concatenate([hprev.astype(BF16), uc.astype(BF16)], axis=1)
        y = jnp.dot(lhs, wy_ref[0], preferred_element_type=F32) + d_ref[0] * uc
        for t in range(lc):
            y_ref[0, pl.ds(r0 * lc + t, rb, stride=lc), :] = y[:, t * LANES:(t + 1) * LANES]
        return 0

    lax.fori_loop(0, (bs * nk) // rb, phase3, 0)


def _s5(u, h0, weights, nb, t, lc):
    bz, wy, lam_re, lam_im, dvec = weights
    nsl = u.shape[0]
    nk = t // lc
    sw = h0.shape[-1]
    bs = nb
    while bs * t > S5_MAX_TOKENS and bs % 2 == 0:
        bs //= 2
    rb = _pick_tile(bs * nk, 256)
    h0 = h0.reshape(nsl, nb // bs, bs, sw)
    slab = lambda j, b: (j, 0, 0)
    tok = lambda j, b: (j, b, 0)
    st = lambda j, b: (j, b, 0, 0)
    y, hf = pl.pallas_call(
        functools.partial(_s5_kernel, nk=nk, bs=bs, lc=lc, rb=rb),
        grid=(nsl, nb // bs),
        in_specs=[pl.BlockSpec((1, bs * t, LANES), tok), pl.BlockSpec((1,) + bz.shape[1:], slab),
                  pl.BlockSpec((1,) + wy.shape[1:], slab), pl.BlockSpec((1, 1, sw // 2), slab),
                  pl.BlockSpec((1, 1, sw // 2), slab), pl.BlockSpec((1, 1, lc * LANES), slab),
                  pl.BlockSpec((1, 1, bs, sw), st)],
        out_specs=[pl.BlockSpec((1, bs * t, LANES), tok), pl.BlockSpec((1, 1, bs, sw), st)],
        out_shape=[jax.ShapeDtypeStruct(u.shape, F32), jax.ShapeDtypeStruct(h0.shape, F32)],
        scratch_shapes=[pltpu.VMEM((sw // LANES, bs * nk, LANES), F32)],
        compiler_params=_cparams(("parallel", "parallel")), name="s5",
    )(u, bz, wy, lam_re, lam_im, dvec, h0)
    return y, hf.reshape(nsl, nb, sw)


def _ones_lane(h):
    return V_DIM if h % 2 == 0 else 0


def _attn_prompt_kernel(cq_ref, ckv_ref, krp_ref, cos_ref, sinp_ref, sinm_ref, wq_ref, wk_ref, wv_ref,
                        gq_ref, gk_ref, o_ref, k_scr, v_scr, q_scr, *, tq, tk, hb):
    qi = pl.program_id(1)

    @pl.when(qi == 0)
    def _():
        c = ckv_ref[...].astype(BF16)
        krp = krp_ref[...]
        for h in range(N_HEADS):
            kh = jnp.dot(c, wk_ref[h], preferred_element_type=F32) + krp
            k_scr[h] = _rms_qk(kh, gk_ref[...]).astype(BF16)
        vlane = lax.broadcasted_iota(jnp.int32, (1, LANES), 1)
        for h in range(N_HEADS):
            one = jnp.where(vlane == _ones_lane(h), 1.0, 0.0)
            v_scr[h] = (jnp.dot(c, wv_ref[h], preferred_element_type=F32) + one).astype(BF16)

    cq = cq_ref[...]
    cos, sinp, sinm = cos_ref[...], sinp_ref[...], sinm_ref[...]
    for h in range(N_HEADS):
        q = _rope(jnp.dot(cq, wq_ref[h], preferred_element_type=F32), cos, sinp, sinm)
        q_scr[h] = (_rms_qk(q, gq_ref[...]) * (LOG2E / math.sqrt(QK))).astype(BF16)

    row = qi * tq + lax.broadcasted_iota(jnp.int32, (tq, tk), 0)
    col = lax.broadcasted_iota(jnp.int32, (tq, tk), 1)
    lane = lax.broadcasted_iota(jnp.int32, (tq, LANES), 1)
    n_full = qi * (tq // tk)

    def kv_step(j, carry, masked, heads):
        ks = pl.ds(pl.multiple_of(j * tk, tk), tk)
        out = []
        for (m, acc), h in zip(carry, heads):
            s = lax.dot_general(q_scr[h], k_scr[h, ks, :], NT_DIMS, preferred_element_type=F32)
            if masked:
                s = jnp.where(col + j * tk <= row, s, -jnp.inf)
            m_new = jnp.maximum(m, jnp.max(s, axis=-1, keepdims=True))
            p = jnp.exp2(s - m_new)
            acc = acc * jnp.exp2(m - m_new) + jnp.dot(p.astype(BF16), v_scr[h, ks, :], preferred_element_type=F32)
            out.append((m_new, acc))
        return tuple(out)

    for h0 in range(0, N_HEADS, hb):
        heads = tuple(range(h0, h0 + hb))
        init = (jnp.full((tq, 1), -jnp.inf, F32), jnp.zeros((tq, LANES), F32))
        carry = lax.fori_loop(0, n_full, functools.partial(kv_step, masked=False, heads=heads), (init,) * hb)
        for jj in range(tq // tk):
            carry = kv_step(n_full + jj, carry, True, heads)
        outs = [acc / acc[:, _ones_lane(h):_ones_lane(h) + 1] for (_, acc), h in zip(carry, heads)]
        for k in range(0, hb, 2):
            hp = (h0 + k) // 2
            o_ref[:, hp * LANES:(hp + 1) * LANES] = jnp.where(lane < V_DIM, outs[k], outs[k + 1])


def _attn_prompt(cq, ckv, krp, tables, wq, wk, wvp, gq, gk, nb, t, tq, tk, hb=ATTN_HEADS_PER_BODY):
    n = cq.shape[0]
    nq = t // tq
    qrow = lambda b, i: (b * nq + i, 0)
    seq = lambda b, i: (b, 0)
    tab = pl.BlockSpec((tq, LANES), lambda b, i: (i, 0))
    c3 = lambda b, i: (0, 0, 0)
    c2 = lambda b, i: (0, 0)
    return pl.pallas_call(
        functools.partial(_attn_prompt_kernel, tq=tq, tk=tk, hb=hb),
        grid=(nb, nq),
        in_specs=[pl.BlockSpec((tq, cq.shape[1]), qrow), pl.BlockSpec((t, ckv.shape[1]), seq),
                  pl.BlockSpec((t, LANES), seq), tab, tab, tab,
                  pl.BlockSpec(wq.shape, c3), pl.BlockSpec(wk.shape, c3), pl.BlockSpec(wvp.shape, c3),
                  pl.BlockSpec((1, LANES), c2), pl.BlockSpec((1, LANES), c2)],
        out_specs=pl.BlockSpec((tq, N_HEADS * V_DIM), qrow),
        out_shape=jax.ShapeDtypeStruct((n, N_HEADS * V_DIM), F32),
        scratch_shapes=[pltpu.VMEM((N_HEADS, t, LANES), BF16), pltpu.VMEM((N_HEADS, t, LANES), BF16),
                        pltpu.VMEM((N_HEADS, tq, LANES), BF16)],
        compiler_params=_cparams(("parallel", "arbitrary")), name="attn_prompt",
    )(cq, ckv, krp, *tables, wq, wk, wvp, gq, gk)


def _q_sample_kernel(cq_ref, cos_ref, sinp_ref, sinm_ref, wq_ref, wukt_ref, sel_ref, gq_ref, gk_ref,
                     qa_ref, qr_ref):
    cq = cq_ref[...]
    for h in range(N_HEADS):
        q = _rope(jnp.dot(cq, wq_ref[h], preferred_element_type=F32), cos_ref[...], sinp_ref[...], sinm_ref[...])
        q = _rms_qk(q, gq_ref[...]) * (LOG2E / math.sqrt(QK))
        qk = (q * gk_ref[...]).astype(BF16)
        qa_ref[h] = jnp.dot(qk, wukt_ref[h], preferred_element_type=F32).astype(BF16)
        qr_ref[h] = jnp.dot(qk, sel_ref[...], preferred_element_type=F32).astype(BF16)


def _q_sample(cq, tables, wq, wukt, sel, gq, gk):
    n = cq.shape[0]
    return pl.pallas_call(
        _q_sample_kernel,
        out_shape=[jax.ShapeDtypeStruct((N_HEADS, n, LANES), BF16), jax.ShapeDtypeStruct((N_HEADS, n, ROPE), BF16)],
        compiler_params=pltpu.CompilerParams(vmem_limit_bytes=VMEM_LIMIT), name="q_sample",
    )(cq, *tables, wq, wukt, sel, gq, gk)


def _attn_sample_kernel(pt_ref, qa_ref, qr_ref, qa_prev_ref, qr_prev_ref, cnew_ref, krnew_ref, wukt_ref, wv_ref,
                        ckv_hbm, kr_hbm, o_ref, cbuf, kbuf, sems, s_scr, cb_scr, m_scr, l_scr, acc_scr,
                        *, pc, nc, total, t_dec):
    g = pl.program_id(0)
    slot = g % 2
    rows = N_HEADS * t_dec

    def page_copies(step, slot_):
        base = step * pc
        cps = []
        for j in range(pc):
            pg = pt_ref[base + j]
            dst = pl.ds(j * PAGE, PAGE)
            cps.append(pltpu.make_async_copy(ckv_hbm.at[pg], cbuf.at[slot_, dst, :], sems.at[0, slot_]))
            cps.append(pltpu.make_async_copy(kr_hbm.at[pg], kbuf.at[slot_, :, dst], sems.at[1, slot_]))
        return cps

    def reset_softmax():
        m_scr[...] = jnp.full(m_scr.shape, -jnp.inf, F32)
        l_scr[...] = jnp.zeros(l_scr.shape, F32)
        acc_scr[...] = jnp.zeros(acc_scr.shape, F32)

    @pl.when(g == 0)
    def _():
        for cp in page_copies(g, slot):
            cp.start()
        s_scr[1] = jnp.zeros(s_scr.shape[1:], F32)
        cb_scr[1] = jnp.zeros(cb_scr.shape[1:], BF16)
        reset_softmax()

    @pl.when((g >= 1) & ((g - 1) % nc == 0))
    def _():
        reset_softmax()

    def scores(qa, qr, cb, krt):
        nk = cb.shape[0]
        knt = lax.dot_general(wukt_ref[...], cb, NT_DIMS, preferred_element_type=F32)
        n2 = jnp.sum((knt * knt).reshape(N_HEADS, NOPE, nk), axis=1)
        kr2 = jnp.sum(krt * krt, axis=0, keepdims=True)
        rinv = lax.rsqrt((n2 + kr2) * (1.0 / QK) + EPS)
        s = lax.dot_general(qa, cb, NT_DIMS, preferred_element_type=F32)
        s += jnp.dot(qr, krt.astype(BF16), preferred_element_type=F32)
        return (s.reshape(N_HEADS, t_dec, nk) * rinv[:, None, :]).reshape(rows, nk)

    def fold(s, cb):
        m = m_scr[...]
        m_new = jnp.maximum(m, jnp.max(s, axis=-1, keepdims=True))
        p = jnp.exp2(s - m_new)
        corr = jnp.exp2(m - m_new)
        l_scr[...] = l_scr[...] * corr + jnp.sum(p, axis=-1, keepdims=True)
        acc_scr[...] = acc_scr[...] * corr + jnp.dot(p.astype(BF16), cb, preferred_element_type=F32)
        m_scr[...] = m_new

    def score_and_fold(cur, prev):
        last = total - 1
        for cp in page_copies(jnp.minimum(g, last), cur):
            cp.wait()
        for cp in page_copies(jnp.minimum(g + 1, last), prev):
            cp.start()
        s_prev, cb_prev = s_scr[prev], cb_scr[prev]
        cb = cbuf[cur].astype(BF16)
        s_scr[cur] = scores(qa_ref[...].reshape(rows, LANES), qr_ref[...].reshape(rows, ROPE), cb, kbuf[cur])
        cb_scr[cur] = cb
        fold(s_prev, cb_prev)

    for parity in range(2):
        pl.when(slot == parity)(functools.partial(score_and_fold, parity, 1 - parity))

    @pl.when(g == total)
    def _():
        for cp in page_copies(total - 1, 1 - slot):
            cp.wait()

    @pl.when((g >= 1) & ((g - 1) % nc == nc - 1))
    def _():
        key = lax.broadcasted_iota(jnp.int32, (rows, PAGE), 1)
        step = lax.broadcasted_iota(jnp.int32, (rows, PAGE), 0) % t_dec
        cb_new = cnew_ref[0].astype(BF16)
        s_new = scores(qa_prev_ref[...].reshape(rows, LANES), qr_prev_ref[...].reshape(rows, ROPE), cb_new, krnew_ref[0])
        fold(jnp.where(key <= step, s_new, -jnp.inf), cb_new)
        o_lat = (acc_scr[...] / l_scr[...]).astype(BF16)
        lane = lax.broadcasted_iota(jnp.int32, (t_dec, LANES), 1)
        for hp in range(N_HEADS // 2):
            lo = jnp.dot(o_lat[(2 * hp) * t_dec:(2 * hp + 1) * t_dec], wv_ref[hp], preferred_element_type=F32)
            hi = jnp.dot(o_lat[(2 * hp + 1) * t_dec:(2 * hp + 2) * t_dec], wv_ref[hp], preferred_element_type=F32)
            o_ref[:, hp * LANES:(hp + 1) * LANES] = jnp.where(lane < V_DIM, lo, hi)


def _attn_sample(page_table, qa, qr, cnew_pad, krnew_pad, wukt_all, wvp, cache_ckv, cache_kr, t_dec, pc):
    nb, npg = page_table.shape
    nc = npg // pc
    total = nb * nc
    kv_rank = cache_ckv.shape[-1]
    nkeys = pc * PAGE
    rows = N_HEADS * t_dec
    seq_cur = lambda g: jnp.minimum(g // nc, nb - 1)
    seq_prev = lambda g: jnp.maximum(g - 1, 0) // nc
    grid_spec = pltpu.PrefetchScalarGridSpec(
        num_scalar_prefetch=1,
        grid=(total + 1,),
        in_specs=[pl.BlockSpec((N_HEADS, t_dec, LANES), lambda g, pt: (0, seq_cur(g), 0)),
                  pl.BlockSpec((N_HEADS, t_dec, ROPE), lambda g, pt: (0, seq_cur(g), 0)),
                  pl.BlockSpec((N_HEADS, t_dec, LANES), lambda g, pt: (0, seq_prev(g), 0)),
                  pl.BlockSpec((N_HEADS, t_dec, ROPE), lambda g, pt: (0, seq_prev(g), 0)),
                  pl.BlockSpec((1, PAGE, kv_rank), lambda g, pt: (seq_prev(g), 0, 0)),
                  pl.BlockSpec((1, ROPE, PAGE), lambda g, pt: (seq_prev(g), 0, 0)),
                  pl.BlockSpec(wukt_all.shape, lambda g, pt: (0, 0)),
                  pl.BlockSpec(wvp.shape, lambda g, pt: (0, 0, 0)),
                  pl.BlockSpec(memory_space=pl.ANY), pl.BlockSpec(memory_space=pl.ANY)],
        out_specs=pl.BlockSpec((t_dec, N_HEADS * V_DIM), lambda g, pt: (seq_prev(g), 0)),
        scratch_shapes=[pltpu.VMEM((2, nkeys, kv_rank), F32), pltpu.VMEM((2, ROPE, nkeys), F32),
                        pltpu.SemaphoreType.DMA((2, 2)),
                        pltpu.VMEM((2, rows, nkeys), F32), pltpu.VMEM((2, nkeys, kv_rank), BF16),
                        pltpu.VMEM((rows, 1), F32), pltpu.VMEM((rows, 1), F32), pltpu.VMEM((rows, kv_rank), F32)],
    )
    return pl.pallas_call(
        functools.partial(_attn_sample_kernel, pc=pc, nc=nc, total=total, t_dec=t_dec),
        grid_spec=grid_spec,
        out_shape=jax.ShapeDtypeStruct((nb * t_dec, N_HEADS * V_DIM), F32),
        compiler_params=_cparams(("arbitrary",)), name="attn_sample",
    )(page_table.reshape(-1), qa, qr, qa, qr, cnew_pad, krnew_pad, wukt_all, wvp, cache_ckv, cache_kr)


def _merge_kernel(xp_ref, xs_ref, yp_ref, ys_ref, ap_ref, as_ref, wglu_ref, bglu_ref, gs_ref, ga_ref,
                  wo1_ref, wo2_ref, g2_ref, wrh_ref, wrl_ref, br_ref, tri_ref,
                  h_ref, xn_ref, meta_ref, cnt_ref, run_scr, *, n_prompt_tiles):
    i = pl.program_id(0)

    @pl.when(i == 0)
    def _():
        run_scr[...] = jnp.zeros(run_scr.shape, F32)

    is_p = i < n_prompt_tiles
    x = jnp.where(is_p, xp_ref[...], xs_ref[...])
    slabs = lambda ref: jnp.concatenate([ref[j] for j in range(ref.shape[0])], axis=1)
    y = jnp.where(is_p, slabs(yp_ref), slabs(ys_ref))
    att = jnp.where(is_p, ap_ref[...], as_ref[...])

    y = jax.nn.gelu(y)
    glu = jnp.dot(y.astype(BF16), wglu_ref[...], preferred_element_type=F32) + bglu_ref[...]
    ssm = y * jax.nn.sigmoid(glu)
    mix = jnp.dot(_rms(ssm, gs_ref[...]).astype(BF16), wo1_ref[...], preferred_element_type=F32)
    mix += jnp.dot(_rms(att, ga_ref[...]).astype(BF16), wo2_ref[...], preferred_element_type=F32)
    h = x + mix
    h_ref[...] = h
    xn = _rms(h, g2_ref[...])
    xn_ref[...] = xn

    tm = x.shape[0]
    xh = xn.astype(BF16)
    xl = (xn - xh.astype(F32)).astype(BF16)
    logits = (jnp.dot(xh, wrh_ref[...], preferred_element_type=F32) + jnp.dot(xl, wrh_ref[...], preferred_element_type=F32)
              + jnp.dot(xh, wrl_ref[...], preferred_element_type=F32)) + br_ref[...]
    lane_i = lax.broadcasted_iota(jnp.int32, (tm, LANES), 1)
    lane = lane_i.astype(F32)
    big = float(LANES)
    first = lambda hit: jnp.min(jnp.where(hit, lane, big), axis=-1, keepdims=True)
    gl = jnp.where(lane_i < N_GROUPS, logits, -jnp.inf)
    gmax = jnp.max(gl, axis=-1, keepdims=True)
    grp = first(gl == gmax)
    p_sel = 1.0 / jnp.sum(jnp.exp(gl - gmax), axis=-1, keepdims=True)
    lane_grp = ((lane_i - N_GROUPS) >> 3).astype(F32)
    in_grp = (lane_i >= N_GROUPS) & (lane_i < N_GROUPS + N_EXPERTS) & (lane_grp == grp)
    el = jnp.where(in_grp, logits, -jnp.inf)
    m1 = jnp.max(el, axis=-1, keepdims=True)
    i1 = first(el == m1)
    el2 = jnp.where(lane == i1, -jnp.inf, el)
    m2 = jnp.max(el2, axis=-1, keepdims=True)
    i2 = first(el2 == m2)
    e21 = jnp.exp(m2 - m1)
    w1 = p_sel / (1.0 + e21)
    w2 = p_sel * e21 / (1.0 + e21)
    e1, e2 = i1 - N_GROUPS, i2 - N_GROUPS

    oh1, oh2 = lane == e1, lane == e2
    oh = jnp.where(oh1 | oh2, 1.0, 0.0)
    before = jnp.dot(tri_ref[...], oh.astype(BF16), preferred_element_type=F32) + run_scr[...]
    r1 = jnp.sum(jnp.where(oh1, before, 0.0), axis=-1, keepdims=True)
    r2 = jnp.sum(jnp.where(oh2, before, 0.0), axis=-1, keepdims=True)
    run_scr[...] += jnp.sum(oh, axis=0, keepdims=True)
    cnt_ref[...] = run_scr[...]
    meta = jnp.zeros((tm, LANES), F32)
    for k, v in enumerate((e1, e2, w1, w2, r1, r2)):
        meta = jnp.where(lane_i == k, v, meta)
    meta_ref[...] = meta


def _merge(x_p, x_s, y_p, y_s, a_p, a_s, wglu, bglu, gs, ga, wo1, wo2, g2, wr, br, tm):
    wrh = wr.astype(BF16)
    wrl = (wr - wrh.astype(F32)).astype(BF16)
    n_p, d = x_p.shape
    n_s = x_s.shape[0]
    npt, nst = n_p // tm, n_s // tm
    n_all = n_p + n_s
    tri = (lax.broadcasted_iota(jnp.int32, (tm, tm), 0) > lax.broadcasted_iota(jnp.int32, (tm, tm), 1)).astype(BF16)
    prow = lambda i: (jnp.minimum(i, npt - 1), 0)
    srow = lambda i: (jnp.maximum(i - npt, 0), 0)
    row = lambda i: (i, 0)
    const = lambda i: (0, 0)
    w = a_p.shape[1]
    nsl = y_p.shape[0]
    full = lambda a: pl.BlockSpec(a.shape, const)
    return pl.pallas_call(
        functools.partial(_merge_kernel, n_prompt_tiles=npt),
        grid=(npt + nst,),
        in_specs=[pl.BlockSpec((tm, d), prow), pl.BlockSpec((tm, d), srow),
                  pl.BlockSpec((nsl, tm, LANES), lambda i: (0, jnp.minimum(i, npt - 1), 0)),
                  pl.BlockSpec((nsl, tm, LANES), lambda i: (0, jnp.maximum(i - npt, 0), 0)),
                  pl.BlockSpec((tm, w), prow), pl.BlockSpec((tm, w), srow),
                  full(wglu), full(bglu), full(gs), full(ga), full(wo1), full(wo2), full(g2), full(wrh), full(wrl), full(br),
                  full(tri)],
        out_specs=[pl.BlockSpec((tm, d), row), pl.BlockSpec((tm, d), row), pl.BlockSpec((tm, LANES), row),
                   pl.BlockSpec((1, LANES), const)],
        out_shape=[jax.ShapeDtypeStruct((n_all, d), F32), jax.ShapeDtypeStruct((n_all, d), F32),
                   jax.ShapeDtypeStruct((n_all, LANES), F32), jax.ShapeDtypeStruct((1, LANES), F32)],
        scratch_shapes=[pltpu.VMEM((1, LANES), F32)],
        compiler_params=_cparams(("arbitrary",)), name="merge",
    )(x_p, x_s, y_p, y_s, a_p, a_s, wglu, bglu, gs, ga, wo1, wo2, g2, wrh, wrl, br, tri)


def _start_row_gather(idx_ref, base, src_hbm, dst, sem, nrows):
    for r in range(nrows):
        pltpu.make_async_copy(src_hbm.at[pl.ds(idx_ref[base + r], 1)], dst.at[pl.ds(r, 1)], sem).start()


def _wait_row_gather(src_hbm, dst, sem, nrows):
    pltpu.make_async_copy(src_hbm.at[pl.ds(0, nrows)], dst, sem).wait()


def _pos_kernel(meta_ref, offs_ref, p1_ref, p2_ref):
    meta = meta_ref[...]
    tm = meta.shape[0]
    lane_i = lax.broadcasted_iota(jnp.int32, (tm, LANES), 1)
    lane = lane_i.astype(F32)
    diag = lax.broadcasted_iota(jnp.int32, (tm, LANES), 0) % LANES == lane_i

    def dense(e, r):
        pos = jnp.sum(jnp.where(lane == e, offs_ref[...], 0.0), axis=-1, keepdims=True) + r
        spread = jnp.where(diag, pos, 0.0).reshape(tm // LANES, LANES, LANES)
        return jnp.sum(spread, axis=1).astype(jnp.int32)

    p1_ref[...] = dense(meta[:, 0:1], meta[:, 4:5])
    p2_ref[...] = dense(meta[:, 1:2], meta[:, 5:6])


def _positions(meta, offs_vec, tm):
    n = meta.shape[0]
    rows = tm // LANES
    return pl.pallas_call(
        _pos_kernel,
        grid=(n // tm,),
        in_specs=[pl.BlockSpec((tm, LANES), lambda i: (i, 0)), pl.BlockSpec((1, LANES), lambda i: (0, 0))],
        out_specs=[pl.BlockSpec((rows, LANES), lambda i: (i, 0)), pl.BlockSpec((rows, LANES), lambda i: (i, 0))],
        out_shape=[jax.ShapeDtypeStruct((n // LANES, LANES), jnp.int32)] * 2,
        compiler_params=_cparams(("parallel",)), name="positions",
    )(meta, offs_vec)


def _dispatch_kernel(p1_ref, p2_ref, tend_ref, xn_ref, xs_hbm, zbuf, sem, *, tmd, tme):
    base = pl.program_id(0) * tmd

    @pl.when(pl.program_id(0) == 0)
    def _():
        zbuf[...] = jnp.zeros(zbuf.shape, zbuf.dtype)

        def last_tile(e, carry, wait):
            end = tend_ref[e]
            begin = jnp.where(e == 0, 0, tend_ref[jnp.maximum(e - 1, 0)])

            @pl.when(end > begin)
            def _():
                cp = pltpu.make_async_copy(zbuf, xs_hbm.at[pl.ds((end - 1) * tme, tme)], sem)
                cp.wait() if wait else cp.start()
            return carry

        def spare_tile(t, carry, wait):
            cp = pltpu.make_async_copy(zbuf, xs_hbm.at[pl.ds(t * tme, tme)], sem)
            cp.wait() if wait else cp.start()
            return carry

        n_used, n_tiles = tend_ref[N_EXPERTS - 1], xs_hbm.shape[0] // tme
        for wait in (False, True):
            lax.fori_loop(0, N_EXPERTS, functools.partial(last_tile, wait=wait), 0)
            lax.fori_loop(n_used, n_tiles, functools.partial(spare_tile, wait=wait), 0)

    for r in range(tmd):
        src = xn_ref.at[pl.ds(r, 1)]
        pltpu.make_async_copy(src, xs_hbm.at[pl.ds(p1_ref[base + r], 1)], sem).start()
        pltpu.make_async_copy(src, xs_hbm.at[pl.ds(p2_ref[base + r], 1)], sem).start()
    for _ in range(2):
        pltpu.make_async_copy(xn_ref, xs_hbm.at[pl.ds(0, tmd)], sem).wait()


def _dispatch(pos1, pos2, tile_end, xn, n_rows, tmd, tme):
    n, d = xn.shape
    grid_spec = pltpu.PrefetchScalarGridSpec(
        num_scalar_prefetch=3,
        grid=(n // tmd,),
        in_specs=[pl.BlockSpec((tmd, d), lambda i, p1, p2, te: (i, 0))],
        out_specs=pl.BlockSpec(memory_space=pl.ANY),
        scratch_shapes=[pltpu.VMEM((tme, d), xn.dtype), pltpu.SemaphoreType.DMA],
    )
    return pl.pallas_call(
        functools.partial(_dispatch_kernel, tmd=tmd, tme=tme),
        grid_spec=grid_spec,
        out_shape=jax.ShapeDtypeStruct((n_rows, d), xn.dtype),
        compiler_params=_cparams(("arbitrary",)), name="dispatch",
    )(pos1, pos2, tile_end, xn)


def _experts_kernel(te_ref, nu_ref, x_ref, wg_ref, wu_ref, wd_ref, o_ref, wg_scr, wu_scr, wd_scr):
    i = pl.program_id(0)

    @pl.when((i == 0) | (te_ref[i] != te_ref[jnp.maximum(i - 1, 0)]))
    def _():
        wg_scr[...] = wg_ref[0].astype(BF16)
        wu_scr[...] = wu_ref[0].astype(BF16)
        wd_scr[...] = wd_ref[0].astype(BF16)

    @pl.when(i < nu_ref[0])
    def _():
        x = x_ref[...].astype(BF16)
        hg = jnp.dot(x, wg_scr[...], preferred_element_type=F32)
        hu = jnp.dot(x, wu_scr[...], preferred_element_type=F32)
        hh = (jax.nn.silu(hg) * hu).astype(BF16)
        o_ref[...] = jnp.dot(hh, wd_scr[...], preferred_element_type=F32)

    @pl.when(i >= nu_ref[0])
    def _():
        o_ref[...] = jnp.zeros(o_ref.shape, F32)


def _experts(tile_expert, n_used, xs, w_gate, w_up, w_down, tme):
    nt = tile_expert.shape[0]
    ne, d, de = w_gate.shape
    wmap = lambda i, te, nu: (te[i], 0, 0)
    grid_spec = pltpu.PrefetchScalarGridSpec(
        num_scalar_prefetch=2,
        grid=(nt,),
        in_specs=[pl.BlockSpec((tme, d), lambda i, te, nu: (jnp.minimum(i, nu[0] - 1), 0)),
                  pl.BlockSpec((1, d, de), wmap), pl.BlockSpec((1, d, de), wmap), pl.BlockSpec((1, de, d), wmap)],
        out_specs=pl.BlockSpec((tme, d), lambda i, te, nu: (i, 0)),
        scratch_shapes=[pltpu.VMEM((d, de), BF16), pltpu.VMEM((d, de), BF16), pltpu.VMEM((de, d), BF16)],
    )
    return pl.pallas_call(
        _experts_kernel,
        grid_spec=grid_spec,
        out_shape=jax.ShapeDtypeStruct((nt * tme, d), F32),
        compiler_params=_cparams(("arbitrary",)), name="experts",
    )(tile_expert, n_used, xs, w_gate, w_up, w_down)


def _combine_kernel(p1_ref, p2_ref, h_ref, meta_ref, eo_hbm, y_ref, buf, sems, *, tmc, nsteps):
    i = pl.program_id(0)
    slot = i % 2

    def start(step, slot_):
        _start_row_gather(p1_ref, step * tmc, eo_hbm, buf.at[slot_, pl.ds(0, tmc)], sems.at[slot_], tmc)
        _start_row_gather(p2_ref, step * tmc, eo_hbm, buf.at[slot_, pl.ds(tmc, tmc)], sems.at[slot_], tmc)

    @pl.when(i == 0)
    def _():
        start(0, 0)

    @pl.when(i + 1 < nsteps)
    def _():
        start(i + 1, 1 - slot)

    _wait_row_gather(eo_hbm, buf.at[slot], sems.at[slot], 2 * tmc)
    meta = meta_ref[...]
    y_ref[...] = h_ref[...] + meta[:, 2:3] * buf[slot, :tmc, :] + meta[:, 3:4] * buf[slot, tmc:, :]


def _combine(pos1, pos2, h_all, meta_all, eo, row0, tmc):
    n = pos1.shape[0]
    d = h_all.shape[1]
    nsteps = n // tmc
    off = row0 // tmc
    grid_spec = pltpu.PrefetchScalarGridSpec(
        num_scalar_prefetch=2,
        grid=(nsteps,),
        in_specs=[pl.BlockSpec((tmc, d), lambda i, p1, p2: (i + off, 0)),
                  pl.BlockSpec((tmc, LANES), lambda i, p1, p2: (i + off, 0)),
                  pl.BlockSpec(memory_space=pl.ANY)],
        out_specs=pl.BlockSpec((tmc, d), lambda i, p1, p2: (i, 0)),
        scratch_shapes=[pltpu.VMEM((2, 2 * tmc, d), F32), pltpu.SemaphoreType.DMA((2,))],
    )
    return pl.pallas_call(
        functools.partial(_combine_kernel, tmc=tmc, nsteps=nsteps),
        grid_spec=grid_spec,
        out_shape=jax.ShapeDtypeStruct((n, d), F32),
        compiler_params=_cparams(("arbitrary",)), name="combine",
    )(pos1, pos2, h_all, meta_all, eo)


def _pick_tile(n, pref):
    t = min(pref, n)
    while n % t:
        t //= 2
    return t


def _layer(x_p, x_s, cache_ckv, cache_kr, st_re, st_im, page_table, p):
    nb_p, t_p, d = x_p.shape
    nb_s, t_s, _ = x_s.shape
    n_p, n_s = nb_p * t_p, nb_s * t_s
    past_len = page_table.shape[1] * PAGE
    g = p['ssm_a_re'].shape[0]
    ssm_w = g * SSM_CH
    q_rank = p['w_uq'].shape[0]
    kv_rank = p['w_uk'].shape[0]

    w_in = p['w_in']
    o3 = ssm_w + q_rank + kv_rank
    w_pad = jnp.zeros((d, o3 + LANES), F32).at[:, :o3].set(w_in[:, :o3])
    w_pad = w_pad.at[:, o3 + NOPE:o3 + QK].set(w_in[:, o3:]).astype(BF16)
    row2 = lambda v: v.reshape(1, -1).astype(F32)
    padq = lambda v: jnp.pad(v, (0, LANES - QK)).reshape(1, LANES)
    wq = jnp.pad(p['w_uq'].transpose(1, 0, 2), ((0, 0), (0, 0), (0, LANES - QK))).astype(BF16)
    wk = jnp.pad(p['w_uk'].transpose(1, 0, 2), ((0, 0), (0, 0), (0, LANES - NOPE))).astype(BF16)
    wvp = p['w_uv'].reshape(kv_rank, N_HEADS // 2, 2 * V_DIM).transpose(1, 0, 2).astype(BF16)
    wv_h = p['w_uv'].transpose(1, 0, 2)
    wv8 = jnp.stack([jnp.pad(wv_h[h], ((0, 0), (V_DIM - _ones_lane(h), _ones_lane(h)))) for h in range(N_HEADS)])
    wv8 = wv8.astype(BF16)
    wukt_all = p['w_uk'].reshape(kv_rank, N_HEADS * NOPE).T.astype(BF16)
    wukt = jnp.pad(p['w_uk'].transpose(1, 2, 0), ((0, 0), (0, LANES - NOPE), (0, 0))).astype(BF16)
    sel = (jnp.arange(LANES)[:, None] == NOPE + jnp.arange(ROPE)[None, :]).astype(BF16)
    gq, gk = padq(p['qk_norm_q']), padq(p['qk_norm_k'])
    ssm_params = (p['ssm_a_re'], p['ssm_a_im'], p['ssm_log_dt'], p['ssm_b_re'], p['ssm_b_im'],
                  p['ssm_c_re'], p['ssm_c_im'], p['ssm_d'])

    tm_p = _pick_tile(t_p, 512)
    tab_p = _rope_tables(jnp.arange(t_p))
    u, cq, ckv_p, krp_p = _in_proj(x_p.reshape(n_p, d), row2(p['norm1_g']), w_pad, row2(p['q_norm_g']),
                                   row2(p['kv_norm_g']), tab_p, tm_p, ssm_w, q_rank, kv_rank)
    lc_p = _pick_tile(t_p, 8)
    nsl = ssm_w // LANES
    s5w_p = _s5_weights(*ssm_params, lc_p)
    y_ssm_p, hf_p = _s5(u, jnp.zeros((nsl, nb_p, 2 * GPB * SSM_P), F32), s5w_p, nb_p, t_p, lc_p)
    tq = _pick_tile(t_p, 512)
    att_p = _attn_prompt(cq, ckv_p, krp_p, tab_p, wq, wk, wv8, gq, gk, nb_p, t_p, tq, tq)

    tm_s = _pick_tile(n_s, 512)
    pos_s = past_len + jnp.arange(t_s)
    tab_s = tuple(jnp.tile(a, (tm_s // t_s, 1)) for a in _rope_tables(pos_s))
    u, cq, ckv_s, krp_s = _in_proj(x_s.reshape(n_s, d), row2(p['norm1_g']), w_pad, row2(p['q_norm_g']),
                                   row2(p['kv_norm_g']), tab_s, tm_s, ssm_w, q_rank, kv_rank)
    slab_state = lambda s: s.astype(F32).reshape(nb_s, nsl, GPB * SSM_P).transpose(1, 0, 2)
    h0 = jnp.concatenate([slab_state(st_re), slab_state(st_im)], axis=-1)
    s5w_s = s5w_p if t_s == lc_p else _s5_weights(*ssm_params, t_s)
    y_ssm_s, hf_s = _s5(u, h0, s5w_s, nb_s, t_s, t_s)
    tab_q = tuple(jnp.tile(a, (n_s // tm_s, 1)) for a in tab_s)
    qa, qr = _q_sample(cq, tab_q, wq, wukt, sel, gq, gk)
    kr_s = krp_s[:, NOPE:QK]
    cnew = jnp.pad(ckv_s.reshape(nb_s, t_s, kv_rank), ((0, 0), (0, PAGE - t_s), (0, 0)))
    krnew = jnp.pad(kr_s.reshape(nb_s, t_s, ROPE), ((0, 0), (0, PAGE - t_s), (0, 0))).swapaxes(1, 2)
    pc = _pick_tile(page_table.shape[1], 32)
    att_s = _attn_sample(page_table, qa, qr, cnew, krnew, wukt_all, wvp, cache_ckv, jnp.swapaxes(cache_kr, 1, 2),
                         t_s, pc)

    tm = _pick_tile(math.gcd(n_p, n_s), 512)
    w_out = p['w_out'].astype(BF16)
    wr = jnp.zeros((d, LANES), F32).at[:, :N_GROUPS].set(p['w_router_group'])
    wr = wr.at[:, N_GROUPS:N_GROUPS + N_EXPERTS].set(p['w_router_expert'])
    br = jnp.zeros((1, LANES), F32).at[0, :N_GROUPS].set(p['b_router_group'])
    br = br.at[0, N_GROUPS:N_GROUPS + N_EXPERTS].set(p['b_router_expert'])
    h_all, xn_all, meta, cnt = _merge(
        x_p.reshape(n_p, d), x_s.reshape(n_s, d), y_ssm_p, y_ssm_s, att_p, att_s,
        p['ssm_w_glu'].astype(BF16), row2(p['ssm_b_glu']), row2(p['out_norm_ssm']), row2(p['out_norm_attn']),
        w_out[:ssm_w], w_out[ssm_w:], row2(p['norm2_g']), wr, br, tm)

    n_all = n_p + n_s
    tme = 512
    nt = (2 * n_all) // tme + N_EXPERTS
    counts = cnt[0, :N_EXPERTS].astype(jnp.int32)
    tiles_per = (counts + tme - 1) // tme
    tile_end = jnp.cumsum(tiles_per)
    tile_start = tile_end - tiles_per
    tile_id = jnp.arange(nt, dtype=jnp.int32)
    tile_expert = jnp.minimum(jnp.sum(tile_end[None, :] <= tile_id[:, None], axis=1), N_EXPERTS - 1).astype(jnp.int32)
    n_used = tile_end[-1:].astype(jnp.int32)
    offs_vec = jnp.zeros((1, LANES), F32).at[0, :N_EXPERTS].set((tile_start * tme).astype(F32))
    tmd = _pick_tile(n_all, 1024)
    pos1, pos2 = (a.reshape(-1) for a in _positions(meta, offs_vec, tmd))
    xs = _dispatch(pos1, pos2, tile_end.astype(jnp.int32), xn_all, nt * tme, tmd, tme)
    eo = _experts(tile_expert, n_used, xs, p['w_gate'], p['w_up'], p['w_down'], tme)

    tmc = _pick_tile(math.gcd(n_p, n_s), 256)
    y_p = _combine(pos1[:n_p], pos2[:n_p], h_all, meta, eo, 0, tmc)
    y_s = _combine(pos1[n_p:], pos2[n_p:], h_all, meta, eo, n_p, tmc)

    n_pg = t_p // PAGE

    def split(hf):
        nb = hf.shape[1]
        half = GPB * SSM_P
        unslab = lambda a: a.reshape(nsl, nb, GPB, SSM_P).transpose(1, 0, 2, 3).reshape(nb, g, SSM_P)
        return unslab(hf[..., :half]), unslab(hf[..., half:])

    hp_re, hp_im = split(hf_p)
    hs_re, hs_im = split(hf_s)
    return (y_p.reshape(nb_p, t_p, d), y_s.reshape(nb_s, t_s, d),
            ckv_p.reshape(nb_p, n_pg, PAGE, kv_rank), krp_p[:, NOPE:QK].reshape(nb_p, n_pg, PAGE, ROPE),
            hp_re, hp_im, ckv_s.reshape(nb_s, t_s, kv_rank), kr_s.reshape(nb_s, t_s, ROPE), hs_re, hs_im)


_PARAM_NAMES = ('norm1_g', 'w_in', 'ssm_a_re', 'ssm_a_im', 'ssm_log_dt', 'ssm_b_re', 'ssm_b_im', 'ssm_c_re',
                'ssm_c_im', 'ssm_d', 'ssm_w_glu', 'ssm_b_glu', 'q_norm_g', 'w_uq', 'kv_norm_g', 'w_uk', 'w_uv',
                'qk_norm_q', 'qk_norm_k', 'out_norm_ssm', 'out_norm_attn', 'w_out', 'norm2_g', 'w_router_group',
                'b_router_group', 'w_router_expert', 'b_router_expert', 'w_gate', 'w_up', 'w_down')


def kernel(x_prompt, x_sample, cache_ckv, cache_krope, state_ssm_re, state_ssm_im, page_table, norm1_g, w_in, ssm_a_re, ssm_a_im, ssm_log_dt, ssm_b_re, ssm_b_im, ssm_c_re, ssm_c_im, ssm_d, ssm_w_glu, ssm_b_glu, q_norm_g, w_uq, kv_norm_g, w_uk, w_uv, qk_norm_q, qk_norm_k, out_norm_ssm, out_norm_attn, w_out, norm2_g, w_router_group, b_router_group, w_router_expert, b_router_expert, w_gate, w_up, w_down):
    params = (norm1_g, w_in, ssm_a_re, ssm_a_im, ssm_log_dt, ssm_b_re, ssm_b_im, ssm_c_re, ssm_c_im, ssm_d,
              ssm_w_glu, ssm_b_glu, q_norm_g, w_uq, kv_norm_g, w_uk, w_uv, qk_norm_q, qk_norm_k, out_norm_ssm,
              out_norm_attn, w_out, norm2_g, w_router_group, b_router_group, w_router_expert, b_router_expert,
              w_gate, w_up, w_down)
    depth = w_in.shape[0]
    h_p, h_s = x_prompt, x_sample
    outs = [[] for _ in range(8)]
    for layer in range(depth):
        p = {k: v[layer] for k, v in zip(_PARAM_NAMES, params)}
        res = _layer(h_p, h_s, cache_ckv[layer], cache_krope[layer], state_ssm_re[layer], state_ssm_im[layer],
                     page_table, p)
        h_p, h_s = res[0], res[1]
        for acc, r in zip(outs, res[2:]):
            acc.append(r)
    return (h_p, h_s) + tuple(jnp.stack(o) for o in outs)
```

```python
import functools
import math

import jax
import jax.numpy as jnp
from jax import lax
from jax.experimental import pallas as pl
from jax.experimental.pallas import tpu as pltpu

F32 = jnp.float32
BF16 = jnp.bfloat16
HIGHEST = lax.Precision.HIGHEST

LANES = 128
SUBLANES = 8
VMEM_LIMIT = 56 * 1024 * 1024

EPS = 1e-6
ROPE_BASE = 10000.0
PAGE = 128
SSM_CH = 16
SSM_P = 64
GPB = LANES // SSM_CH
S5_MAX_TOKENS = 8192
N_HEADS = 8
NOPE = 64
ROPE = 32
QK = NOPE + ROPE
V_DIM = 64
N_GROUPS = 4
EXP_PER_GROUP = 8
N_EXPERTS = N_GROUPS * EXP_PER_GROUP
ATTN_HEADS_PER_BODY = 8

NT_DIMS = (((1,), (1,)), ((), ()))
LOG2E = math.log2(math.e)


def _cparams(sem, vmem=VMEM_LIMIT):
    return pltpu.CompilerParams(dimension_semantics=sem, vmem_limit_bytes=vmem)


def _rms(x, g):
    return x * lax.rsqrt(jnp.mean(x * x, axis=-1, keepdims=True) + EPS) * g


def _rms_qk(x, g):
    return x * lax.rsqrt(jnp.sum(x * x, axis=-1, keepdims=True) * (1.0 / QK) + EPS) * g


def _rope(x, cos, sinp, sinm):
    return x * cos + pltpu.roll(x, 16, 1) * sinp + pltpu.roll(x, LANES - 16, 1) * sinm


def _rope_tables(pos):
    inv_freq = ROPE_BASE ** (-jnp.arange(0, ROPE, 2, dtype=F32) / ROPE)
    ang = pos.astype(F32)[:, None] * inv_freq[None, :]
    c, s = jnp.cos(ang), jnp.sin(ang)
    t = pos.shape[0]
    z = lambda n: jnp.zeros((t, n), F32)
    cosf = jnp.concatenate([jnp.ones((t, NOPE), F32), c, c, z(LANES - QK)], axis=1)
    sinp = jnp.concatenate([z(NOPE + ROPE // 2), s, z(LANES - QK)], axis=1)
    sinm = jnp.concatenate([z(NOPE), -s, z(LANES - QK + ROPE // 2)], axis=1)
    return cosf, sinp, sinm


def _in_proj_kernel(x_ref, g1_ref, w_ref, gq_ref, gkv_ref, cos_ref, sinp_ref, sinm_ref,
                    u_ref, cq_ref, ckv_ref, kr_ref, *, ssm_w, q_rank, kv_rank):
    n = _rms(x_ref[...], g1_ref[...])
    z = jnp.dot(n.astype(BF16), w_ref[...], preferred_element_type=F32)
    o1, o2, o3 = ssm_w, ssm_w + q_rank, ssm_w + q_rank + kv_rank
    for j in range(ssm_w // LANES):
        u_ref[j] = z[:, j * LANES:(j + 1) * LANES]
    cq_ref[...] = _rms(z[:, o1:o2], gq_ref[...]).astype(cq_ref.dtype)
    ckv_ref[...] = _rms(z[:, o2:o3], gkv_ref[...])
    kr_ref[...] = _rope(z[:, o3:o3 + LANES], cos_ref[...], sinp_ref[...], sinm_ref[...])


def _in_proj(x, g1, w_pad, gq, gkv, tables, tm, ssm_w, q_rank, kv_rank):
    n, d = x.shape
    period = tables[0].shape[0]
    nper = period // tm
    row = lambda i: (i, 0)
    const = lambda i: (0, 0)
    tab = pl.BlockSpec((tm, LANES), lambda i: (i % nper, 0))
    return pl.pallas_call(
        functools.partial(_in_proj_kernel, ssm_w=ssm_w, q_rank=q_rank, kv_rank=kv_rank),
        grid=(n // tm,),
        in_specs=[pl.BlockSpec((tm, d), row), pl.BlockSpec((1, d), const),
                  pl.BlockSpec(w_pad.shape, const), pl.BlockSpec((1, q_rank), const),
                  pl.BlockSpec((1, kv_rank), const), tab, tab, tab],
        out_specs=[pl.BlockSpec((ssm_w // LANES, tm, LANES), lambda i: (0, i, 0)), pl.BlockSpec((tm, q_rank), row),
                   pl.BlockSpec((tm, kv_rank), row), pl.BlockSpec((tm, LANES), row)],
        out_shape=[jax.ShapeDtypeStruct((ssm_w // LANES, n, LANES), F32), jax.ShapeDtypeStruct((n, q_rank), BF16),
                   jax.ShapeDtypeStruct((n, kv_rank), F32), jax.ShapeDtypeStruct((n, LANES), F32)],
        compiler_params=_cparams(("parallel",)), name="in_proj",
    )(x, g1, w_pad, gq, gkv, *tables)


def _s5_weights(a_re, a_im, log_dt, b_re, b_im, c_re, c_im, d, lc):
    g, p = a_re.shape
    dt = jnp.exp(log_dt)[:, None]
    den = a_re * a_re + a_im * a_im
    mag, ang = jnp.exp(a_re * dt), a_im * dt
    lb_re, lb_im = mag * jnp.cos(ang), mag * jnp.sin(ang)
    num_re = lb_re - 1.0
    coef_re = (num_re * a_re + lb_im * a_im) / den
    coef_im = (lb_im * a_re - num_re * a_im) / den
    bb_re = coef_re[..., None] * b_re - coef_im[..., None] * b_im
    bb_im = coef_re[..., None] * b_im + coef_im[..., None] * b_re

    def lam_pow(k):
        kk = k.astype(F32)[:, None, None]
        m = jnp.exp(a_re * dt * kk)
        return m * jnp.cos(ang * kk), m * jnp.sin(ang * kk)

    steps = jnp.arange(lc)
    pr, pi = lam_pow(lc - 1 - steps)
    bz_re = pr[..., None] * bb_re[None] - pi[..., None] * bb_im[None]
    bz_im = pr[..., None] * bb_im[None] + pi[..., None] * bb_re[None]
    bz = jnp.concatenate([bz_re, bz_im], axis=2)
    bz = bz.transpose(1, 0, 3, 2).reshape(g, lc * SSM_CH, 2 * p)
    qr, qi = lam_pow(steps + 1)
    cl_re = c_re[None] * qr[:, :, None, :] - c_im[None] * qi[:, :, None, :]
    cl_im = c_re[None] * qi[:, :, None, :] + c_im[None] * qr[:, :, None, :]
    cz = jnp.concatenate([cl_re, -cl_im], axis=3)
    cz = cz.transpose(1, 3, 0, 2).reshape(g, 2 * p, lc * SSM_CH)
    tr, ti = lam_pow(steps)
    cb_re = jnp.einsum('gcp,kgp,gpd->gkcd', c_re, tr, bb_re, precision=HIGHEST)
    cb_re -= jnp.einsum('gcp,kgp,gpd->gkcd', c_re, ti, bb_im, precision=HIGHEST)
    cb_re -= jnp.einsum('gcp,kgp,gpd->gkcd', c_im, tr, bb_im, precision=HIGHEST)
    cb_re -= jnp.einsum('gcp,kgp,gpd->gkcd', c_im, ti, bb_re, precision=HIGHEST)
    tau = steps[None, :] - steps[:, None]
    ksel = cb_re[:, jnp.clip(tau, 0, lc - 1)]
    ksel = jnp.where((tau >= 0)[None, :, :, None, None], ksel, 0.0)
    tz = ksel.transpose(0, 1, 4, 2, 3).reshape(g, lc * SSM_CH, lc * SSM_CH)
    assert lc * SSM_CH == LANES and 2 * p == LANES
    nsl = g // GPB
    sw = GPB * LANES
    col = jnp.arange(sw)
    rep_state = (jnp.arange(LANES)[:, None] == (col // (sw // 2)) * p + col % p).astype(BF16)
    rep_token = (jnp.arange(LANES)[:, None] == (col // LANES) * SSM_CH + col % SSM_CH).astype(BF16)
    grp_state = (col // p) % GPB
    grp_token = (col // SSM_CH) % GPB

    def slab(rows, rep, row_grp, col_grp):
        full = jnp.einsum('jrk,kc->jrc', rows.astype(BF16), rep, preferred_element_type=F32)
        return jnp.where(row_grp[:, None] == col_grp[None, :], full, 0.0).astype(BF16)

    bz_rows = bz.reshape(nsl, GPB, lc, SSM_CH, LANES).transpose(0, 2, 1, 3, 4).reshape(nsl, sw, LANES)
    tz_rows = tz.reshape(nsl, GPB, lc, SSM_CH, LANES).transpose(0, 2, 1, 3, 4).reshape(nsl, sw, LANES)
    cz_rows = cz.reshape(nsl, GPB, 2, p, LANES).transpose(0, 2, 1, 3, 4).reshape(nsl, sw, LANES)
    bz_s = slab(bz_rows, rep_state, grp_token, grp_state)
    wy = jnp.concatenate([slab(cz_rows, rep_token, grp_state, grp_token),
                          slab(tz_rows, rep_token, grp_token, grp_token)], axis=1)
    lr, li = lam_pow(jnp.array([lc]))
    lam_re = lr[0].reshape(nsl, 1, GPB * p)
    lam_im = li[0].reshape(nsl, 1, GPB * p)
    dvec = jnp.tile(d.reshape(nsl, 1, LANES), (1, 1, lc))
    return bz_s, wy, lam_re, lam_im, dvec


def _s5_kernel(u_ref, bz_ref, wy_ref, lr_ref, li_ref, d_ref, h0_ref, y_ref, hf_ref, s_scr, hp_scr,
               *, nk, bs, lc, rb):
    nh = s_scr.shape[0] // 2
    blk = lambda a, c: a[:, c * LANES:(c + 1) * LANES]

    def ucat(r0):
        return jnp.concatenate([u_ref[0, pl.ds(r0 * lc + t, rb, stride=lc), :] for t in range(lc)], axis=1)

    def scan_rows(r0):
        if nk == 1:
            return pl.ds(r0, rb)
        seq = r0 // nk
        return pl.ds((r0 - seq * nk) * bs + seq, rb, stride=bs)

    def phase1(i, _):
        r0 = pl.multiple_of(i * rb, rb)
        s = jnp.dot(ucat(r0).astype(BF16), bz_ref[0], preferred_element_type=F32)
        for c in range(2 * nh):
            s_scr[c, scan_rows(r0), :] = blk(s, c)
        return 0

    lax.fori_loop(0, (bs * nk) // rb, phase1, 0)
    lr, li = lr_ref[0], li_ref[0]

    def step(k, h):
        rows = pl.ds(pl.multiple_of(k * bs, bs), bs)
        new = [None] * (2 * nh)
        for c in range(nh):
            h_re, h_im = h[c], h[nh + c]
            s_re, s_im = s_scr[c, rows, :], s_scr[nh + c, rows, :]
            hp_scr[c, rows, :] = h_re
            hp_scr[nh + c, rows, :] = h_im
            new[c] = blk(lr, c) * h_re - blk(li, c) * h_im + s_re
            new[nh + c] = blk(lr, c) * h_im + blk(li, c) * h_re + s_im
        return tuple(new)

    h0 = h0_ref[0, 0]
    hf = lax.fori_loop(0, nk, step, tuple(blk(h0, c) for c in range(2 * nh)), unroll=math.gcd(nk, 4))
    hf_ref[0, 0] = jnp.concatenate(hf, axis=1)

    def phase3(i, _):
        r0 = pl.multiple_of(i * rb, rb)
        uc = ucat(r0)
        hprev = jnp.concatenate([hp_scr[c, scan_rows(r0), :] for c in range(2 * nh)], axis=1)
        lhs = jnp.concatenate([hprev.astype(BF16), uc.astype(BF16)], axis=1)
        y = jnp.dot(lhs, wy_ref[0], preferred_element_type=F32) + d_ref[0] * uc
        for t in range(lc):
            y_ref[0, pl.ds(r0 * lc + t, rb, stride=lc), :] = y[:, t * LANES:(t + 1) * LANES]
        return 0

    lax.fori_loop(0, (bs * nk) // rb, phase3, 0)


def _s5(u, h0, weights, nb, t, lc):
    bz, wy, lam_re, lam_im, dvec = weights
    nsl = u.shape[0]
    nk = t // lc
    sw = h0.shape[-1]
    bs = nb
    while bs * t > S5_MAX_TOKENS and bs % 2 == 0:
        bs //= 2
    rb = _pick_tile(bs, 256) if nk == 1 else _pick_tile(nk, 256)
    h0 = h0.reshape(nsl, nb // bs, bs, sw)
    slab = lambda j, b: (j, 0, 0)
    tok = lambda j, b: (j, b, 0)
    st = lambda j, b: (j, b, 0, 0)
    y, hf = pl.pallas_call(
        functools.partial(_s5_kernel, nk=nk, bs=bs, lc=lc, rb=rb),
        grid=(nsl, nb // bs),
        in_specs=[pl.BlockSpec((1, bs * t, LANES), tok), pl.BlockSpec((1,) + bz.shape[1:], slab),
                  pl.BlockSpec((1,) + wy.shape[1:], slab), pl.BlockSpec((1, 1, sw // 2), slab),
                  pl.BlockSpec((1, 1, sw // 2), slab), pl.BlockSpec((1, 1, lc * LANES), slab),
                  pl.BlockSpec((1, 1, bs, sw), st)],
        out_specs=[pl.BlockSpec((1, bs * t, LANES), tok), pl.BlockSpec((1, 1, bs, sw), st)],
        out_shape=[jax.ShapeDtypeStruct(u.shape, F32), jax.ShapeDtypeStruct(h0.shape, F32)],
        scratch_shapes=[pltpu.VMEM((sw // LANES, bs * nk, LANES), F32)] * 2,
        compiler_params=_cparams(("parallel", "parallel")), name="s5",
    )(u, bz, wy, lam_re, lam_im, dvec, h0)
    return y, hf.reshape(nsl, nb, sw)


def _ones_lane(h):
    return V_DIM if h % 2 == 0 else 0


def _attn_prompt_kernel(cq_ref, ckv_ref, krp_ref, cos_ref, sinp_ref, sinm_ref, wq_ref, wk_ref, wv_ref,
                        gq_ref, gk_ref, o_ref, k_scr, v_scr, q_scr, *, tq, tk, hb):
    qi = pl.program_id(1)

    @pl.when(qi == 0)
    def _():
        c = ckv_ref[...].astype(BF16)
        krp = krp_ref[...]
        for h in range(N_HEADS):
            kh = jnp.dot(c, wk_ref[h], preferred_element_type=F32) + krp
            k_scr[h] = _rms_qk(kh, gk_ref[...]).astype(BF16)
        vlane = lax.broadcasted_iota(jnp.int32, (1, LANES), 1)
        for h in range(N_HEADS):
            one = jnp.where(vlane == _ones_lane(h), 1.0, 0.0)
            v_scr[h] = (jnp.dot(c, wv_ref[h], preferred_element_type=F32) + one).astype(BF16)

    cq = cq_ref[...]
    cos, sinp, sinm = cos_ref[...], sinp_ref[...], sinm_ref[...]
    for h in range(N_HEADS):
        q = _rope(jnp.dot(cq, wq_ref[h], preferred_element_type=F32), cos, sinp, sinm)
        q_scr[h] = (_rms_qk(q, gq_ref[...]) * (LOG2E / math.sqrt(QK))).astype(BF16)

    assert tq == tk
    causal = lax.broadcasted_iota(jnp.int32, (tq, tk), 1) <= lax.broadcasted_iota(jnp.int32, (tq, tk), 0)
    lane = lax.broadcasted_iota(jnp.int32, (tq, LANES), 1)

    def fold(m, acc, q, h, ks, mask):
        s = lax.dot_general(q, k_scr[h, ks, :], NT_DIMS, preferred_element_type=F32)
        if mask is not None:
            s = jnp.where(mask, s, -jnp.inf)
        m_new = jnp.maximum(m, jnp.max(s, axis=-1, keepdims=True))
        p = jnp.exp2(s - m_new)
        acc = acc * jnp.exp2(m - m_new) + jnp.dot(p.astype(BF16), v_scr[h, ks, :], preferred_element_type=F32)
        return m_new, acc

    def kv_step(j, carry, heads, mask):
        ks = pl.ds(pl.multiple_of(j * tk, tk), tk)
        return tuple(fold(m, acc, q_scr[h], h, ks, mask) for (m, acc), h in zip(carry, heads))

    for h0 in range(0, N_HEADS, hb):
        heads = tuple(range(h0, h0 + hb))
        init = (jnp.full((tq, 1), -jnp.inf, F32), jnp.zeros((tq, LANES), F32))
        carry = lax.fori_loop(0, qi, functools.partial(kv_step, heads=heads, mask=None), (init,) * hb)
        carry = kv_step(qi, carry, heads, causal)
        outs = [acc / acc[:, _ones_lane(h):_ones_lane(h) + 1] for (_, acc), h in zip(carry, heads)]
        for k in range(0, hb, 2):
            hp = (h0 + k) // 2
            o_ref[:, hp * LANES:(hp + 1) * LANES] = jnp.where(lane < V_DIM, outs[k], outs[k + 1])


def _attn_prompt(cq, ckv, krp, tables, wq, wk, wvp, gq, gk, nb, t, tq, tk, hb=ATTN_HEADS_PER_BODY):
    n = cq.shape[0]
    nq = t // tq
    qrow = lambda b, i: (b * nq + i, 0)
    seq = lambda b, i: (b, 0)
    tab = pl.BlockSpec((tq, LANES), lambda b, i: (i, 0))
    c3 = lambda b, i: (0, 0, 0)
    c2 = lambda b, i: (0, 0)
    return pl.pallas_call(
        functools.partial(_attn_prompt_kernel, tq=tq, tk=tk, hb=hb),
        grid=(nb, nq),
        in_specs=[pl.BlockSpec((tq, cq.shape[1]), qrow), pl.BlockSpec((t, ckv.shape[1]), seq),
                  pl.BlockSpec((t, LANES), seq), tab, tab, tab,
                  pl.BlockSpec(wq.shape, c3), pl.BlockSpec(wk.shape, c3), pl.BlockSpec(wvp.shape, c3),
                  pl.BlockSpec((1, LANES), c2), pl.BlockSpec((1, LANES), c2)],
        out_specs=pl.BlockSpec((tq, N_HEADS * V_DIM), qrow),
        out_shape=jax.ShapeDtypeStruct((n, N_HEADS * V_DIM), F32),
        scratch_shapes=[pltpu.VMEM((N_HEADS, t, LANES), BF16), pltpu.VMEM((N_HEADS, t, LANES), BF16),
                        pltpu.VMEM((N_HEADS, tq, LANES), BF16)],
        compiler_params=_cparams(("parallel", "arbitrary")), name="attn_prompt",
    )(cq, ckv, krp, *tables, wq, wk, wvp, gq, gk)


def _q_sample_kernel(cq_ref, cos_ref, sinp_ref, sinm_ref, wq_ref, wukt_ref, sel_ref, gq_ref, gk_ref,
                     qa_ref, qr_ref):
    cq = cq_ref[...]
    for h in range(N_HEADS):
        q = _rope(jnp.dot(cq, wq_ref[h], preferred_element_type=F32), cos_ref[...], sinp_ref[...], sinm_ref[...])
        q = _rms_qk(q, gq_ref[...]) * (LOG2E / math.sqrt(QK))
        qk = (q * gk_ref[...]).astype(BF16)
        qa_ref[h] = jnp.dot(qk, wukt_ref[h], preferred_element_type=F32).astype(BF16)
        qr_ref[h] = jnp.dot(qk, sel_ref[...], preferred_element_type=F32).astype(BF16)


def _q_sample(cq, tables, wq, wukt, sel, gq, gk):
    n = cq.shape[0]
    return pl.pallas_call(
        _q_sample_kernel,
        out_shape=[jax.ShapeDtypeStruct((N_HEADS, n, LANES), BF16), jax.ShapeDtypeStruct((N_HEADS, n, ROPE), BF16)],
        compiler_params=pltpu.CompilerParams(vmem_limit_bytes=VMEM_LIMIT), name="q_sample",
    )(cq, *tables, wq, wukt, sel, gq, gk)


def _attn_sample_kernel(pt_ref, qa_ref, qr_ref, qa_prev_ref, qr_prev_ref, cnew_ref, krnew_ref, wukt_ref, wv_ref,
                        ckv_hbm, kr_hbm, o_ref, cbuf, kbuf, sems, s_scr, cb_scr, m_scr, l_scr, acc_scr,
                        *, pc, nc, total, t_dec):
    g = pl.program_id(0)
    slot = g % 2
    rows = N_HEADS * t_dec

    def page_copies(step, slot_):
        base = step * pc
        cps = []
        for j in range(pc):
            pg = pt_ref[base + j]
            cps.append(pltpu.make_async_copy(ckv_hbm.at[pg], cbuf.at[slot_, j], sems.at[0, slot_]))
            cps.append(pltpu.make_async_copy(kr_hbm.at[pg], kbuf.at[slot_, j], sems.at[1, slot_]))
        return cps

    def wait_pages(slot_):
        pltpu.make_async_copy(ckv_hbm.at[pl.ds(0, pc)], cbuf.at[slot_], sems.at[0, slot_]).wait()
        pltpu.make_async_copy(kr_hbm.at[pl.ds(0, pc)], kbuf.at[slot_], sems.at[1, slot_]).wait()

    def reset_softmax():
        m_scr[...] = jnp.full(m_scr.shape, -jnp.inf, F32)
        l_scr[...] = jnp.zeros(l_scr.shape, F32)
        acc_scr[...] = jnp.zeros(acc_scr.shape, F32)

    @pl.when(g == 0)
    def _():
        for cp in page_copies(g, slot):
            cp.start()
        s_scr[1] = jnp.zeros(s_scr.shape[1:], F32)
        cb_scr[1] = jnp.zeros(cb_scr.shape[1:], BF16)
        reset_softmax()

    @pl.when((g >= 1) & ((g - 1) % nc == 0))
    def _():
        reset_softmax()

    def scores(qa, qr, cb, krt):
        nk = cb.shape[0]
        knt = lax.dot_general(wukt_ref[...], cb, NT_DIMS, preferred_element_type=F32)
        n2 = jnp.sum((knt * knt).reshape(N_HEADS, NOPE, nk), axis=1)
        kr2 = jnp.sum(krt * krt, axis=0, keepdims=True)
        rinv = lax.rsqrt((n2 + kr2) * (1.0 / QK) + EPS)
        s = lax.dot_general(qa, cb, NT_DIMS, preferred_element_type=F32)
        s += jnp.dot(qr, krt.astype(BF16), preferred_element_type=F32)
        return (s.reshape(N_HEADS, t_dec, nk) * rinv[:, None, :]).reshape(rows, nk)

    def fold(s, cb):
        m = m_scr[...]
        m_new = jnp.maximum(m, jnp.max(s, axis=-1, keepdims=True))
        p = jnp.exp2(s - m_new)
        corr = jnp.exp2(m - m_new)
        l_scr[...] = l_scr[...] * corr + jnp.sum(p, axis=-1, keepdims=True)
        acc_scr[...] = acc_scr[...] * corr + jnp.dot(p.astype(BF16), cb, preferred_element_type=F32)
        m_scr[...] = m_new

    def score_and_fold(cur, prev):
        wait_pages(cur)
        for cp in page_copies(jnp.minimum(g + 1, total - 1), prev):
            cp.start()
        s_prev, cb_prev = s_scr[prev], cb_scr[prev]
        cb = cbuf[cur].reshape(pc * PAGE, cbuf.shape[-1]).astype(BF16)
        krt = jnp.concatenate([kbuf[cur, j] for j in range(pc)], axis=1)
        s_scr[cur] = scores(qa_ref[...].reshape(rows, LANES), qr_ref[...].reshape(rows, ROPE), cb, krt)
        cb_scr[cur] = cb
        fold(s_prev, cb_prev)

    for parity in range(2):
        pl.when(slot == parity)(functools.partial(score_and_fold, parity, 1 - parity))

    @pl.when(g == total)
    def _():
        wait_pages(1 - slot)

    @pl.when((g >= 1) & ((g - 1) % nc == nc - 1))
    def _():
        key = lax.broadcasted_iota(jnp.int32, (rows, PAGE), 1)
        step = lax.broadcasted_iota(jnp.int32, (rows, PAGE), 0) % t_dec
        cb_new = cnew_ref[0].astype(BF16)
        s_new = scores(qa_prev_ref[...].reshape(rows, LANES), qr_prev_ref[...].reshape(rows, ROPE), cb_new, krnew_ref[0])
        fold(jnp.where(key <= step, s_new, -jnp.inf), cb_new)
        o_lat = (acc_scr[...] / l_scr[...]).astype(BF16)
        lane = lax.broadcasted_iota(jnp.int32, (t_dec, LANES), 1)
        for hp in range(N_HEADS // 2):
            lo = jnp.dot(o_lat[(2 * hp) * t_dec:(2 * hp + 1) * t_dec], wv_ref[hp], preferred_element_type=F32)
            hi = jnp.dot(o_lat[(2 * hp + 1) * t_dec:(2 * hp + 2) * t_dec], wv_ref[hp], preferred_element_type=F32)
            o_ref[:, hp * LANES:(hp + 1) * LANES] = jnp.where(lane < V_DIM, lo, hi)


def _attn_sample(page_table, qa, qr, cnew_pad, krnew_pad, wukt_all, wvp, cache_ckv, cache_kr, t_dec, pc):
    nb, npg = page_table.shape
    nc = npg // pc
    total = nb * nc
    kv_rank = cache_ckv.shape[-1]
    nkeys = pc * PAGE
    rows = N_HEADS * t_dec
    seq_cur = lambda g: jnp.minimum(g // nc, nb - 1)
    seq_prev = lambda g: jnp.maximum(g - 1, 0) // nc
    grid_spec = pltpu.PrefetchScalarGridSpec(
        num_scalar_prefetch=1,
        grid=(total + 1,),
        in_specs=[pl.BlockSpec((N_HEADS, t_dec, LANES), lambda g, pt: (0, seq_cur(g), 0)),
                  pl.BlockSpec((N_HEADS, t_dec, ROPE), lambda g, pt: (0, seq_cur(g), 0)),
                  pl.BlockSpec((N_HEADS, t_dec, LANES), lambda g, pt: (0, seq_prev(g), 0)),
                  pl.BlockSpec((N_HEADS, t_dec, ROPE), lambda g, pt: (0, seq_prev(g), 0)),
                  pl.BlockSpec((1, PAGE, kv_rank), lambda g, pt: (seq_prev(g), 0, 0)),
                  pl.BlockSpec((1, ROPE, PAGE), lambda g, pt: (seq_prev(g), 0, 0)),
                  pl.BlockSpec(wukt_all.shape, lambda g, pt: (0, 0)),
                  pl.BlockSpec(wvp.shape, lambda g, pt: (0, 0, 0)),
                  pl.BlockSpec(memory_space=pl.ANY), pl.BlockSpec(memory_space=pl.ANY)],
        out_specs=pl.BlockSpec((t_dec, N_HEADS * V_DIM), lambda g, pt: (seq_prev(g), 0)),
        scratch_shapes=[pltpu.VMEM((2, pc, PAGE, kv_rank), F32), pltpu.VMEM((2, pc, ROPE, PAGE), F32),
                        pltpu.SemaphoreType.DMA((2, 2)),
                        pltpu.VMEM((2, rows, nkeys), F32), pltpu.VMEM((2, nkeys, kv_rank), BF16),
                        pltpu.VMEM((rows, 1), F32), pltpu.VMEM((rows, 1), F32), pltpu.VMEM((rows, kv_rank), F32)],
    )
    return pl.pallas_call(
        functools.partial(_attn_sample_kernel, pc=pc, nc=nc, total=total, t_dec=t_dec),
        grid_spec=grid_spec,
        out_shape=jax.ShapeDtypeStruct((nb * t_dec, N_HEADS * V_DIM), F32),
        compiler_params=_cparams(("arbitrary",)), name="attn_sample",
    )(page_table.reshape(-1), qa, qr, qa, qr, cnew_pad, krnew_pad, wukt_all, wvp, cache_ckv, cache_kr)


def _merge_kernel(xp_ref, xs_ref, yp_ref, ys_ref, ap_ref, as_ref, wglu_ref, bglu_ref, gs_ref, ga_ref,
                  wo1_ref, wo2_ref, g2_ref, wrh_ref, wrl_ref, br_ref, tri_ref,
                  h_ref, xn_ref, meta_ref, cnt_ref, run_scr, *, n_prompt_tiles):
    i = pl.program_id(0)

    @pl.when(i == 0)
    def _():
        run_scr[...] = jnp.zeros(run_scr.shape, F32)

    is_p = i < n_prompt_tiles
    x = jnp.where(is_p, xp_ref[...], xs_ref[...])
    slabs = lambda ref: jnp.concatenate([ref[j] for j in range(ref.shape[0])], axis=1)
    y = jnp.where(is_p, slabs(yp_ref), slabs(ys_ref))
    att = jnp.where(is_p, ap_ref[...], as_ref[...])

    y = jax.nn.gelu(y)
    glu = jnp.dot(y.astype(BF16), wglu_ref[...], preferred_element_type=F32) + bglu_ref[...]
    ssm = y * jax.nn.sigmoid(glu)
    mix = jnp.dot(_rms(ssm, gs_ref[...]).astype(BF16), wo1_ref[...], preferred_element_type=F32)
    mix += jnp.dot(_rms(att, ga_ref[...]).astype(BF16), wo2_ref[...], preferred_element_type=F32)
    h = x + mix
    h_ref[...] = h
    xn = _rms(h, g2_ref[...])
    xn_ref[...] = xn

    tm = x.shape[0]
    xh = xn.astype(BF16)
    xl = (xn - xh.astype(F32)).astype(BF16)
    logits = (jnp.dot(xh, wrh_ref[...], preferred_element_type=F32) + jnp.dot(xl, wrh_ref[...], preferred_element_type=F32)
              + jnp.dot(xh, wrl_ref[...], preferred_element_type=F32)) + br_ref[...]
    lane_i = lax.broadcasted_iota(jnp.int32, (tm, LANES), 1)
    lane = lane_i.astype(F32)
    big = float(LANES)
    first = lambda hit: jnp.min(jnp.where(hit, lane, big), axis=-1, keepdims=True)
    gl = jnp.where(lane_i < N_GROUPS, logits, -jnp.inf)
    gmax = jnp.max(gl, axis=-1, keepdims=True)
    grp = first(gl == gmax)
    p_sel = 1.0 / jnp.sum(jnp.exp(gl - gmax), axis=-1, keepdims=True)
    lane_grp = ((lane_i - N_GROUPS) >> 3).astype(F32)
    in_grp = (lane_i >= N_GROUPS) & (lane_i < N_GROUPS + N_EXPERTS) & (lane_grp == grp)
    el = jnp.where(in_grp, logits, -jnp.inf)
    m1 = jnp.max(el, axis=-1, keepdims=True)
    i1 = first(el == m1)
    el2 = jnp.where(lane == i1, -jnp.inf, el)
    m2 = jnp.max(el2, axis=-1, keepdims=True)
    i2 = first(el2 == m2)
    e21 = jnp.exp(m2 - m1)
    w1 = p_sel / (1.0 + e21)
    w2 = p_sel * e21 / (1.0 + e21)
    e1, e2 = i1 - N_GROUPS, i2 - N_GROUPS

    oh1, oh2 = lane == e1, lane == e2
    oh = jnp.where(oh1 | oh2, 1.0, 0.0)
    before = jnp.dot(tri_ref[...], oh.astype(BF16), preferred_element_type=F32) + run_scr[...]
    r1 = jnp.sum(jnp.where(oh1, before, 0.0), axis=-1, keepdims=True)
    r2 = jnp.sum(jnp.where(oh2, before, 0.0), axis=-1, keepdims=True)
    run_scr[...] += jnp.sum(oh, axis=0, keepdims=True)
    cnt_ref[...] = run_scr[...]
    meta = jnp.zeros((tm, LANES), F32)
    for k, v in enumerate((e1, e2, w1, w2, r1, r2)):
        meta = jnp.where(lane_i == k, v, meta)
    meta_ref[...] = meta


def _merge(x_p, x_s, y_p, y_s, a_p, a_s, wglu, bglu, gs, ga, wo1, wo2, g2, wr, br, tm):
    wrh = wr.astype(BF16)
    wrl = (wr - wrh.astype(F32)).astype(BF16)
    n_p, d = x_p.shape
    n_s = x_s.shape[0]
    npt, nst = n_p // tm, n_s // tm
    n_all = n_p + n_s
    tri = (lax.broadcasted_iota(jnp.int32, (tm, tm), 0) > lax.broadcasted_iota(jnp.int32, (tm, tm), 1)).astype(BF16)
    prow = lambda i: (jnp.minimum(i, npt - 1), 0)
    srow = lambda i: (jnp.maximum(i - npt, 0), 0)
    row = lambda i: (i, 0)
    const = lambda i: (0, 0)
    w = a_p.shape[1]
    nsl = y_p.shape[0]
    full = lambda a: pl.BlockSpec(a.shape, const)
    return pl.pallas_call(
        functools.partial(_merge_kernel, n_prompt_tiles=npt),
        grid=(npt + nst,),
        in_specs=[pl.BlockSpec((tm, d), prow), pl.BlockSpec((tm, d), srow),
                  pl.BlockSpec((nsl, tm, LANES), lambda i: (0, jnp.minimum(i, npt - 1), 0)),
                  pl.BlockSpec((nsl, tm, LANES), lambda i: (0, jnp.maximum(i - npt, 0), 0)),
                  pl.BlockSpec((tm, w), prow), pl.BlockSpec((tm, w), srow),
                  full(wglu), full(bglu), full(gs), full(ga), full(wo1), full(wo2), full(g2), full(wrh), full(wrl), full(br),
                  full(tri)],
        out_specs=[pl.BlockSpec((tm, d), row), pl.BlockSpec((tm, d), row), pl.BlockSpec((tm, LANES), row),
                   pl.BlockSpec((1, LANES), const)],
        out_shape=[jax.ShapeDtypeStruct((n_all, d), F32), jax.ShapeDtypeStruct((n_all, d), F32),
                   jax.ShapeDtypeStruct((n_all, LANES), F32), jax.ShapeDtypeStruct((1, LANES), F32)],
        scratch_shapes=[pltpu.VMEM((1, LANES), F32)],
        compiler_params=_cparams(("arbitrary",)), name="merge",
    )(x_p, x_s, y_p, y_s, a_p, a_s, wglu, bglu, gs, ga, wo1, wo2, g2, wrh, wrl, br, tri)


def _start_row_gather(idx_ref, base, src_hbm, dst, sem, nrows):
    for r in range(nrows):
        pltpu.make_async_copy(src_hbm.at[pl.ds(idx_ref[base + r], 1)], dst.at[pl.ds(r, 1)], sem).start()


def _wait_row_gather(src_hbm, dst, sem, nrows):
    pltpu.make_async_copy(src_hbm.at[pl.ds(0, nrows)], dst, sem).wait()


def _pos_kernel(meta_ref, offs_ref, p1_ref, p2_ref):
    meta = meta_ref[...]
    tm = meta.shape[0]
    lane_i = lax.broadcasted_iota(jnp.int32, (tm, LANES), 1)
    lane = lane_i.astype(F32)
    diag = lax.broadcasted_iota(jnp.int32, (tm, LANES), 0) % LANES == lane_i

    def dense(e, r):
        pos = jnp.sum(jnp.where(lane == e, offs_ref[...], 0.0), axis=-1, keepdims=True) + r
        spread = jnp.where(diag, pos, 0.0).reshape(tm // LANES, LANES, LANES)
        return jnp.sum(spread, axis=1).astype(jnp.int32)

    p1_ref[...] = dense(meta[:, 0:1], meta[:, 4:5])
    p2_ref[...] = dense(meta[:, 1:2], meta[:, 5:6])


def _positions(meta, offs_vec, tm):
    n = meta.shape[0]
    rows = tm // LANES
    return pl.pallas_call(
        _pos_kernel,
        grid=(n // tm,),
        in_specs=[pl.BlockSpec((tm, LANES), lambda i: (i, 0)), pl.BlockSpec((1, LANES), lambda i: (0, 0))],
        out_specs=[pl.BlockSpec((rows, LANES), lambda i: (i, 0)), pl.BlockSpec((rows, LANES), lambda i: (i, 0))],
        out_shape=[jax.ShapeDtypeStruct((n // LANES, LANES), jnp.int32)] * 2,
        compiler_params=_cparams(("parallel",)), name="positions",
    )(meta, offs_vec)


def _dispatch_kernel(p1_ref, p2_ref, tend_ref, xn_ref, xs_hbm, zbuf, sem, *, tmd, tme):
    base = pl.program_id(0) * tmd

    @pl.when(pl.program_id(0) == 0)
    def _():
        zbuf[...] = jnp.zeros(zbuf.shape, zbuf.dtype)

        def last_tile(e, carry, wait):
            end = tend_ref[e]
            begin = jnp.where(e == 0, 0, tend_ref[jnp.maximum(e - 1, 0)])

            @pl.when(end > begin)
            def _():
                cp = pltpu.make_async_copy(zbuf, xs_hbm.at[pl.ds((end - 1) * tme, tme)], sem)
                cp.wait() if wait else cp.start()
            return carry

        def spare_tile(t, carry, wait):
            cp = pltpu.make_async_copy(zbuf, xs_hbm.at[pl.ds(t * tme, tme)], sem)
            cp.wait() if wait else cp.start()
            return carry

        n_used, n_tiles = tend_ref[N_EXPERTS - 1], xs_hbm.shape[0] // tme
        for wait in (False, True):
            lax.fori_loop(0, N_EXPERTS, functools.partial(last_tile, wait=wait), 0)
            lax.fori_loop(n_used, n_tiles, functools.partial(spare_tile, wait=wait), 0)

    for r in range(tmd):
        src = xn_ref.at[pl.ds(r, 1)]
        pltpu.make_async_copy(src, xs_hbm.at[pl.ds(p1_ref[base + r], 1)], sem).start()
        pltpu.make_async_copy(src, xs_hbm.at[pl.ds(p2_ref[base + r], 1)], sem).start()
    for _ in range(2):
        pltpu.make_async_copy(xn_ref, xs_hbm.at[pl.ds(0, tmd)], sem).wait()


def _dispatch(pos1, pos2, tile_end, xn, n_rows, tmd, tme):
    n, d = xn.shape
    grid_spec = pltpu.PrefetchScalarGridSpec(
        num_scalar_prefetch=3,
        grid=(n // tmd,),
        in_specs=[pl.BlockSpec((tmd, d), lambda i, p1, p2, te: (i, 0))],
        out_specs=pl.BlockSpec(memory_space=pl.ANY),
        scratch_shapes=[pltpu.VMEM((tme, d), xn.dtype), pltpu.SemaphoreType.DMA],
    )
    return pl.pallas_call(
        functools.partial(_dispatch_kernel, tmd=tmd, tme=tme),
        grid_spec=grid_spec,
        out_shape=jax.ShapeDtypeStruct((n_rows, d), xn.dtype),
        compiler_params=_cparams(("arbitrary",)), name="dispatch",
    )(pos1, pos2, tile_end, xn)


def _experts_kernel(te_ref, nu_ref, x_ref, wg_ref, wu_ref, wd_ref, o_ref, wg_scr, wu_scr, wd_scr):
    i = pl.program_id(0)

    @pl.when((i == 0) | (te_ref[i] != te_ref[jnp.maximum(i - 1, 0)]))
    def _():
        wg_scr[...] = wg_ref[0].astype(BF16)
        wu_scr[...] = wu_ref[0].astype(BF16)
        wd_scr[...] = wd_ref[0].astype(BF16)

    @pl.when(i < nu_ref[0])
    def _():
        x = x_ref[...].astype(BF16)
        hg = jnp.dot(x, wg_scr[...], preferred_element_type=F32)
        hu = jnp.dot(x, wu_scr[...], preferred_element_type=F32)
        hh = (jax.nn.silu(hg) * hu).astype(BF16)
        o_ref[...] = jnp.dot(hh, wd_scr[...], preferred_element_type=F32)

    @pl.when(i >= nu_ref[0])
    def _():
        o_ref[...] = jnp.zeros(o_ref.shape, F32)


def _experts(tile_expert, n_used, xs, w_gate, w_up, w_down, tme):
    nt = tile_expert.shape[0]
    ne, d, de = w_gate.shape
    wmap = lambda i, te, nu: (te[i], 0, 0)
    grid_spec = pltpu.PrefetchScalarGridSpec(
        num_scalar_prefetch=2,
        grid=(nt,),
        in_specs=[pl.BlockSpec((tme, d), lambda i, te, nu: (jnp.minimum(i, nu[0] - 1), 0)),
                  pl.BlockSpec((1, d, de), wmap), pl.BlockSpec((1, d, de), wmap), pl.BlockSpec((1, de, d), wmap)],
        out_specs=pl.BlockSpec((tme, d), lambda i, te, nu: (i, 0)),
        scratch_shapes=[pltpu.VMEM((d, de), BF16), pltpu.VMEM((d, de), BF16), pltpu.VMEM((de, d), BF16)],
    )
    return pl.pallas_call(
        _experts_kernel,
        grid_spec=grid_spec,
        out_shape=jax.ShapeDtypeStruct((nt * tme, d), F32),
        compiler_params=_cparams(("arbitrary",)), name="experts",
    )(tile_expert, n_used, xs, w_gate, w_up, w_down)


def _combine_kernel(p1_ref, p2_ref, h_ref, meta_ref, eo_hbm, y_ref, buf, sems, *, tmc, nsteps):
    i = pl.program_id(0)
    slot = i % 2

    def start(step, slot_):
        _start_row_gather(p1_ref, step * tmc, eo_hbm, buf.at[slot_, pl.ds(0, tmc)], sems.at[slot_], tmc)
        _start_row_gather(p2_ref, step * tmc, eo_hbm, buf.at[slot_, pl.ds(tmc, tmc)], sems.at[slot_], tmc)

    @pl.when(i == 0)
    def _():
        start(0, 0)

    @pl.when(i + 1 < nsteps)
    def _():
        start(i + 1, 1 - slot)

    _wait_row_gather(eo_hbm, buf.at[slot], sems.at[slot], 2 * tmc)
    meta = meta_ref[...]
    y_ref[...] = h_ref[...] + meta[:, 2:3] * buf[slot, :tmc, :] + meta[:, 3:4] * buf[slot, tmc:, :]


def _combine(pos1, pos2, h_all, meta_all, eo, row0, tmc):
    n = pos1.shape[0]
    d = h_all.shape[1]
    nsteps = n // tmc
    off = row0 // tmc
    grid_spec = pltpu.PrefetchScalarGridSpec(
        num_scalar_prefetch=2,
        grid=(nsteps,),
        in_specs=[pl.BlockSpec((tmc, d), lambda i, p1, p2: (i + off, 0)),
                  pl.BlockSpec((tmc, LANES), lambda i, p1, p2: (i + off, 0)),
                  pl.BlockSpec(memory_space=pl.ANY)],
        out_specs=pl.BlockSpec((tmc, d), lambda i, p1, p2: (i, 0)),
        scratch_shapes=[pltpu.VMEM((2, 2 * tmc, d), F32), pltpu.SemaphoreType.DMA((2,))],
    )
    return pl.pallas_call(
        functools.partial(_combine_kernel, tmc=tmc, nsteps=nsteps),
        grid_spec=grid_spec,
        out_shape=jax.ShapeDtypeStruct((n, d), F32),
        compiler_params=_cparams(("arbitrary",)), name="combine",
    )(pos1, pos2, h_all, meta_all, eo)


def _pick_tile(n, pref):
    t = min(pref, n)
    while n % t:
        t //= 2
    return t


def _layer(x_p, x_s, cache_ckv, cache_kr, st_re, st_im, page_table, p):
    nb_p, t_p, d = x_p.shape
    nb_s, t_s, _ = x_s.shape
    n_p, n_s = nb_p * t_p, nb_s * t_s
    past_len = page_table.shape[1] * PAGE
    g = p['ssm_a_re'].shape[0]
    ssm_w = g * SSM_CH
    q_rank = p['w_uq'].shape[0]
    kv_rank = p['w_uk'].shape[0]

    w_in = p['w_in']
    o3 = ssm_w + q_rank + kv_rank
    w_pad = jnp.zeros((d, o3 + LANES), F32).at[:, :o3].set(w_in[:, :o3])
    w_pad = w_pad.at[:, o3 + NOPE:o3 + QK].set(w_in[:, o3:]).astype(BF16)
    row2 = lambda v: v.reshape(1, -1).astype(F32)
    padq = lambda v: jnp.pad(v, (0, LANES - QK)).reshape(1, LANES)
    wq = jnp.pad(p['w_uq'].transpose(1, 0, 2), ((0, 0), (0, 0), (0, LANES - QK))).astype(BF16)
    wk = jnp.pad(p['w_uk'].transpose(1, 0, 2), ((0, 0), (0, 0), (0, LANES - NOPE))).astype(BF16)
    wvp = p['w_uv'].reshape(kv_rank, N_HEADS // 2, 2 * V_DIM).transpose(1, 0, 2).astype(BF16)
    wv_h = p['w_uv'].transpose(1, 0, 2)
    wv8 = jnp.stack([jnp.pad(wv_h[h], ((0, 0), (V_DIM - _ones_lane(h), _ones_lane(h)))) for h in range(N_HEADS)])
    wv8 = wv8.astype(BF16)
    wukt_all = p['w_uk'].reshape(kv_rank, N_HEADS * NOPE).T.astype(BF16)
    wukt = jnp.pad(p['w_uk'].transpose(1, 2, 0), ((0, 0), (0, LANES - NOPE), (0, 0))).astype(BF16)
    sel = (jnp.arange(LANES)[:, None] == NOPE + jnp.arange(ROPE)[None, :]).astype(BF16)
    gq, gk = padq(p['qk_norm_q']), padq(p['qk_norm_k'])
    ssm_params = (p['ssm_a_re'], p['ssm_a_im'], p['ssm_log_dt'], p['ssm_b_re'], p['ssm_b_im'],
                  p['ssm_c_re'], p['ssm_c_im'], p['ssm_d'])

    tm_p = _pick_tile(t_p, 512)
    tab_p = _rope_tables(jnp.arange(t_p))
    u, cq, ckv_p, krp_p = _in_proj(x_p.reshape(n_p, d), row2(p['norm1_g']), w_pad, row2(p['q_norm_g']),
                                   row2(p['kv_norm_g']), tab_p, tm_p, ssm_w, q_rank, kv_rank)
    lc_p = _pick_tile(t_p, 8)
    nsl = ssm_w // LANES
    s5w_p = _s5_weights(*ssm_params, lc_p)
    y_ssm_p, hf_p = _s5(u, jnp.zeros((nsl, nb_p, 2 * GPB * SSM_P), F32), s5w_p, nb_p, t_p, lc_p)
    tq = _pick_tile(t_p, 512)
    att_p = _attn_prompt(cq, ckv_p, krp_p, tab_p, wq, wk, wv8, gq, gk, nb_p, t_p, tq, tq)

    tm_s = _pick_tile(n_s, 512)
    pos_s = past_len + jnp.arange(t_s)
    tab_s = tuple(jnp.tile(a, (tm_s // t_s, 1)) for a in _rope_tables(pos_s))
    u, cq, ckv_s, krp_s = _in_proj(x_s.reshape(n_s, d), row2(p['norm1_g']), w_pad, row2(p['q_norm_g']),
                                   row2(p['kv_norm_g']), tab_s, tm_s, ssm_w, q_rank, kv_rank)
    slab_state = lambda s: s.astype(F32).reshape(nb_s, nsl, GPB * SSM_P).transpose(1, 0, 2)
    h0 = jnp.concatenate([slab_state(st_re), slab_state(st_im)], axis=-1)
    s5w_s = s5w_p if t_s == lc_p else _s5_weights(*ssm_params, t_s)
    y_ssm_s, hf_s = _s5(u, h0, s5w_s, nb_s, t_s, t_s)
    tab_q = tuple(jnp.tile(a, (n_s // tm_s, 1)) for a in tab_s)
    qa, qr = _q_sample(cq, tab_q, wq, wukt, sel, gq, gk)
    kr_s = krp_s[:, NOPE:QK]
    cnew = jnp.pad(ckv_s.reshape(nb_s, t_s, kv_rank), ((0, 0), (0, PAGE - t_s), (0, 0)))
    krnew = jnp.pad(kr_s.reshape(nb_s, t_s, ROPE), ((0, 0), (0, PAGE - t_s), (0, 0))).swapaxes(1, 2)
    pc = _pick_tile(page_table.shape[1], 32)
    att_s = _attn_sample(page_table, qa, qr, cnew, krnew, wukt_all, wvp, cache_ckv, jnp.swapaxes(cache_kr, 1, 2),
                         t_s, pc)

    tm = _pick_tile(math.gcd(n_p, n_s), 512)
    w_out = p['w_out'].astype(BF16)
    wr = jnp.zeros((d, LANES), F32).at[:, :N_GROUPS].set(p['w_router_group'])
    wr = wr.at[:, N_GROUPS:N_GROUPS + N_EXPERTS].set(p['w_router_expert'])
    br = jnp.zeros((1, LANES), F32).at[0, :N_GROUPS].set(p['b_router_group'])
    br = br.at[0, N_GROUPS:N_GROUPS + N_EXPERTS].set(p['b_router_expert'])
    h_all, xn_all, meta, cnt = _merge(
        x_p.reshape(n_p, d), x_s.reshape(n_s, d), y_ssm_p, y_ssm_s, att_p, att_s,
        p['ssm_w_glu'].astype(BF16), row2(p['ssm_b_glu']), row2(p['out_norm_ssm']), row2(p['out_norm_attn']),
        w_out[:ssm_w], w_out[ssm_w:], row2(p['norm2_g']), wr, br, tm)

    n_all = n_p + n_s
    tme = 512
    nt = (2 * n_all) // tme + N_EXPERTS
    counts = cnt[0, :N_EXPERTS].astype(jnp.int32)
    tiles_per = (counts + tme - 1) // tme
    tile_end = jnp.cumsum(tiles_per)
    tile_start = tile_end - tiles_per
    tile_id = jnp.arange(nt, dtype=jnp.int32)
    tile_expert = jnp.minimum(jnp.sum(tile_end[None, :] <= tile_id[:, None], axis=1), N_EXPERTS - 1).astype(jnp.int32)
    n_used = tile_end[-1:].astype(jnp.int32)
    offs_vec = jnp.zeros((1, LANES), F32).at[0, :N_EXPERTS].set((tile_start * tme).astype(F32))
    tmd = _pick_tile(n_all, 1024)
    pos1, pos2 = (a.reshape(-1) for a in _positions(meta, offs_vec, tmd))
    xs = _dispatch(pos1, pos2, tile_end.astype(jnp.int32), xn_all, nt * tme, tmd, tme)
    eo = _experts(tile_expert, n_used, xs, p['w_gate'], p['w_up'], p['w_down'], tme)

    tmc = _pick_tile(math.gcd(n_p, n_s), 256)
    y_p = _combine(pos1[:n_p], pos2[:n_p], h_all, meta, eo, 0, tmc)
    y_s = _combine(pos1[n_p:], pos2[n_p:], h_all, meta, eo, n_p, tmc)

    n_pg = t_p // PAGE

    def split(hf):
        nb = hf.shape[1]
        half = GPB * SSM_P
        unslab = lambda a: a.reshape(nsl, nb, GPB, SSM_P).transpose(1, 0, 2, 3).reshape(nb, g, SSM_P)
        return unslab(hf[..., :half]), unslab(hf[..., half:])

    hp_re, hp_im = split(hf_p)
    hs_re, hs_im = split(hf_s)
    return (y_p.reshape(nb_p, t_p, d), y_s.reshape(nb_s, t_s, d),
            ckv_p.reshape(nb_p, n_pg, PAGE, kv_rank), krp_p[:, NOPE:QK].reshape(nb_p, n_pg, PAGE, ROPE),
            hp_re, hp_im, ckv_s.reshape(nb_s, t_s, kv_rank), kr_s.reshape(nb_s, t_s, ROPE), hs_re, hs_im)


_PARAM_NAMES = ('norm1_g', 'w_in', 'ssm_a_re', 'ssm_a_im', 'ssm_log_dt', 'ssm_b_re', 'ssm_b_im', 'ssm_c_re',
                'ssm_c_im', 'ssm_d', 'ssm_w_glu', 'ssm_b_glu', 'q_norm_g', 'w_uq', 'kv_norm_g', 'w_uk', 'w_uv',
                'qk_norm_q', 'qk_norm_k', 'out_norm_ssm', 'out_norm_attn', 'w_out', 'norm2_g', 'w_router_group',
                'b_router_group', 'w_router_expert', 'b_router_expert', 'w_gate', 'w_up', 'w_down')


def kernel(x_prompt, x_sample, cache_ckv, cache_krope, state_ssm_re, state_ssm_im, page_table, norm1_g, w_in, ssm_a_re, ssm_a_im, ssm_log_dt, ssm_b_re, ssm_b_im, ssm_c_re, ssm_c_im, ssm_d, ssm_w_glu, ssm_b_glu, q_norm_g, w_uq, kv_norm_g, w_uk, w_uv, qk_norm_q, qk_norm_k, out_norm_ssm, out_norm_attn, w_out, norm2_g, w_router_group, b_router_group, w_router_expert, b_router_expert, w_gate, w_up, w_down):
    params = (norm1_g, w_in, ssm_a_re, ssm_a_im, ssm_log_dt, ssm_b_re, ssm_b_im, ssm_c_re, ssm_c_im, ssm_d,
              ssm_w_glu, ssm_b_glu, q_norm_g, w_uq, kv_norm_g, w_uk, w_uv, qk_norm_q, qk_norm_k, out_norm_ssm,
              out_norm_attn, w_out, norm2_g, w_router_group, b_router_group, w_router_expert, b_router_expert,
              w_gate, w_up, w_down)
    depth = w_in.shape[0]
    h_p, h_s = x_prompt, x_sample
    outs = [[] for _ in range(8)]
    for layer in range(depth):
        p = {k: v[layer] for k, v in zip(_PARAM_NAMES, params)}
        res = _layer(h_p, h_s, cache_ckv[layer], cache_krope[layer], state_ssm_re[layer], state_ssm_im[layer],
                     page_table, p)
        h_p, h_s = res[0], res[1]
        for acc, r in zip(outs, res[2:]):
            acc.append(r)
    return (h_p, h_s) + tuple(jnp.stack(o) for o in outs)
```

```python
import functools
import math

import jax
import jax.numpy as jnp
from jax import lax
from jax.experimental import pallas as pl
from jax.experimental.pallas import tpu as pltpu

F32 = jnp.float32
BF16 = jnp.bfloat16
HIGHEST = lax.Precision.HIGHEST

LANES = 128
SUBLANES = 8
VMEM_LIMIT = 56 * 1024 * 1024

EPS = 1e-6
ROPE_BASE = 10000.0
PAGE = 128
SSM_CH = 16
SSM_P = 64
GPB = LANES // SSM_CH
S5_MAX_TOKENS = 8192
N_HEADS = 8
NOPE = 64
ROPE = 32
QK = NOPE + ROPE
V_DIM = 64
N_GROUPS = 4
EXP_PER_GROUP = 8
N_EXPERTS = N_GROUPS * EXP_PER_GROUP
ATTN_HEADS_PER_BODY = 8

NT_DIMS = (((1,), (1,)), ((), ()))
LOG2E = math.log2(math.e)


def _cparams(sem, vmem=VMEM_LIMIT):
    return pltpu.CompilerParams(dimension_semantics=sem, vmem_limit_bytes=vmem)


def _rms(x, g):
    return x * lax.rsqrt(jnp.mean(x * x, axis=-1, keepdims=True) + EPS) * g


def _rms_qk(x, g):
    return x * lax.rsqrt(jnp.sum(x * x, axis=-1, keepdims=True) * (1.0 / QK) + EPS) * g


def _rope(x, cos, sinp, sinm):
    return x * cos + pltpu.roll(x, 16, 1) * sinp + pltpu.roll(x, LANES - 16, 1) * sinm


def _row_chunks(n, size=256):
    size = min(size, n)
    return [pl.ds(r, size) for r in range(0, n, size)]


def _rope_tables(pos):
    inv_freq = ROPE_BASE ** (-jnp.arange(0, ROPE, 2, dtype=F32) / ROPE)
    ang = pos.astype(F32)[:, None] * inv_freq[None, :]
    c, s = jnp.cos(ang), jnp.sin(ang)
    t = pos.shape[0]
    z = lambda n: jnp.zeros((t, n), F32)
    cosf = jnp.concatenate([jnp.ones((t, NOPE), F32), c, c, z(LANES - QK)], axis=1)
    sinp = jnp.concatenate([z(NOPE + ROPE // 2), s, z(LANES - QK)], axis=1)
    sinm = jnp.concatenate([z(NOPE), -s, z(LANES - QK + ROPE // 2)], axis=1)
    return cosf, sinp, sinm


def _in_proj_kernel(x_ref, g1_ref, w_ref, gq_ref, gkv_ref, cos_ref, sinp_ref, sinm_ref,
                    u_ref, cq_ref, ckv_ref, kr_ref, *, ssm_w, q_rank, kv_rank):
    o1, o2, o3 = ssm_w, ssm_w + q_rank, ssm_w + q_rank + kv_rank
    for rows in _row_chunks(x_ref.shape[0]):
        n = _rms(x_ref[rows, :], g1_ref[...])
        z = jnp.dot(n.astype(BF16), w_ref[...], preferred_element_type=F32)
        for j in range(ssm_w // LANES):
            u_ref[j, rows, :] = z[:, j * LANES:(j + 1) * LANES]
        cq_ref[rows, :] = _rms(z[:, o1:o2], gq_ref[...]).astype(cq_ref.dtype)
        ckv_ref[rows, :] = _rms(z[:, o2:o3], gkv_ref[...])
        kr_ref[rows, :] = _rope(z[:, o3:o3 + LANES], cos_ref[rows, :], sinp_ref[rows, :], sinm_ref[rows, :])


def _in_proj(x, g1, w_pad, gq, gkv, tables, tm, ssm_w, q_rank, kv_rank):
    n, d = x.shape
    period = tables[0].shape[0]
    nper = period // tm
    row = lambda i: (i, 0)
    const = lambda i: (0, 0)
    tab = pl.BlockSpec((tm, LANES), lambda i: (i % nper, 0))
    return pl.pallas_call(
        functools.partial(_in_proj_kernel, ssm_w=ssm_w, q_rank=q_rank, kv_rank=kv_rank),
        grid=(n // tm,),
        in_specs=[pl.BlockSpec((tm, d), row), pl.BlockSpec((1, d), const),
                  pl.BlockSpec(w_pad.shape, const), pl.BlockSpec((1, q_rank), const),
                  pl.BlockSpec((1, kv_rank), const), tab, tab, tab],
        out_specs=[pl.BlockSpec((ssm_w // LANES, tm, LANES), lambda i: (0, i, 0)), pl.BlockSpec((tm, q_rank), row),
                   pl.BlockSpec((tm, kv_rank), row), pl.BlockSpec((tm, LANES), row)],
        out_shape=[jax.ShapeDtypeStruct((ssm_w // LANES, n, LANES), F32), jax.ShapeDtypeStruct((n, q_rank), BF16),
                   jax.ShapeDtypeStruct((n, kv_rank), F32), jax.ShapeDtypeStruct((n, LANES), F32)],
        compiler_params=_cparams(("parallel",)), name="in_proj",
    )(x, g1, w_pad, gq, gkv, *tables)


def _s5_weights(a_re, a_im, log_dt, b_re, b_im, c_re, c_im, d, lc):
    g, p = a_re.shape
    dt = jnp.exp(log_dt)[:, None]
    den = a_re * a_re + a_im * a_im
    mag, ang = jnp.exp(a_re * dt), a_im * dt
    lb_re, lb_im = mag * jnp.cos(ang), mag * jnp.sin(ang)
    num_re = lb_re - 1.0
    coef_re = (num_re * a_re + lb_im * a_im) / den
    coef_im = (lb_im * a_re - num_re * a_im) / den
    bb_re = coef_re[..., None] * b_re - coef_im[..., None] * b_im
    bb_im = coef_re[..., None] * b_im + coef_im[..., None] * b_re

    def lam_pow(k):
        kk = k.astype(F32)[:, None, None]
        m = jnp.exp(a_re * dt * kk)
        return m * jnp.cos(ang * kk), m * jnp.sin(ang * kk)

    steps = jnp.arange(lc)
    pr, pi = lam_pow(lc - 1 - steps)
    bz_re = pr[..., None] * bb_re[None] - pi[..., None] * bb_im[None]
    bz_im = pr[..., None] * bb_im[None] + pi[..., None] * bb_re[None]
    bz = jnp.concatenate([bz_re, bz_im], axis=2)
    bz = bz.transpose(1, 0, 3, 2).reshape(g, lc * SSM_CH, 2 * p)
    qr, qi = lam_pow(steps + 1)
    cl_re = c_re[None] * qr[:, :, None, :] - c_im[None] * qi[:, :, None, :]
    cl_im = c_re[None] * qi[:, :, None, :] + c_im[None] * qr[:, :, None, :]
    cz = jnp.concatenate([cl_re, -cl_im], axis=3)
    cz = cz.transpose(1, 3, 0, 2).reshape(g, 2 * p, lc * SSM_CH)
    tr, ti = lam_pow(steps)
    cb_re = jnp.einsum('gcp,kgp,gpd->gkcd', c_re, tr, bb_re, precision=HIGHEST)
    cb_re -= jnp.einsum('gcp,kgp,gpd->gkcd', c_re, ti, bb_im, precision=HIGHEST)
    cb_re -= jnp.einsum('gcp,kgp,gpd->gkcd', c_im, tr, bb_im, precision=HIGHEST)
    cb_re -= jnp.einsum('gcp,kgp,gpd->gkcd', c_im, ti, bb_re, precision=HIGHEST)
    tau = steps[None, :] - steps[:, None]
    ksel = cb_re[:, jnp.clip(tau, 0, lc - 1)]
    ksel = jnp.where((tau >= 0)[None, :, :, None, None], ksel, 0.0)
    tz = ksel.transpose(0, 1, 4, 2, 3).reshape(g, lc * SSM_CH, lc * SSM_CH)
    assert lc * SSM_CH == LANES and 2 * p == LANES
    nsl = g // GPB
    sw = GPB * LANES
    col = jnp.arange(sw)
    rep_state = (jnp.arange(LANES)[:, None] == (col // (sw // 2)) * p + col % p).astype(BF16)
    rep_token = (jnp.arange(LANES)[:, None] == (col // LANES) * SSM_CH + col % SSM_CH).astype(BF16)
    grp_state = (col // p) % GPB
    grp_token = (col // SSM_CH) % GPB

    def slab(rows, rep, row_grp, col_grp):
        full = jnp.einsum('jrk,kc->jrc', rows.astype(BF16), rep, preferred_element_type=F32)
        return jnp.where(row_grp[:, None] == col_grp[None, :], full, 0.0).astype(BF16)

    bz_rows = bz.reshape(nsl, GPB, lc, SSM_CH, LANES).transpose(0, 2, 1, 3, 4).reshape(nsl, sw, LANES)
    tz_rows = tz.reshape(nsl, GPB, lc, SSM_CH, LANES).transpose(0, 2, 1, 3, 4).reshape(nsl, sw, LANES)
    cz_rows = cz.reshape(nsl, GPB, 2, p, LANES).transpose(0, 2, 1, 3, 4).reshape(nsl, sw, LANES)
    bz_s = slab(bz_rows, rep_state, grp_token, grp_state)
    wy = jnp.concatenate([slab(cz_rows, rep_token, grp_state, grp_token),
                          slab(tz_rows, rep_token, grp_token, grp_token)], axis=1)
    lr, li = lam_pow(jnp.array([lc]))
    lam_re = lr[0].reshape(nsl, 1, GPB * p)
    lam_im = li[0].reshape(nsl, 1, GPB * p)
    dvec = jnp.tile(d.reshape(nsl, 1, LANES), (1, 1, lc))
    return bz_s, wy, lam_re, lam_im, dvec


def _s5_kernel(u_ref, bz_ref, wy_ref, lr_ref, li_ref, d_ref, h0_ref, y_ref, hf_ref, s_scr, hp_scr,
               *, nk, bs, lc, rb):
    nh = s_scr.shape[0] // 2
    blk = lambda a, c: a[:, c * LANES:(c + 1) * LANES]

    def ucat(r0):
        return jnp.concatenate([u_ref[0, pl.ds(r0 * lc + t, rb, stride=lc), :] for t in range(lc)], axis=1)

    def scan_rows(r0):
        if nk == 1:
            return pl.ds(r0, rb)
        seq = r0 // nk
        return pl.ds((r0 - seq * nk) * bs + seq, rb, stride=bs)

    def phase1(i, _):
        r0 = pl.multiple_of(i * rb, rb)
        s = jnp.dot(ucat(r0).astype(BF16), bz_ref[0], preferred_element_type=F32)
        for c in range(2 * nh):
            s_scr[c, scan_rows(r0), :] = blk(s, c)
        return 0

    lax.fori_loop(0, (bs * nk) // rb, phase1, 0)
    lr, li = lr_ref[0], li_ref[0]

    def step(k, h):
        rows = pl.ds(pl.multiple_of(k * bs, bs), bs)
        new = [None] * (2 * nh)
        for c in range(nh):
            h_re, h_im = h[c], h[nh + c]
            s_re, s_im = s_scr[c, rows, :], s_scr[nh + c, rows, :]
            hp_scr[c, rows, :] = h_re
            hp_scr[nh + c, rows, :] = h_im
            new[c] = blk(lr, c) * h_re - blk(li, c) * h_im + s_re
            new[nh + c] = blk(lr, c) * h_im + blk(li, c) * h_re + s_im
        return tuple(new)

    h0 = h0_ref[0, 0]
    hf = lax.fori_loop(0, nk, step, tuple(blk(h0, c) for c in range(2 * nh)), unroll=math.gcd(nk, 4))
    hf_ref[0, 0] = jnp.concatenate(hf, axis=1)

    def phase3(i, _):
        r0 = pl.multiple_of(i * rb, rb)
        uc = ucat(r0)
        hprev = jnp.concatenate([hp_scr[c, scan_rows(r0), :] for c in range(2 * nh)], axis=1)
        lhs = jnp.concatenate([hprev.astype(BF16), uc.astype(BF16)], axis=1)
        y = jnp.dot(lhs, wy_ref[0], preferred_element_type=F32) + d_ref[0] * uc
        for t in range(lc):
            y_ref[0, pl.ds(r0 * lc + t, rb, stride=lc), :] = y[:, t * LANES:(t + 1) * LANES]
        return 0

    lax.fori_loop(0, (bs * nk) // rb, phase3, 0)


def _s5(u, h0, weights, nb, t, lc):
    bz, wy, lam_re, lam_im, dvec = weights
    nsl = u.shape[0]
    nk = t // lc
    sw = h0.shape[-1]
    bs = nb
    while bs * t > S5_MAX_TOKENS and bs % 2 == 0:
        bs //= 2
    rb = _pick_tile(bs, 256) if nk == 1 else _pick_tile(nk, 256)
    h0 = h0.reshape(nsl, nb // bs, bs, sw)
    slab = lambda j, b: (j, 0, 0)
    tok = lambda j, b: (j, b, 0)
    st = lambda j, b: (j, b, 0, 0)
    y, hf = pl.pallas_call(
        functools.partial(_s5_kernel, nk=nk, bs=bs, lc=lc, rb=rb),
        grid=(nsl, nb // bs),
        in_specs=[pl.BlockSpec((1, bs * t, LANES), tok), pl.BlockSpec((1,) + bz.shape[1:], slab),
                  pl.BlockSpec((1,) + wy.shape[1:], slab), pl.BlockSpec((1, 1, sw // 2), slab),
                  pl.BlockSpec((1, 1, sw // 2), slab), pl.BlockSpec((1, 1, lc * LANES), slab),
                  pl.BlockSpec((1, 1, bs, sw), st)],
        out_specs=[pl.BlockSpec((1, bs * t, LANES), tok), pl.BlockSpec((1, 1, bs, sw), st)],
        out_shape=[jax.ShapeDtypeStruct(u.shape, F32), jax.ShapeDtypeStruct(h0.shape, F32)],
        scratch_shapes=[pltpu.VMEM((sw // LANES, bs * nk, LANES), F32)] * 2,
        compiler_params=_cparams(("parallel", "parallel")), name="s5",
    )(u, bz, wy, lam_re, lam_im, dvec, h0)
    return y, hf.reshape(nsl, nb, sw)


def _ones_lane(h):
    return V_DIM if h % 2 == 0 else 0


def _attn_prompt_kernel(cq_ref, ckv_ref, krp_ref, cos_ref, sinp_ref, sinm_ref, wq_ref, wk_ref, wv_ref,
                        gq_ref, gk_ref, o_ref, k_scr, v_scr, q_scr, *, tq, tk, hb):
    qi = pl.program_id(1)

    @pl.when(qi == 0)
    def _():
        c = ckv_ref[...].astype(BF16)
        krp = krp_ref[...]
        for h in range(N_HEADS):
            kh = jnp.dot(c, wk_ref[h], preferred_element_type=F32) + krp
            k_scr[h] = _rms_qk(kh, gk_ref[...]).astype(BF16)
        vlane = lax.broadcasted_iota(jnp.int32, (1, LANES), 1)
        for h in range(N_HEADS):
            one = jnp.where(vlane == _ones_lane(h), 1.0, 0.0)
            v_scr[h] = (jnp.dot(c, wv_ref[h], preferred_element_type=F32) + one).astype(BF16)

    cq = cq_ref[...]
    cos, sinp, sinm = cos_ref[...], sinp_ref[...], sinm_ref[...]
    sin = sinp - sinm
    for h in range(N_HEADS):
        qq = jnp.dot(cq, wq_ref[h], preferred_element_type=F32)
        q = qq[:, :LANES] * cos + qq[:, LANES:] * sin
        q_scr[h] = (_rms_qk(q, gq_ref[...]) * (LOG2E / math.sqrt(QK))).astype(BF16)

    assert tq == tk
    causal = lax.broadcasted_iota(jnp.int32, (tq, tk), 1) <= lax.broadcasted_iota(jnp.int32, (tq, tk), 0)
    lane = lax.broadcasted_iota(jnp.int32, (tq, LANES), 1)

    def fold(m, acc, q, h, ks, mask):
        s = lax.dot_general(q, k_scr[h, ks, :], NT_DIMS, preferred_element_type=F32)
        if mask is not None:
            s = jnp.where(mask, s, -jnp.inf)
        m_new = jnp.maximum(m, jnp.max(s, axis=-1, keepdims=True))
        p = jnp.exp2(s - m_new)
        acc = acc * jnp.exp2(m - m_new) + jnp.dot(p.astype(BF16), v_scr[h, ks, :], preferred_element_type=F32)
        return m_new, acc

    def kv_step(j, carry, heads, mask):
        ks = pl.ds(pl.multiple_of(j * tk, tk), tk)
        return tuple(fold(m, acc, q_scr[h], h, ks, mask) for (m, acc), h in zip(carry, heads))

    for h0 in range(0, N_HEADS, hb):
        heads = tuple(range(h0, h0 + hb))
        init = (jnp.full((tq, 1), -jnp.inf, F32), jnp.zeros((tq, LANES), F32))
        carry = lax.fori_loop(0, qi, functools.partial(kv_step, heads=heads, mask=None), (init,) * hb)
        carry = kv_step(qi, carry, heads, causal)
        outs = [acc / acc[:, _ones_lane(h):_ones_lane(h) + 1] for (_, acc), h in zip(carry, heads)]
        for k in range(0, hb, 2):
            hp = (h0 + k) // 2
            o_ref[:, hp * LANES:(hp + 1) * LANES] = jnp.where(lane < V_DIM, outs[k], outs[k + 1])


def _attn_prompt(cq, ckv, krp, tables, wq, wk, wvp, gq, gk, nb, t, tq, tk, hb=ATTN_HEADS_PER_BODY):
    n = cq.shape[0]
    nq = t // tq
    qrow = lambda b, i: (b * nq + i, 0)
    seq = lambda b, i: (b, 0)
    tab = pl.BlockSpec((tq, LANES), lambda b, i: (i, 0))
    c3 = lambda b, i: (0, 0, 0)
    c2 = lambda b, i: (0, 0)
    return pl.pallas_call(
        functools.partial(_attn_prompt_kernel, tq=tq, tk=tk, hb=hb),
        grid=(nb, nq),
        in_specs=[pl.BlockSpec((tq, cq.shape[1]), qrow), pl.BlockSpec((t, ckv.shape[1]), seq),
                  pl.BlockSpec((t, LANES), seq), tab, tab, tab,
                  pl.BlockSpec(wq.shape, c3), pl.BlockSpec(wk.shape, c3), pl.BlockSpec(wvp.shape, c3),
                  pl.BlockSpec((1, LANES), c2), pl.BlockSpec((1, LANES), c2)],
        out_specs=pl.BlockSpec((tq, N_HEADS * V_DIM), qrow),
        out_shape=jax.ShapeDtypeStruct((n, N_HEADS * V_DIM), F32),
        scratch_shapes=[pltpu.VMEM((N_HEADS, t, LANES), BF16), pltpu.VMEM((N_HEADS, t, LANES), BF16),
                        pltpu.VMEM((N_HEADS, tq, LANES), BF16)],
        compiler_params=_cparams(("parallel", "arbitrary")), name="attn_prompt",
    )(cq, ckv, krp, *tables, wq, wk, wvp, gq, gk)


def _q_sample_kernel(cq_ref, cos_ref, sinp_ref, sinm_ref, wq_ref, wukt_ref, sel_ref, gq_ref, gk_ref,
                     qa_ref, qr_ref):
    cq = cq_ref[...]
    for h in range(N_HEADS):
        q = _rope(jnp.dot(cq, wq_ref[h], preferred_element_type=F32), cos_ref[...], sinp_ref[...], sinm_ref[...])
        q = _rms_qk(q, gq_ref[...]) * (LOG2E / math.sqrt(QK))
        qk = (q * gk_ref[...]).astype(BF16)
        qa_ref[h] = jnp.dot(qk, wukt_ref[h], preferred_element_type=F32).astype(BF16)
        qr_ref[h] = jnp.dot(qk, sel_ref[...], preferred_element_type=F32).astype(BF16)


def _q_sample(cq, tables, wq, wukt, sel, gq, gk):
    n = cq.shape[0]
    return pl.pallas_call(
        _q_sample_kernel,
        out_shape=[jax.ShapeDtypeStruct((N_HEADS, n, LANES), BF16), jax.ShapeDtypeStruct((N_HEADS, n, ROPE), BF16)],
        compiler_params=pltpu.CompilerParams(vmem_limit_bytes=VMEM_LIMIT), name="q_sample",
    )(cq, *tables, wq, wukt, sel, gq, gk)


def _attn_sample_kernel(pt_ref, qa_ref, qr_ref, qa_prev_ref, qr_prev_ref, cnew_ref, krnew_ref, wukt_ref, wv_ref,
                        ckv_hbm, kr_hbm, o_ref, cbuf, kbuf, sems, s_scr, cb_scr, m_scr, l_scr, acc_scr,
                        *, pc, nc, total, t_dec):
    g = pl.program_id(0)
    slot = g % 2
    rows = N_HEADS * t_dec

    def page_copies(step, slot_):
        base = step * pc
        cps = []
        for j in range(pc):
            pg = pt_ref[base + j]
            cps.append(pltpu.make_async_copy(ckv_hbm.at[pg], cbuf.at[slot_, j], sems.at[0, slot_]))
            cps.append(pltpu.make_async_copy(kr_hbm.at[pg], kbuf.at[slot_, j], sems.at[1, slot_]))
        return cps

    def wait_pages(slot_):
        pltpu.make_async_copy(ckv_hbm.at[pl.ds(0, pc)], cbuf.at[slot_], sems.at[0, slot_]).wait()
        pltpu.make_async_copy(kr_hbm.at[pl.ds(0, pc)], kbuf.at[slot_], sems.at[1, slot_]).wait()

    def reset_softmax():
        m_scr[...] = jnp.full(m_scr.shape, -jnp.inf, F32)
        l_scr[...] = jnp.zeros(l_scr.shape, F32)
        acc_scr[...] = jnp.zeros(acc_scr.shape, F32)

    @pl.when(g == 0)
    def _():
        for cp in page_copies(g, slot):
            cp.start()
        s_scr[1] = jnp.zeros(s_scr.shape[1:], F32)
        cb_scr[1] = jnp.zeros(cb_scr.shape[1:], BF16)
        reset_softmax()

    @pl.when((g >= 1) & ((g - 1) % nc == 0))
    def _():
        reset_softmax()

    def scores(qa, qr, cb, krt):
        nk = cb.shape[0]
        knt = lax.dot_general(wukt_ref[...], cb, NT_DIMS, preferred_element_type=F32)
        n2 = jnp.sum((knt * knt).reshape(N_HEADS, NOPE, nk), axis=1)
        kr2 = jnp.sum(krt * krt, axis=0, keepdims=True)
        rinv = lax.rsqrt((n2 + kr2) * (1.0 / QK) + EPS)
        s = lax.dot_general(qa, cb, NT_DIMS, preferred_element_type=F32)
        s += jnp.dot(qr, krt.astype(BF16), preferred_element_type=F32)
        return (s.reshape(N_HEADS, t_dec, nk) * rinv[:, None, :]).reshape(rows, nk)

    def fold(s, cb):
        m = m_scr[...]
        m_new = jnp.maximum(m, jnp.max(s, axis=-1, keepdims=True))
        p = jnp.exp2(s - m_new)
        corr = jnp.exp2(m - m_new)
        l_scr[...] = l_scr[...] * corr + jnp.sum(p, axis=-1, keepdims=True)
        acc_scr[...] = acc_scr[...] * corr + jnp.dot(p.astype(BF16), cb, preferred_element_type=F32)
        m_scr[...] = m_new

    def score_and_fold(cur, prev):
        wait_pages(cur)
        for cp in page_copies(jnp.minimum(g + 1, total - 1), prev):
            cp.start()
        s_prev, cb_prev = s_scr[prev], cb_scr[prev]
        cb = cbuf[cur].reshape(pc * PAGE, cbuf.shape[-1]).astype(BF16)
        krt = jnp.concatenate([kbuf[cur, j] for j in range(pc)], axis=1)
        s_scr[cur] = scores(qa_ref[...].reshape(rows, LANES), qr_ref[...].reshape(rows, ROPE), cb, krt)
        cb_scr[cur] = cb
        fold(s_prev, cb_prev)

    for parity in range(2):
        pl.when(slot == parity)(functools.partial(score_and_fold, parity, 1 - parity))

    @pl.when(g == total)
    def _():
        wait_pages(1 - slot)

    @pl.when((g >= 1) & ((g - 1) % nc == nc - 1))
    def _():
        key = lax.broadcasted_iota(jnp.int32, (rows, PAGE), 1)
        step = lax.broadcasted_iota(jnp.int32, (rows, PAGE), 0) % t_dec
        cb_new = cnew_ref[0].astype(BF16)
        s_new = scores(qa_prev_ref[...].reshape(rows, LANES), qr_prev_ref[...].reshape(rows, ROPE), cb_new, krnew_ref[0])
        fold(jnp.where(key <= step, s_new, -jnp.inf), cb_new)
        o_lat = (acc_scr[...] / l_scr[...]).astype(BF16)
        lane = lax.broadcasted_iota(jnp.int32, (t_dec, LANES), 1)
        for hp in range(N_HEADS // 2):
            lo = jnp.dot(o_lat[(2 * hp) * t_dec:(2 * hp + 1) * t_dec], wv_ref[hp], preferred_element_type=F32)
            hi = jnp.dot(o_lat[(2 * hp + 1) * t_dec:(2 * hp + 2) * t_dec], wv_ref[hp], preferred_element_type=F32)
            o_ref[:, hp * LANES:(hp + 1) * LANES] = jnp.where(lane < V_DIM, lo, hi)


def _attn_sample(page_table, qa, qr, cnew_pad, krnew_pad, wukt_all, wvp, cache_ckv, cache_kr, t_dec, pc):
    nb, npg = page_table.shape
    nc = npg // pc
    total = nb * nc
    kv_rank = cache_ckv.shape[-1]
    nkeys = pc * PAGE
    rows = N_HEADS * t_dec
    seq_cur = lambda g: jnp.minimum(g // nc, nb - 1)
    seq_prev = lambda g: jnp.maximum(g - 1, 0) // nc
    grid_spec = pltpu.PrefetchScalarGridSpec(
        num_scalar_prefetch=1,
        grid=(total + 1,),
        in_specs=[pl.BlockSpec((N_HEADS, t_dec, LANES), lambda g, pt: (0, seq_cur(g), 0)),
                  pl.BlockSpec((N_HEADS, t_dec, ROPE), lambda g, pt: (0, seq_cur(g), 0)),
                  pl.BlockSpec((N_HEADS, t_dec, LANES), lambda g, pt: (0, seq_prev(g), 0)),
                  pl.BlockSpec((N_HEADS, t_dec, ROPE), lambda g, pt: (0, seq_prev(g), 0)),
                  pl.BlockSpec((1, PAGE, kv_rank), lambda g, pt: (seq_prev(g), 0, 0)),
                  pl.BlockSpec((1, ROPE, PAGE), lambda g, pt: (seq_prev(g), 0, 0)),
                  pl.BlockSpec(wukt_all.shape, lambda g, pt: (0, 0)),
                  pl.BlockSpec(wvp.shape, lambda g, pt: (0, 0, 0)),
                  pl.BlockSpec(memory_space=pl.ANY), pl.BlockSpec(memory_space=pl.ANY)],
        out_specs=pl.BlockSpec((t_dec, N_HEADS * V_DIM), lambda g, pt: (seq_prev(g), 0)),
        scratch_shapes=[pltpu.VMEM((2, pc, PAGE, kv_rank), F32), pltpu.VMEM((2, pc, ROPE, PAGE), F32),
                        pltpu.SemaphoreType.DMA((2, 2)),
                        pltpu.VMEM((2, rows, nkeys), F32), pltpu.VMEM((2, nkeys, kv_rank), BF16),
                        pltpu.VMEM((rows, 1), F32), pltpu.VMEM((rows, 1), F32), pltpu.VMEM((rows, kv_rank), F32)],
    )
    return pl.pallas_call(
        functools.partial(_attn_sample_kernel, pc=pc, nc=nc, total=total, t_dec=t_dec),
        grid_spec=grid_spec,
        out_shape=jax.ShapeDtypeStruct((nb * t_dec, N_HEADS * V_DIM), F32),
        compiler_params=_cparams(("arbitrary",)), name="attn_sample",
    )(page_table.reshape(-1), qa, qr, qa, qr, cnew_pad, krnew_pad, wukt_all, wvp, cache_ckv, cache_kr)


def _merge_kernel(xp_ref, xs_ref, yp_ref, ys_ref, ap_ref, as_ref, wglu_ref, bglu_ref, gs_ref, ga_ref,
                  wo1_ref, wo2_ref, g2_ref, wrh_ref, wrl_ref, br_ref, tri_ref,
                  h_ref, xn_ref, meta_ref, cnt_ref, run_scr, *, n_prompt_tiles):
    i = pl.program_id(0)

    @pl.when(i == 0)
    def _():
        run_scr[...] = jnp.zeros(run_scr.shape, F32)

    is_p = i < n_prompt_tiles
    run = run_scr[...]
    for rows in _row_chunks(h_ref.shape[0], tri_ref.shape[0]):
        x = jnp.where(is_p, xp_ref[rows, :], xs_ref[rows, :])
        slabs = lambda ref: jnp.concatenate([ref[j, rows, :] for j in range(ref.shape[0])], axis=1)
        y = jnp.where(is_p, slabs(yp_ref), slabs(ys_ref))
        att = jnp.where(is_p, ap_ref[rows, :], as_ref[rows, :])

        y = jax.nn.gelu(y)
        glu = jnp.dot(y.astype(BF16), wglu_ref[...], preferred_element_type=F32) + bglu_ref[...]
        ssm = y * jax.nn.sigmoid(glu)
        mix = jnp.dot(_rms(ssm, gs_ref[...]).astype(BF16), wo1_ref[...], preferred_element_type=F32)
        mix += jnp.dot(_rms(att, ga_ref[...]).astype(BF16), wo2_ref[...], preferred_element_type=F32)
        h = x + mix
        h_ref[rows, :] = h
        xn = _rms(h, g2_ref[...])
        xn_ref[rows, :] = xn
        xh = xn.astype(BF16)
        xl = (xn - xh.astype(F32)).astype(BF16)
        logits = (jnp.dot(xh, wrh_ref[...], preferred_element_type=F32)
                  + jnp.dot(xl, wrh_ref[...], preferred_element_type=F32)
                  + jnp.dot(xh, wrl_ref[...], preferred_element_type=F32)) + br_ref[...]

        tc = logits.shape[0]
        lane_i = lax.broadcasted_iota(jnp.int32, (tc, LANES), 1)
        lane = lane_i.astype(F32)
        big = float(LANES)
        first = lambda hit: jnp.min(jnp.where(hit, lane, big), axis=-1, keepdims=True)
        gl = jnp.where(lane_i < N_GROUPS, logits, -jnp.inf)
        gmax = jnp.max(gl, axis=-1, keepdims=True)
        grp = first(gl == gmax)
        p_sel = 1.0 / jnp.sum(jnp.exp(gl - gmax), axis=-1, keepdims=True)
        lane_grp = ((lane_i - N_GROUPS) >> 3).astype(F32)
        in_grp = (lane_i >= N_GROUPS) & (lane_i < N_GROUPS + N_EXPERTS) & (lane_grp == grp)
        el = jnp.where(in_grp, logits, -jnp.inf)
        m1 = jnp.max(el, axis=-1, keepdims=True)
        i1 = first(el == m1)
        el2 = jnp.where(lane == i1, -jnp.inf, el)
        m2 = jnp.max(el2, axis=-1, keepdims=True)
        i2 = first(el2 == m2)
        e21 = jnp.exp(m2 - m1)
        w1 = p_sel / (1.0 + e21)
        w2 = p_sel * e21 / (1.0 + e21)
        e1, e2 = i1 - N_GROUPS, i2 - N_GROUPS

        oh1, oh2 = lane == e1, lane == e2
        oh = jnp.where(oh1 | oh2, 1.0, 0.0)
        before = jnp.dot(tri_ref[...], oh.astype(BF16), preferred_element_type=F32) + run
        r1 = jnp.sum(jnp.where(oh1, before, 0.0), axis=-1, keepdims=True)
        r2 = jnp.sum(jnp.where(oh2, before, 0.0), axis=-1, keepdims=True)
        run = run + jnp.sum(oh, axis=0, keepdims=True)
        meta = jnp.zeros((tc, LANES), F32)
        for k, v in enumerate((e1, e2, w1, w2, r1, r2)):
            meta = jnp.where(lane_i == k, v, meta)
        meta_ref[rows, :] = meta

    run_scr[...] = run
    cnt_ref[...] = run


def _merge(x_p, x_s, y_p, y_s, a_p, a_s, wglu, bglu, gs, ga, wo1, wo2, g2, wr, br, tm):
    wrh = wr.astype(BF16)
    wrl = (wr - wrh.astype(F32)).astype(BF16)
    n_p, d = x_p.shape
    n_s = x_s.shape[0]
    npt, nst = n_p // tm, n_s // tm
    n_all = n_p + n_s
    tc = tm
    tri = (lax.broadcasted_iota(jnp.int32, (tc, tc), 0) > lax.broadcasted_iota(jnp.int32, (tc, tc), 1)).astype(BF16)
    prow = lambda i: (jnp.minimum(i, npt - 1), 0)
    srow = lambda i: (jnp.maximum(i - npt, 0), 0)
    row = lambda i: (i, 0)
    const = lambda i: (0, 0)
    w = a_p.shape[1]
    nsl = y_p.shape[0]
    full = lambda a: pl.BlockSpec(a.shape, const)
    return pl.pallas_call(
        functools.partial(_merge_kernel, n_prompt_tiles=npt),
        grid=(npt + nst,),
        in_specs=[pl.BlockSpec((tm, d), prow), pl.BlockSpec((tm, d), srow),
                  pl.BlockSpec((nsl, tm, LANES), lambda i: (0, jnp.minimum(i, npt - 1), 0)),
                  pl.BlockSpec((nsl, tm, LANES), lambda i: (0, jnp.maximum(i - npt, 0), 0)),
                  pl.BlockSpec((tm, w), prow), pl.BlockSpec((tm, w), srow),
                  full(wglu), full(bglu), full(gs), full(ga), full(wo1), full(wo2), full(g2), full(wrh), full(wrl), full(br),
                  full(tri)],
        out_specs=[pl.BlockSpec((tm, d), row), pl.BlockSpec((tm, d), row), pl.BlockSpec((tm, LANES), row),
                   pl.BlockSpec((1, LANES), const)],
        out_shape=[jax.ShapeDtypeStruct((n_all, d), F32), jax.ShapeDtypeStruct((n_all, d), F32),
                   jax.ShapeDtypeStruct((n_all, LANES), F32), jax.ShapeDtypeStruct((1, LANES), F32)],
        scratch_shapes=[pltpu.VMEM((1, LANES), F32)],
        compiler_params=_cparams(("arbitrary",)), name="merge",
    )(x_p, x_s, y_p, y_s, a_p, a_s, wglu, bglu, gs, ga, wo1, wo2, g2, wrh, wrl, br, tri)


def _start_row_gather(idx_ref, base, src_hbm, dst, sem, nrows):
    for r in range(nrows):
        pltpu.make_async_copy(src_hbm.at[pl.ds(idx_ref[base + r], 1)], dst.at[pl.ds(r, 1)], sem).start()


def _wait_row_gather(src_hbm, dst, sem, nrows):
    pltpu.make_async_copy(src_hbm.at[pl.ds(0, nrows)], dst, sem).wait()


def _pos_kernel(meta_ref, offs_ref, p1_ref, p2_ref):
    meta = meta_ref[...]
    tm = meta.shape[0]
    lane_i = lax.broadcasted_iota(jnp.int32, (tm, LANES), 1)
    lane = lane_i.astype(F32)
    diag = lax.broadcasted_iota(jnp.int32, (tm, LANES), 0) % LANES == lane_i

    def dense(e, r):
        pos = jnp.sum(jnp.where(lane == e, offs_ref[...], 0.0), axis=-1, keepdims=True) + r
        spread = jnp.where(diag, pos, 0.0).reshape(tm // LANES, LANES, LANES)
        return jnp.sum(spread, axis=1).astype(jnp.int32)

    p1_ref[...] = dense(meta[:, 0:1], meta[:, 4:5])
    p2_ref[...] = dense(meta[:, 1:2], meta[:, 5:6])


def _positions(meta, offs_vec, tm):
    n = meta.shape[0]
    rows = tm // LANES
    return pl.pallas_call(
        _pos_kernel,
        grid=(n // tm,),
        in_specs=[pl.BlockSpec((tm, LANES), lambda i: (i, 0)), pl.BlockSpec((1, LANES), lambda i: (0, 0))],
        out_specs=[pl.BlockSpec((rows, LANES), lambda i: (i, 0)), pl.BlockSpec((rows, LANES), lambda i: (i, 0))],
        out_shape=[jax.ShapeDtypeStruct((n // LANES, LANES), jnp.int32)] * 2,
        compiler_params=_cparams(("parallel",)), name="positions",
    )(meta, offs_vec)


def _dispatch_kernel(p1_ref, p2_ref, tend_ref, xn_ref, xs_hbm, zbuf, sem, *, tmd, tme):
    base = pl.program_id(0) * tmd

    @pl.when(pl.program_id(0) == 0)
    def _():
        zbuf[...] = jnp.zeros(zbuf.shape, zbuf.dtype)

        def last_tile(e, carry, wait):
            end = tend_ref[e]
            begin = jnp.where(e == 0, 0, tend_ref[jnp.maximum(e - 1, 0)])

            @pl.when(end > begin)
            def _():
                cp = pltpu.make_async_copy(zbuf, xs_hbm.at[pl.ds((end - 1) * tme, tme)], sem)
                cp.wait() if wait else cp.start()
            return carry

        def spare_tile(t, carry, wait):
            cp = pltpu.make_async_copy(zbuf, xs_hbm.at[pl.ds(t * tme, tme)], sem)
            cp.wait() if wait else cp.start()
            return carry

        n_used, n_tiles = tend_ref[N_EXPERTS - 1], xs_hbm.shape[0] // tme
        for wait in (False, True):
            lax.fori_loop(0, N_EXPERTS, functools.partial(last_tile, wait=wait), 0)
            lax.fori_loop(n_used, n_tiles, functools.partial(spare_tile, wait=wait), 0)

    for r in range(tmd):
        src = xn_ref.at[pl.ds(r, 1)]
        pltpu.make_async_copy(src, xs_hbm.at[pl.ds(p1_ref[base + r], 1)], sem).start()
        pltpu.make_async_copy(src, xs_hbm.at[pl.ds(p2_ref[base + r], 1)], sem).start()
    for _ in range(2):
        pltpu.make_async_copy(xn_ref, xs_hbm.at[pl.ds(0, tmd)], sem).wait()


def _dispatch(pos1, pos2, tile_end, xn, n_rows, tmd, tme):
    n, d = xn.shape
    grid_spec = pltpu.PrefetchScalarGridSpec(
        num_scalar_prefetch=3,
        grid=(n // tmd,),
        in_specs=[pl.BlockSpec((tmd, d), lambda i, p1, p2, te: (i, 0))],
        out_specs=pl.BlockSpec(memory_space=pl.ANY),
        scratch_shapes=[pltpu.VMEM((tme, d), xn.dtype), pltpu.SemaphoreType.DMA],
    )
    return pl.pallas_call(
        functools.partial(_dispatch_kernel, tmd=tmd, tme=tme),
        grid_spec=grid_spec,
        out_shape=jax.ShapeDtypeStruct((n_rows, d), xn.dtype),
        compiler_params=_cparams(("arbitrary",)), name="dispatch",
    )(pos1, pos2, tile_end, xn)


def _experts_kernel(te_ref, nu_ref, x_ref, wg_ref, wu_ref, wd_ref, o_ref, wg_scr, wu_scr, wd_scr):
    i = pl.program_id(0)

    @pl.when((i == 0) | (te_ref[i] != te_ref[jnp.maximum(i - 1, 0)]))
    def _():
        wg_scr[...] = wg_ref[0].astype(BF16)
        wu_scr[...] = wu_ref[0].astype(BF16)
        wd_scr[...] = wd_ref[0].astype(BF16)

    @pl.when(i < nu_ref[0])
    def _():
        x = x_ref[...].astype(BF16)
        hg = jnp.dot(x, wg_scr[...], preferred_element_type=F32)
        hu = jnp.dot(x, wu_scr[...], preferred_element_type=F32)
        hh = (jax.nn.silu(hg) * hu).astype(BF16)
        o_ref[...] = jnp.dot(hh, wd_scr[...], preferred_element_type=F32)

    @pl.when(i >= nu_ref[0])
    def _():
        o_ref[...] = jnp.zeros(o_ref.shape, F32)


def _experts(tile_expert, n_used, xs, w_gate, w_up, w_down, tme):
    nt = tile_expert.shape[0]
    ne, d, de = w_gate.shape
    wmap = lambda i, te, nu: (te[i], 0, 0)
    grid_spec = pltpu.PrefetchScalarGridSpec(
        num_scalar_prefetch=2,
        grid=(nt,),
        in_specs=[pl.BlockSpec((tme, d), lambda i, te, nu: (jnp.minimum(i, nu[0] - 1), 0)),
                  pl.BlockSpec((1, d, de), wmap), pl.BlockSpec((1, d, de), wmap), pl.BlockSpec((1, de, d), wmap)],
        out_specs=pl.BlockSpec((tme, d), lambda i, te, nu: (i, 0)),
        scratch_shapes=[pltpu.VMEM((d, de), BF16), pltpu.VMEM((d, de), BF16), pltpu.VMEM((de, d), BF16)],
    )
    return pl.pallas_call(
        _experts_kernel,
        grid_spec=grid_spec,
        out_shape=jax.ShapeDtypeStruct((nt * tme, d), F32),
        compiler_params=_cparams(("arbitrary",)), name="experts",
    )(tile_expert, n_used, xs, w_gate, w_up, w_down)


def _combine_kernel(p1_ref, p2_ref, h_ref, meta_ref, eo_hbm, y_ref, buf, sems, *, tmc, nsteps):
    i = pl.program_id(0)
    slot = i % 2

    def start(step, slot_):
        _start_row_gather(p1_ref, step * tmc, eo_hbm, buf.at[slot_, pl.ds(0, tmc)], sems.at[slot_], tmc)
        _start_row_gather(p2_ref, step * tmc, eo_hbm, buf.at[slot_, pl.ds(tmc, tmc)], sems.at[slot_], tmc)

    @pl.when(i == 0)
    def _():
        start(0, 0)

    @pl.when(i + 1 < nsteps)
    def _():
        start(i + 1, 1 - slot)

    _wait_row_gather(eo_hbm, buf.at[slot], sems.at[slot], 2 * tmc)
    meta = meta_ref[...]
    y_ref[...] = h_ref[...] + meta[:, 2:3] * buf[slot, :tmc, :] + meta[:, 3:4] * buf[slot, tmc:, :]


def _combine(pos1, pos2, h_all, meta_all, eo, row0, tmc):
    n = pos1.shape[0]
    d = h_all.shape[1]
    nsteps = n // tmc
    off = row0 // tmc
    grid_spec = pltpu.PrefetchScalarGridSpec(
        num_scalar_prefetch=2,
        grid=(nsteps,),
        in_specs=[pl.BlockSpec((tmc, d), lambda i, p1, p2: (i + off, 0)),
                  pl.BlockSpec((tmc, LANES), lambda i, p1, p2: (i + off, 0)),
                  pl.BlockSpec(memory_space=pl.ANY)],
        out_specs=pl.BlockSpec((tmc, d), lambda i, p1, p2: (i, 0)),
        scratch_shapes=[pltpu.VMEM((2, 2 * tmc, d), F32), pltpu.SemaphoreType.DMA((2,))],
    )
    return pl.pallas_call(
        functools.partial(_combine_kernel, tmc=tmc, nsteps=nsteps),
        grid_spec=grid_spec,
        out_shape=jax.ShapeDtypeStruct((n, d), F32),
        compiler_params=_cparams(("arbitrary",)), name="combine",
    )(pos1, pos2, h_all, meta_all, eo)


def _pick_tile(n, pref):
    t = min(pref, n)
    while n % t:
        t //= 2
    return t


def _layer(x_p, x_s, cache_ckv, cache_kr, st_re, st_im, page_table, p):
    nb_p, t_p, d = x_p.shape
    nb_s, t_s, _ = x_s.shape
    n_p, n_s = nb_p * t_p, nb_s * t_s
    past_len = page_table.shape[1] * PAGE
    g = p['ssm_a_re'].shape[0]
    ssm_w = g * SSM_CH
    q_rank = p['w_uq'].shape[0]
    kv_rank = p['w_uk'].shape[0]

    w_in = p['w_in']
    o3 = ssm_w + q_rank + kv_rank
    w_pad = jnp.zeros((d, o3 + LANES), F32).at[:, :o3].set(w_in[:, :o3])
    w_pad = w_pad.at[:, o3 + NOPE:o3 + QK].set(w_in[:, o3:]).astype(BF16)
    row2 = lambda v: v.reshape(1, -1).astype(F32)
    padq = lambda v: jnp.pad(v, (0, LANES - QK)).reshape(1, LANES)
    wq = jnp.pad(p['w_uq'].transpose(1, 0, 2), ((0, 0), (0, 0), (0, LANES - QK))).astype(BF16)
    wk = jnp.pad(p['w_uk'].transpose(1, 0, 2), ((0, 0), (0, 0), (0, LANES - NOPE))).astype(BF16)
    wvp = p['w_uv'].reshape(kv_rank, N_HEADS // 2, 2 * V_DIM).transpose(1, 0, 2).astype(BF16)
    wv_h = p['w_uv'].transpose(1, 0, 2)
    wv8 = jnp.stack([jnp.pad(wv_h[h], ((0, 0), (V_DIM - _ones_lane(h), _ones_lane(h)))) for h in range(N_HEADS)])
    wv8 = wv8.astype(BF16)
    wukt_all = p['w_uk'].reshape(kv_rank, N_HEADS * NOPE).T.astype(BF16)
    wukt = jnp.pad(p['w_uk'].transpose(1, 2, 0), ((0, 0), (0, LANES - NOPE), (0, 0))).astype(BF16)
    sel = (jnp.arange(LANES)[:, None] == NOPE + jnp.arange(ROPE)[None, :]).astype(BF16)
    gq, gk = padq(p['qk_norm_q']), padq(p['qk_norm_k'])
    ssm_params = (p['ssm_a_re'], p['ssm_a_im'], p['ssm_log_dt'], p['ssm_b_re'], p['ssm_b_im'],
                  p['ssm_c_re'], p['ssm_c_im'], p['ssm_d'])

    tm_p = _pick_tile(t_p, 512)
    tab_p = _rope_tables(jnp.arange(t_p))
    u, cq, ckv_p, krp_p = _in_proj(x_p.reshape(n_p, d), row2(p['norm1_g']), w_pad, row2(p['q_norm_g']),
                                   row2(p['kv_norm_g']), tab_p, tm_p, ssm_w, q_rank, kv_rank)
    lc_p = _pick_tile(t_p, 8)
    nsl = ssm_w // LANES
    s5w_p = _s5_weights(*ssm_params, lc_p)
    y_ssm_p, hf_p = _s5(u, jnp.zeros((nsl, nb_p, 2 * GPB * SSM_P), F32), s5w_p, nb_p, t_p, lc_p)
    tq = _pick_tile(t_p, 512)
    h1, h2 = NOPE + ROPE // 2, QK
    wq_rot = jnp.concatenate([jnp.zeros_like(wq[..., :NOPE]), -wq[..., h1:h2], wq[..., NOPE:h1],
                              jnp.zeros_like(wq[..., h2:])], axis=-1)
    att_p = _attn_prompt(cq, ckv_p, krp_p, tab_p, jnp.concatenate([wq, wq_rot], axis=-1), wk, wv8, gq, gk,
                         nb_p, t_p, tq, tq)

    tm_s = _pick_tile(n_s, 512)
    pos_s = past_len + jnp.arange(t_s)
    tab_s = tuple(jnp.tile(a, (tm_s // t_s, 1)) for a in _rope_tables(pos_s))
    u, cq, ckv_s, krp_s = _in_proj(x_s.reshape(n_s, d), row2(p['norm1_g']), w_pad, row2(p['q_norm_g']),
                                   row2(p['kv_norm_g']), tab_s, tm_s, ssm_w, q_rank, kv_rank)
    slab_state = lambda s: s.astype(F32).reshape(nb_s, nsl, GPB * SSM_P).transpose(1, 0, 2)
    h0 = jnp.concatenate([slab_state(st_re), slab_state(st_im)], axis=-1)
    s5w_s = s5w_p if t_s == lc_p else _s5_weights(*ssm_params, t_s)
    y_ssm_s, hf_s = _s5(u, h0, s5w_s, nb_s, t_s, t_s)
    tab_q = tuple(jnp.tile(a, (n_s // tm_s, 1)) for a in tab_s)
    qa, qr = _q_sample(cq, tab_q, wq, wukt, sel, gq, gk)
    kr_s = krp_s[:, NOPE:QK]
    cnew = jnp.pad(ckv_s.reshape(nb_s, t_s, kv_rank), ((0, 0), (0, PAGE - t_s), (0, 0)))
    krnew = jnp.pad(kr_s.reshape(nb_s, t_s, ROPE), ((0, 0), (0, PAGE - t_s), (0, 0))).swapaxes(1, 2)
    pc = _pick_tile(page_table.shape[1], 32)
    att_s = _attn_sample(page_table, qa, qr, cnew, krnew, wukt_all, wvp, cache_ckv, jnp.swapaxes(cache_kr, 1, 2),
                         t_s, pc)

    tm = _pick_tile(math.gcd(n_p, n_s), 512)
    w_out = p['w_out'].astype(BF16)
    wr = jnp.zeros((d, LANES), F32).at[:, :N_GROUPS].set(p['w_router_group'])
    wr = wr.at[:, N_GROUPS:N_GROUPS + N_EXPERTS].set(p['w_router_expert'])
    br = jnp.zeros((1, LANES), F32).at[0, :N_GROUPS].set(p['b_router_group'])
    br = br.at[0, N_GROUPS:N_GROUPS + N_EXPERTS].set(p['b_router_expert'])
    h_all, xn_all, meta, cnt = _merge(
        x_p.reshape(n_p, d), x_s.reshape(n_s, d), y_ssm_p, y_ssm_s, att_p, att_s,
        p['ssm_w_glu'].astype(BF16), row2(p['ssm_b_glu']), row2(p['out_norm_ssm']), row2(p['out_norm_attn']),
        w_out[:ssm_w], w_out[ssm_w:], row2(p['norm2_g']), wr, br, tm)

    n_all = n_p + n_s
    tme = 512
    nt = (2 * n_all) // tme + N_EXPERTS
    counts = cnt[0, :N_EXPERTS].astype(jnp.int32)
    tiles_per = (counts + tme - 1) // tme
    tile_end = jnp.cumsum(tiles_per)
    tile_start = tile_end - tiles_per
    tile_id = jnp.arange(nt, dtype=jnp.int32)
    tile_expert = jnp.minimum(jnp.sum(tile_end[None, :] <= tile_id[:, None], axis=1), N_EXPERTS - 1).astype(jnp.int32)
    n_used = tile_end[-1:].astype(jnp.int32)
    offs_vec = jnp.zeros((1, LANES), F32).at[0, :N_EXPERTS].set((tile_start * tme).astype(F32))
    tmd = _pick_tile(n_all, 1024)
    pos1, pos2 = (a.reshape(-1) for a in _positions(meta, offs_vec, tmd))
    xs = _dispatch(pos1, pos2, tile_end.astype(jnp.int32), xn_all, nt * tme, tmd, tme)
    eo = _experts(tile_expert, n_used, xs, p['w_gate'], p['w_up'], p['w_down'], tme)

    tmc = _pick_tile(math.gcd(n_p, n_s), 256)
    y_p = _combine(pos1[:n_p], pos2[:n_p], h_all, meta, eo, 0, tmc)
    y_s = _combine(pos1[n_p:], pos2[n_p:], h_all, meta, eo, n_p, tmc)

    n_pg = t_p // PAGE

    def split(hf):
        nb = hf.shape[1]
        half = GPB * SSM_P
        unslab = lambda a: a.reshape(nsl, nb, GPB, SSM_P).transpose(1, 0, 2, 3).reshape(nb, g, SSM_P)
        return unslab(hf[..., :half]), unslab(hf[..., half:])

    hp_re, hp_im = split(hf_p)
    hs_re, hs_im = split(hf_s)
    return (y_p.reshape(nb_p, t_p, d), y_s.reshape(nb_s, t_s, d),
            ckv_p.reshape(nb_p, n_pg, PAGE, kv_rank), krp_p[:, NOPE:QK].reshape(nb_p, n_pg, PAGE, ROPE),
            hp_re, hp_im, ckv_s.reshape(nb_s, t_s, kv_rank), kr_s.reshape(nb_s, t_s, ROPE), hs_re, hs_im)


_PARAM_NAMES = ('norm1_g', 'w_in', 'ssm_a_re', 'ssm_a_im', 'ssm_log_dt', 'ssm_b_re', 'ssm_b_im', 'ssm_c_re',
                'ssm_c_im', 'ssm_d', 'ssm_w_glu', 'ssm_b_glu', 'q_norm_g', 'w_uq', 'kv_norm_g', 'w_uk', 'w_uv',
                'qk_norm_q', 'qk_norm_k', 'out_norm_ssm', 'out_norm_attn', 'w_out', 'norm2_g', 'w_router_group',
                'b_router_group', 'w_router_expert', 'b_router_expert', 'w_gate', 'w_up', 'w_down')


def kernel(x_prompt, x_sample, cache_ckv, cache_krope, state_ssm_re, state_ssm_im, page_table, norm1_g, w_in, ssm_a_re, ssm_a_im, ssm_log_dt, ssm_b_re, ssm_b_im, ssm_c_re, ssm_c_im, ssm_d, ssm_w_glu, ssm_b_glu, q_norm_g, w_uq, kv_norm_g, w_uk, w_uv, qk_norm_q, qk_norm_k, out_norm_ssm, out_norm_attn, w_out, norm2_g, w_router_group, b_router_group, w_router_expert, b_router_expert, w_gate, w_up, w_down):
    params = (norm1_g, w_in, ssm_a_re, ssm_a_im, ssm_log_dt, ssm_b_re, ssm_b_im, ssm_c_re, ssm_c_im, ssm_d,
              ssm_w_glu, ssm_b_glu, q_norm_g, w_uq, kv_norm_g, w_uk, w_uv, qk_norm_q, qk_norm_k, out_norm_ssm,
              out_norm_attn, w_out, norm2_g, w_router_group, b_router_group, w_router_expert, b_router_expert,
              w_gate, w_up, w_down)
    depth = w_in.shape[0]
    h_p, h_s = x_prompt, x_sample
    outs = [[] for _ in range(8)]
    for layer in range(depth):
        p = {k: v[layer] for k, v in zip(_PARAM_NAMES, params)}
        res = _layer(h_p, h_s, cache_ckv[layer], cache_krope[layer], state_ssm_re[layer], state_ssm_im[layer],
                     page_table, p)
        h_p, h_s = res[0], res[1]
        for acc, r in zip(outs, res[2:]):
            acc.append(r)
    return (h_p, h_s) + tuple(jnp.stack(o) for o in outs)
```

```python
import functools
import math

import jax
import jax.numpy as jnp
from jax import lax
from jax.experimental import pallas as pl
from jax.experimental.pallas import tpu as pltpu

F32 = jnp.float32
BF16 = jnp.bfloat16
HIGHEST = lax.Precision.HIGHEST

LANES = 128
SUBLANES = 8
VMEM_LIMIT = 56 * 1024 * 1024

EPS = 1e-6
ROPE_BASE = 10000.0
PAGE = 128
SSM_CH = 16
SSM_P = 64
GPB = LANES // SSM_CH
S5_MAX_TOKENS = 8192
N_HEADS = 8
NOPE = 64
ROPE = 32
QK = NOPE + ROPE
V_DIM = 64
N_GROUPS = 4
EXP_PER_GROUP = 8
N_EXPERTS = N_GROUPS * EXP_PER_GROUP
ATTN_HEADS_PER_BODY = 8

NT_DIMS = (((1,), (1,)), ((), ()))
LOG2E = math.log2(math.e)


def _cparams(sem, vmem=VMEM_LIMIT):
    return pltpu.CompilerParams(dimension_semantics=sem, vmem_limit_bytes=vmem)


def _rms(x, g):
    return x * lax.rsqrt(jnp.mean(x * x, axis=-1, keepdims=True) + EPS) * g


def _rms_qk(x, g):
    return x * lax.rsqrt(jnp.sum(x * x, axis=-1, keepdims=True) * (1.0 / QK) + EPS) * g


def _rope(x, cos, sinp, sinm):
    return x * cos + pltpu.roll(x, 16, 1) * sinp + pltpu.roll(x, LANES - 16, 1) * sinm


def _row_chunks(n, size=256):
    size = min(size, n)
    return [pl.ds(r, size) for r in range(0, n, size)]


def _rope_tables(pos):
    inv_freq = ROPE_BASE ** (-jnp.arange(0, ROPE, 2, dtype=F32) / ROPE)
    ang = pos.astype(F32)[:, None] * inv_freq[None, :]
    c, s = jnp.cos(ang), jnp.sin(ang)
    t = pos.shape[0]
    z = lambda n: jnp.zeros((t, n), F32)
    cosf = jnp.concatenate([jnp.ones((t, NOPE), F32), c, c, z(LANES - QK)], axis=1)
    sinp = jnp.concatenate([z(NOPE + ROPE // 2), s, z(LANES - QK)], axis=1)
    sinm = jnp.concatenate([z(NOPE), -s, z(LANES - QK + ROPE // 2)], axis=1)
    return cosf, sinp, sinm


def _in_proj_kernel(x_ref, g1_ref, w_ref, gq_ref, gkv_ref, cos_ref, sinp_ref, sinm_ref,
                    u_ref, cq_ref, ckv_ref, kr_ref, *, ssm_w, q_rank, kv_rank):
    o1, o2, o3 = ssm_w, ssm_w + q_rank, ssm_w + q_rank + kv_rank
    for rows in _row_chunks(x_ref.shape[0], 128):
        n = _rms(x_ref[rows, :], g1_ref[...])
        z = jnp.dot(n.astype(BF16), w_ref[...], preferred_element_type=F32)
        for j in range(ssm_w // LANES):
            u_ref[j, rows, :] = z[:, j * LANES:(j + 1) * LANES]
        cq_ref[rows, :] = _rms(z[:, o1:o2], gq_ref[...]).astype(cq_ref.dtype)
        ckv_ref[rows, :] = _rms(z[:, o2:o3], gkv_ref[...])
        kr_ref[rows, :] = _rope(z[:, o3:o3 + LANES], cos_ref[rows, :], sinp_ref[rows, :], sinm_ref[rows, :])


def _in_proj(x, g1, w_pad, gq, gkv, tables, tm, ssm_w, q_rank, kv_rank):
    n, d = x.shape
    period = tables[0].shape[0]
    nper = period // tm
    row = lambda i: (i, 0)
    const = lambda i: (0, 0)
    tab = pl.BlockSpec((tm, LANES), lambda i: (i % nper, 0))
    return pl.pallas_call(
        functools.partial(_in_proj_kernel, ssm_w=ssm_w, q_rank=q_rank, kv_rank=kv_rank),
        grid=(n // tm,),
        in_specs=[pl.BlockSpec((tm, d), row), pl.BlockSpec((1, d), const),
                  pl.BlockSpec(w_pad.shape, const), pl.BlockSpec((1, q_rank), const),
                  pl.BlockSpec((1, kv_rank), const), tab, tab, tab],
        out_specs=[pl.BlockSpec((ssm_w // LANES, tm, LANES), lambda i: (0, i, 0)), pl.BlockSpec((tm, q_rank), row),
                   pl.BlockSpec((tm, kv_rank), row), pl.BlockSpec((tm, LANES), row)],
        out_shape=[jax.ShapeDtypeStruct((ssm_w // LANES, n, LANES), F32), jax.ShapeDtypeStruct((n, q_rank), BF16),
                   jax.ShapeDtypeStruct((n, kv_rank), F32), jax.ShapeDtypeStruct((n, LANES), F32)],
        compiler_params=_cparams(("parallel",)), name="in_proj",
    )(x, g1, w_pad, gq, gkv, *tables)


def _s5_weights(a_re, a_im, log_dt, b_re, b_im, c_re, c_im, d, lc):
    g, p = a_re.shape
    dt = jnp.exp(log_dt)[:, None]
    den = a_re * a_re + a_im * a_im
    mag, ang = jnp.exp(a_re * dt), a_im * dt
    lb_re, lb_im = mag * jnp.cos(ang), mag * jnp.sin(ang)
    num_re = lb_re - 1.0
    coef_re = (num_re * a_re + lb_im * a_im) / den
    coef_im = (lb_im * a_re - num_re * a_im) / den
    bb_re = coef_re[..., None] * b_re - coef_im[..., None] * b_im
    bb_im = coef_re[..., None] * b_im + coef_im[..., None] * b_re

    def lam_pow(k):
        kk = k.astype(F32)[:, None, None]
        m = jnp.exp(a_re * dt * kk)
        return m * jnp.cos(ang * kk), m * jnp.sin(ang * kk)

    steps = jnp.arange(lc)
    pr, pi = lam_pow(lc - 1 - steps)
    bz_re = pr[..., None] * bb_re[None] - pi[..., None] * bb_im[None]
    bz_im = pr[..., None] * bb_im[None] + pi[..., None] * bb_re[None]
    bz = jnp.concatenate([bz_re, bz_im], axis=2)
    bz = bz.transpose(1, 0, 3, 2).reshape(g, lc * SSM_CH, 2 * p)
    qr, qi = lam_pow(steps + 1)
    cl_re = c_re[None] * qr[:, :, None, :] - c_im[None] * qi[:, :, None, :]
    cl_im = c_re[None] * qi[:, :, None, :] + c_im[None] * qr[:, :, None, :]
    cz = jnp.concatenate([cl_re, -cl_im], axis=3)
    cz = cz.transpose(1, 3, 0, 2).reshape(g, 2 * p, lc * SSM_CH)
    tr, ti = lam_pow(steps)
    cb_re = jnp.einsum('gcp,kgp,gpd->gkcd', c_re, tr, bb_re, precision=HIGHEST)
    cb_re -= jnp.einsum('gcp,kgp,gpd->gkcd', c_re, ti, bb_im, precision=HIGHEST)
    cb_re -= jnp.einsum('gcp,kgp,gpd->gkcd', c_im, tr, bb_im, precision=HIGHEST)
    cb_re -= jnp.einsum('gcp,kgp,gpd->gkcd', c_im, ti, bb_re, precision=HIGHEST)
    tau = steps[None, :] - steps[:, None]
    ksel = cb_re[:, jnp.clip(tau, 0, lc - 1)]
    ksel = jnp.where((tau >= 0)[None, :, :, None, None], ksel, 0.0)
    tz = ksel.transpose(0, 1, 4, 2, 3).reshape(g, lc * SSM_CH, lc * SSM_CH)
    assert lc * SSM_CH == LANES and 2 * p == LANES
    nsl = g // GPB
    sw = GPB * LANES
    col = jnp.arange(sw)
    rep_state = (jnp.arange(LANES)[:, None] == (col // (sw // 2)) * p + col % p).astype(BF16)
    rep_token = (jnp.arange(LANES)[:, None] == (col // LANES) * SSM_CH + col % SSM_CH).astype(BF16)
    grp_state = (col // p) % GPB
    grp_token = (col // SSM_CH) % GPB

    def slab(rows, rep, row_grp, col_grp):
        full = jnp.einsum('jrk,kc->jrc', rows.astype(BF16), rep, preferred_element_type=F32)
        return jnp.where(row_grp[:, None] == col_grp[None, :], full, 0.0).astype(BF16)

    bz_rows = bz.reshape(nsl, GPB, lc, SSM_CH, LANES).transpose(0, 2, 1, 3, 4).reshape(nsl, sw, LANES)
    tz_rows = tz.reshape(nsl, GPB, lc, SSM_CH, LANES).transpose(0, 2, 1, 3, 4).reshape(nsl, sw, LANES)
    cz_rows = cz.reshape(nsl, GPB, 2, p, LANES).transpose(0, 2, 1, 3, 4).reshape(nsl, sw, LANES)
    bz_s = slab(bz_rows, rep_state, grp_token, grp_state)
    wy = jnp.concatenate([slab(cz_rows, rep_token, grp_state, grp_token),
                          slab(tz_rows, rep_token, grp_token, grp_token)], axis=1)
    lr, li = lam_pow(jnp.array([lc]))
    lam_re = lr[0].reshape(nsl, 1, GPB * p)
    lam_im = li[0].reshape(nsl, 1, GPB * p)
    dvec = jnp.tile(d.reshape(nsl, 1, LANES), (1, 1, lc))
    return bz_s, wy, lam_re, lam_im, dvec


def _s5_kernel(u_ref, bz_ref, wy_ref, lr_ref, li_ref, d_ref, h0_ref, y_ref, hf_ref, s_scr, hp_scr,
               *, nk, bs, lc, rb):
    nh = s_scr.shape[0] // 2
    blk = lambda a, c: a[:, c * LANES:(c + 1) * LANES]

    def ucat(r0):
        return jnp.concatenate([u_ref[0, pl.ds(r0 * lc + t, rb, stride=lc), :] for t in range(lc)], axis=1)

    def scan_rows(r0):
        if nk == 1:
            return pl.ds(r0, rb)
        seq = r0 // nk
        return pl.ds((r0 - seq * nk) * bs + seq, rb, stride=bs)

    def phase1(i, _):
        r0 = pl.multiple_of(i * rb, rb)
        s = jnp.dot(ucat(r0).astype(BF16), bz_ref[0], preferred_element_type=F32)
        for c in range(2 * nh):
            s_scr[c, scan_rows(r0), :] = blk(s, c)
        return 0

    lax.fori_loop(0, (bs * nk) // rb, phase1, 0)
    lr, li = lr_ref[0], li_ref[0]

    def step(k, h):
        rows = pl.ds(pl.multiple_of(k * bs, bs), bs)
        new = [None] * (2 * nh)
        for c in range(nh):
            h_re, h_im = h[c], h[nh + c]
            s_re, s_im = s_scr[c, rows, :], s_scr[nh + c, rows, :]
            hp_scr[c, rows, :] = h_re
            hp_scr[nh + c, rows, :] = h_im
            new[c] = blk(lr, c) * h_re - blk(li, c) * h_im + s_re
            new[nh + c] = blk(lr, c) * h_im + blk(li, c) * h_re + s_im
        return tuple(new)

    h0 = h0_ref[0, 0]
    hf = lax.fori_loop(0, nk, step, tuple(blk(h0, c) for c in range(2 * nh)), unroll=math.gcd(nk, 4))
    hf_ref[0, 0] = jnp.concatenate(hf, axis=1)

    def phase3(i, _):
        r0 = pl.multiple_of(i * rb, rb)
        uc = ucat(r0)
        hprev = jnp.concatenate([hp_scr[c, scan_rows(r0), :] for c in range(2 * nh)], axis=1)
        lhs = jnp.concatenate([hprev.astype(BF16), uc.astype(BF16)], axis=1)
        y = jnp.dot(lhs, wy_ref[0], preferred_element_type=F32) + d_ref[0] * uc
        for t in range(lc):
            y_ref[0, pl.ds(r0 * lc + t, rb, stride=lc), :] = y[:, t * LANES:(t + 1) * LANES]
        return 0

    lax.fori_loop(0, (bs * nk) // rb, phase3, 0)


def _s5(u, h0, weights, nb, t, lc):
    bz, wy, lam_re, lam_im, dvec = weights
    nsl = u.shape[0]
    nk = t // lc
    sw = h0.shape[-1]
    bs = nb
    while bs * t > S5_MAX_TOKENS and bs % 2 == 0:
        bs //= 2
    rb = _pick_tile(bs, 256) if nk == 1 else _pick_tile(nk, 256)
    h0 = h0.reshape(nsl, nb // bs, bs, sw)
    slab = lambda j, b: (j, 0, 0)
    tok = lambda j, b: (j, b, 0)
    st = lambda j, b: (j, b, 0, 0)
    y, hf = pl.pallas_call(
        functools.partial(_s5_kernel, nk=nk, bs=bs, lc=lc, rb=rb),
        grid=(nsl, nb // bs),
        in_specs=[pl.BlockSpec((1, bs * t, LANES), tok), pl.BlockSpec((1,) + bz.shape[1:], slab),
                  pl.BlockSpec((1,) + wy.shape[1:], slab), pl.BlockSpec((1, 1, sw // 2), slab),
                  pl.BlockSpec((1, 1, sw // 2), slab), pl.BlockSpec((1, 1, lc * LANES), slab),
                  pl.BlockSpec((1, 1, bs, sw), st)],
        out_specs=[pl.BlockSpec((1, bs * t, LANES), tok), pl.BlockSpec((1, 1, bs, sw), st)],
        out_shape=[jax.ShapeDtypeStruct(u.shape, F32), jax.ShapeDtypeStruct(h0.shape, F32)],
        scratch_shapes=[pltpu.VMEM((sw // LANES, bs * nk, LANES), F32)] * 2,
        compiler_params=_cparams(("parallel", "parallel")), name="s5",
    )(u, bz, wy, lam_re, lam_im, dvec, h0)
    return y, hf.reshape(nsl, nb, sw)


def _ones_lane(h):
    return V_DIM if h % 2 == 0 else 0


def _attn_prompt_kernel(cq_ref, ckv_ref, krp_ref, cos_ref, sinp_ref, sinm_ref, wq_ref, wk_ref, wv_ref,
                        gq_ref, gk_ref, o_ref, k_scr, v_scr, q_scr, *, tq, tk, hb):
    qi = pl.program_id(1)

    @pl.when(qi == 0)
    def _():
        c = ckv_ref[...].astype(BF16)
        krp = krp_ref[...]
        for h in range(N_HEADS):
            kh = jnp.dot(c, wk_ref[h], preferred_element_type=F32) + krp
            k_scr[h] = _rms_qk(kh, gk_ref[...]).astype(BF16)
        vlane = lax.broadcasted_iota(jnp.int32, (1, LANES), 1)
        for h in range(N_HEADS):
            one = jnp.where(vlane == _ones_lane(h), 1.0, 0.0)
            v_scr[h] = (jnp.dot(c, wv_ref[h], preferred_element_type=F32) + one).astype(BF16)

    cq = cq_ref[...]
    cos, sinp, sinm = cos_ref[...], sinp_ref[...], sinm_ref[...]
    sin = sinp - sinm
    for h in range(N_HEADS):
        qq = jnp.dot(cq, wq_ref[h], preferred_element_type=F32)
        q = qq[:, :LANES] * cos + qq[:, LANES:] * sin
        q_scr[h] = (_rms_qk(q, gq_ref[...]) * (LOG2E / math.sqrt(QK))).astype(BF16)

    assert tq == tk
    causal = lax.broadcasted_iota(jnp.int32, (tq, tk), 1) <= lax.broadcasted_iota(jnp.int32, (tq, tk), 0)
    lane = lax.broadcasted_iota(jnp.int32, (tq, LANES), 1)

    def fold(m, acc, q, h, ks, mask):
        s = lax.dot_general(q, k_scr[h, ks, :], NT_DIMS, preferred_element_type=F32)
        if mask is not None:
            s = jnp.where(mask, s, -jnp.inf)
        m_new = jnp.maximum(m, jnp.max(s, axis=-1, keepdims=True))
        p = jnp.exp2(s - m_new)
        acc = acc * jnp.exp2(m - m_new) + jnp.dot(p.astype(BF16), v_scr[h, ks, :], preferred_element_type=F32)
        return m_new, acc

    def kv_step(j, carry, heads, mask):
        ks = pl.ds(pl.multiple_of(j * tk, tk), tk)
        return tuple(fold(m, acc, q_scr[h], h, ks, mask) for (m, acc), h in zip(carry, heads))

    for h0 in range(0, N_HEADS, hb):
        heads = tuple(range(h0, h0 + hb))
        init = (jnp.full((tq, 1), -jnp.inf, F32), jnp.zeros((tq, LANES), F32))
        carry = lax.fori_loop(0, qi, functools.partial(kv_step, heads=heads, mask=None), (init,) * hb)
        carry = kv_step(qi, carry, heads, causal)
        outs = [acc / acc[:, _ones_lane(h):_ones_lane(h) + 1] for (_, acc), h in zip(carry, heads)]
        for k in range(0, hb, 2):
            hp = (h0 + k) // 2
            o_ref[:, hp * LANES:(hp + 1) * LANES] = jnp.where(lane < V_DIM, outs[k], outs[k + 1])


def _attn_prompt(cq, ckv, krp, tables, wq, wk, wvp, gq, gk, nb, t, tq, tk, hb=ATTN_HEADS_PER_BODY):
    n = cq.shape[0]
    nq = t // tq
    qrow = lambda b, i: (b * nq + i, 0)
    seq = lambda b, i: (b, 0)
    tab = pl.BlockSpec((tq, LANES), lambda b, i: (i, 0))
    c3 = lambda b, i: (0, 0, 0)
    c2 = lambda b, i: (0, 0)
    return pl.pallas_call(
        functools.partial(_attn_prompt_kernel, tq=tq, tk=tk, hb=hb),
        grid=(nb, nq),
        in_specs=[pl.BlockSpec((tq, cq.shape[1]), qrow), pl.BlockSpec((t, ckv.shape[1]), seq),
                  pl.BlockSpec((t, LANES), seq), tab, tab, tab,
                  pl.BlockSpec(wq.shape, c3), pl.BlockSpec(wk.shape, c3), pl.BlockSpec(wvp.shape, c3),
                  pl.BlockSpec((1, LANES), c2), pl.BlockSpec((1, LANES), c2)],
        out_specs=pl.BlockSpec((tq, N_HEADS * V_DIM), qrow),
        out_shape=jax.ShapeDtypeStruct((n, N_HEADS * V_DIM), F32),
        scratch_shapes=[pltpu.VMEM((N_HEADS, t, LANES), BF16), pltpu.VMEM((N_HEADS, t, LANES), BF16),
                        pltpu.VMEM((N_HEADS, tq, LANES), BF16)],
        compiler_params=_cparams(("parallel", "arbitrary")), name="attn_prompt",
    )(cq, ckv, krp, *tables, wq, wk, wvp, gq, gk)


def _q_sample_kernel(cq_ref, cos_ref, sinp_ref, sinm_ref, wq_ref, wukt_ref, sel_ref, gq_ref, gk_ref,
                     qa_ref, qr_ref):
    cq = cq_ref[...]
    for h in range(N_HEADS):
        q = _rope(jnp.dot(cq, wq_ref[h], preferred_element_type=F32), cos_ref[...], sinp_ref[...], sinm_ref[...])
        q = _rms_qk(q, gq_ref[...]) * (LOG2E / math.sqrt(QK))
        qk = (q * gk_ref[...]).astype(BF16)
        qa_ref[h] = jnp.dot(qk, wukt_ref[h], preferred_element_type=F32).astype(BF16)
        qr_ref[h] = jnp.dot(qk, sel_ref[...], preferred_element_type=F32).astype(BF16)


def _q_sample(cq, tables, wq, wukt, sel, gq, gk):
    n = cq.shape[0]
    return pl.pallas_call(
        _q_sample_kernel,
        out_shape=[jax.ShapeDtypeStruct((N_HEADS, n, LANES), BF16), jax.ShapeDtypeStruct((N_HEADS, n, ROPE), BF16)],
        compiler_params=pltpu.CompilerParams(vmem_limit_bytes=VMEM_LIMIT), name="q_sample",
    )(cq, *tables, wq, wukt, sel, gq, gk)


def _attn_sample_kernel(pt_ref, p1_ref, p2_ref, tend_ref, qa_ref, qr_ref, qa_prev_ref, qr_prev_ref, cnew_ref, krnew_ref,
                        wukt_ref, wv_ref, xn_ref, ckv_hbm, kr_hbm, o_ref, xs_hbm,
                        cbuf, kbuf, sems, s_scr, cb_scr, m_scr, l_scr, acc_scr, zbuf, dsem,
                        *, pc, nc, total, t_dec, td, nds, tme):
    g = pl.program_id(0)
    slot = g % 2
    rows = N_HEADS * t_dec

    def dispatch_copies():
        base = jnp.minimum(g, nds - 1) * td
        cps = []
        for r in range(td):
            src = xn_ref.at[pl.ds(r, 1)]
            cps.append(pltpu.make_async_copy(src, xs_hbm.at[pl.ds(p1_ref[base + r], 1)], dsem))
            cps.append(pltpu.make_async_copy(src, xs_hbm.at[pl.ds(p2_ref[base + r], 1)], dsem))
        return cps

    def page_copies(step, slot_):
        base = step * pc
        cps = []
        for j in range(pc):
            pg = pt_ref[base + j]
            cps.append(pltpu.make_async_copy(ckv_hbm.at[pg], cbuf.at[slot_, j], sems.at[0, slot_]))
            cps.append(pltpu.make_async_copy(kr_hbm.at[pg], kbuf.at[slot_, j], sems.at[1, slot_]))
        return cps

    def wait_pages(slot_):
        pltpu.make_async_copy(ckv_hbm.at[pl.ds(0, pc)], cbuf.at[slot_], sems.at[0, slot_]).wait()
        pltpu.make_async_copy(kr_hbm.at[pl.ds(0, pc)], kbuf.at[slot_], sems.at[1, slot_]).wait()

    def reset_softmax():
        m_scr[...] = jnp.full(m_scr.shape, -jnp.inf, F32)
        l_scr[...] = jnp.zeros(l_scr.shape, F32)
        acc_scr[...] = jnp.zeros(acc_scr.shape, F32)

    @pl.when(g == 0)
    def _():
        for cp in page_copies(g, slot):
            cp.start()
        s_scr[1] = jnp.zeros(s_scr.shape[1:], F32)
        cb_scr[1] = jnp.zeros(cb_scr.shape[1:], BF16)
        reset_softmax()
        _zero_padding_tiles(tend_ref, xs_hbm, zbuf, dsem, tme)

    @pl.when((g >= 1) & ((g - 1) % nc == 0))
    def _():
        reset_softmax()

    def scores(qa, qr, cb, krt):
        nk = cb.shape[0]
        knt = lax.dot_general(wukt_ref[...], cb, NT_DIMS, preferred_element_type=F32)
        n2 = jnp.sum((knt * knt).reshape(N_HEADS, NOPE, nk), axis=1)
        kr2 = jnp.sum(krt * krt, axis=0, keepdims=True)
        rinv = lax.rsqrt((n2 + kr2) * (1.0 / QK) + EPS)
        s = lax.dot_general(qa, cb, NT_DIMS, preferred_element_type=F32)
        s += jnp.dot(qr, krt.astype(BF16), preferred_element_type=F32)
        return (s.reshape(N_HEADS, t_dec, nk) * rinv[:, None, :]).reshape(rows, nk)

    def fold(s, cb):
        m = m_scr[...]
        m_new = jnp.maximum(m, jnp.max(s, axis=-1, keepdims=True))
        p = jnp.exp2(s - m_new)
        corr = jnp.exp2(m - m_new)
        l_scr[...] = l_scr[...] * corr + jnp.sum(p, axis=-1, keepdims=True)
        acc_scr[...] = acc_scr[...] * corr + jnp.dot(p.astype(BF16), cb, preferred_element_type=F32)
        m_scr[...] = m_new

    def score_and_fold(cur, prev):
        wait_pages(cur)
        for cp in page_copies(jnp.minimum(g + 1, total - 1), prev) + dispatch_copies():
            cp.start()
        s_prev, cb_prev = s_scr[prev], cb_scr[prev]
        cb = cbuf[cur].reshape(pc * PAGE, cbuf.shape[-1]).astype(BF16)
        krt = jnp.concatenate([kbuf[cur, j] for j in range(pc)], axis=1)
        s_scr[cur] = scores(qa_ref[...].reshape(rows, LANES), qr_ref[...].reshape(rows, ROPE), cb, krt)
        cb_scr[cur] = cb
        fold(s_prev, cb_prev)
        for _ in range(2):
            pltpu.make_async_copy(xn_ref, xs_hbm.at[pl.ds(0, td)], dsem).wait()

    for parity in range(2):
        pl.when(slot == parity)(functools.partial(score_and_fold, parity, 1 - parity))

    @pl.when(g == total)
    def _():
        wait_pages(1 - slot)

    @pl.when((g >= 1) & ((g - 1) % nc == nc - 1))
    def _():
        key = lax.broadcasted_iota(jnp.int32, (rows, PAGE), 1)
        step = lax.broadcasted_iota(jnp.int32, (rows, PAGE), 0) % t_dec
        cb_new = cnew_ref[0].astype(BF16)
        s_new = scores(qa_prev_ref[...].reshape(rows, LANES), qr_prev_ref[...].reshape(rows, ROPE), cb_new, krnew_ref[0])
        fold(jnp.where(key <= step, s_new, -jnp.inf), cb_new)
        o_lat = (acc_scr[...] / l_scr[...]).astype(BF16)
        lane = lax.broadcasted_iota(jnp.int32, (t_dec, LANES), 1)
        for hp in range(N_HEADS // 2):
            lo = jnp.dot(o_lat[(2 * hp) * t_dec:(2 * hp + 1) * t_dec], wv_ref[hp], preferred_element_type=F32)
            hi = jnp.dot(o_lat[(2 * hp + 1) * t_dec:(2 * hp + 2) * t_dec], wv_ref[hp], preferred_element_type=F32)
            o_ref[:, hp * LANES:(hp + 1) * LANES] = jnp.where(lane < V_DIM, lo, hi)


def _attn_sample(page_table, qa, qr, cnew_pad, krnew_pad, wukt_all, wvp, cache_ckv, cache_kr, t_dec, pc,
                 xn, pos1, pos2, tile_end, n_rows, tme):
    nb, npg = page_table.shape
    nc = npg // pc
    total = nb * nc
    kv_rank = cache_ckv.shape[-1]
    nkeys = pc * PAGE
    rows = N_HEADS * t_dec
    n_tok, d = xn.shape
    td = SUBLANES
    while n_tok // td > total + 1:
        td *= 2
    assert n_tok % td == 0
    nds = n_tok // td
    seq_cur = lambda g: jnp.minimum(g // nc, nb - 1)
    seq_prev = lambda g: jnp.maximum(g - 1, 0) // nc
    grid_spec = pltpu.PrefetchScalarGridSpec(
        num_scalar_prefetch=4,
        grid=(total + 1,),
        in_specs=[pl.BlockSpec((N_HEADS, t_dec, LANES), lambda g, *_: (0, seq_cur(g), 0)),
                  pl.BlockSpec((N_HEADS, t_dec, ROPE), lambda g, *_: (0, seq_cur(g), 0)),
                  pl.BlockSpec((N_HEADS, t_dec, LANES), lambda g, *_: (0, seq_prev(g), 0)),
                  pl.BlockSpec((N_HEADS, t_dec, ROPE), lambda g, *_: (0, seq_prev(g), 0)),
                  pl.BlockSpec((1, PAGE, kv_rank), lambda g, *_: (seq_prev(g), 0, 0)),
                  pl.BlockSpec((1, ROPE, PAGE), lambda g, *_: (seq_prev(g), 0, 0)),
                  pl.BlockSpec(wukt_all.shape, lambda g, *_: (0, 0)),
                  pl.BlockSpec(wvp.shape, lambda g, *_: (0, 0, 0)),
                  pl.BlockSpec((td, d), lambda g, *_: (jnp.minimum(g, nds - 1), 0)),
                  pl.BlockSpec(memory_space=pl.ANY), pl.BlockSpec(memory_space=pl.ANY)],
        out_specs=[pl.BlockSpec((t_dec, N_HEADS * V_DIM), lambda g, *_: (seq_prev(g), 0)),
                   pl.BlockSpec(memory_space=pl.ANY)],
        scratch_shapes=[pltpu.VMEM((2, pc, PAGE, kv_rank), F32), pltpu.VMEM((2, pc, ROPE, PAGE), F32),
                        pltpu.SemaphoreType.DMA((2, 2)),
                        pltpu.VMEM((2, rows, nkeys), F32), pltpu.VMEM((2, nkeys, kv_rank), BF16),
                        pltpu.VMEM((rows, 1), F32), pltpu.VMEM((rows, 1), F32), pltpu.VMEM((rows, kv_rank), F32),
                        pltpu.VMEM((tme, d), xn.dtype), pltpu.SemaphoreType.DMA],
    )
    return pl.pallas_call(
        functools.partial(_attn_sample_kernel, pc=pc, nc=nc, total=total, t_dec=t_dec, td=td, nds=nds, tme=tme),
        grid_spec=grid_spec,
        out_shape=[jax.ShapeDtypeStruct((nb * t_dec, N_HEADS * V_DIM), F32), jax.ShapeDtypeStruct((n_rows, d), xn.dtype)],
        compiler_params=_cparams(("arbitrary",)), name="attn_sample",
    )(page_table.reshape(-1), pos1, pos2, tile_end, qa, qr, qa, qr, cnew_pad, krnew_pad, wukt_all, wvp, xn,
      cache_ckv, cache_kr)


def _merge_kernel(x_ref, y_ref, a_ref, wglu_ref, bglu_ref, gs_ref, ga_ref,
                  wo1_ref, wo2_ref, g2_ref, wrh_ref, wrl_ref, br_ref, tri_ref,
                  h_ref, xn_ref, meta_ref, cnt_ref, run_scr):
    @pl.when(pl.program_id(0) == 0)
    def _():
        run_scr[...] = jnp.zeros(run_scr.shape, F32)

    run = run_scr[...]
    for rows in _row_chunks(h_ref.shape[0], tri_ref.shape[0]):
        x = x_ref[rows, :]
        att = a_ref[rows, :]
        y = jnp.concatenate([y_ref[j, rows, :] for j in range(y_ref.shape[0])], axis=1)
        y = jax.nn.gelu(y)
        glu = jnp.dot(y.astype(BF16), wglu_ref[...], preferred_element_type=F32) + bglu_ref[...]
        ssm = y * jax.nn.sigmoid(glu)
        mix = jnp.dot(_rms(ssm, gs_ref[...]).astype(BF16), wo1_ref[...], preferred_element_type=F32)
        mix += jnp.dot(_rms(att, ga_ref[...]).astype(BF16), wo2_ref[...], preferred_element_type=F32)
        h = x + mix
        h_ref[rows, :] = h
        xn = _rms(h, g2_ref[...])
        xn_ref[rows, :] = xn
        xh = xn.astype(BF16)
        xl = (xn - xh.astype(F32)).astype(BF16)
        logits = (jnp.dot(xh, wrh_ref[...], preferred_element_type=F32)
                  + jnp.dot(xl, wrh_ref[...], preferred_element_type=F32)
                  + jnp.dot(xh, wrl_ref[...], preferred_element_type=F32)) + br_ref[...]

        tc = logits.shape[0]
        lane_i = lax.broadcasted_iota(jnp.int32, (tc, LANES), 1)
        lane = lane_i.astype(F32)
        big = float(LANES)
        first = lambda hit: jnp.min(jnp.where(hit, lane, big), axis=-1, keepdims=True)
        gl = jnp.where(lane_i < N_GROUPS, logits, -jnp.inf)
        gmax = jnp.max(gl, axis=-1, keepdims=True)
        grp = first(gl == gmax)
        p_sel = 1.0 / jnp.sum(jnp.exp(gl - gmax), axis=-1, keepdims=True)
        lane_grp = ((lane_i - N_GROUPS) >> 3).astype(F32)
        in_grp = (lane_i >= N_GROUPS) & (lane_i < N_GROUPS + N_EXPERTS) & (lane_grp == grp)
        el = jnp.where(in_grp, logits, -jnp.inf)
        m1 = jnp.max(el, axis=-1, keepdims=True)
        i1 = first(el == m1)
        el2 = jnp.where(lane == i1, -jnp.inf, el)
        m2 = jnp.max(el2, axis=-1, keepdims=True)
        i2 = first(el2 == m2)
        e21 = jnp.exp(m2 - m1)
        w1 = p_sel / (1.0 + e21)
        w2 = p_sel * e21 / (1.0 + e21)
        e1, e2 = i1 - N_GROUPS, i2 - N_GROUPS

        oh1, oh2 = lane == e1, lane == e2
        oh = jnp.where(oh1 | oh2, 1.0, 0.0)
        before = jnp.dot(tri_ref[...], oh.astype(BF16), preferred_element_type=F32) + run
        r1 = jnp.sum(jnp.where(oh1, before, 0.0), axis=-1, keepdims=True)
        r2 = jnp.sum(jnp.where(oh2, before, 0.0), axis=-1, keepdims=True)
        run = run + jnp.sum(oh, axis=0, keepdims=True)
        meta = jnp.zeros((tc, LANES), F32)
        for k, v in enumerate((e1, e2, w1, w2, r1, r2)):
            meta = jnp.where(lane_i == k, v, meta)
        meta_ref[rows, :] = meta

    run_scr[...] = run
    cnt_ref[...] = run


def _merge(x, y, att, wglu, bglu, gs, ga, wo1, wo2, g2, wr, br, tm):
    wrh = wr.astype(BF16)
    wrl = (wr - wrh.astype(F32)).astype(BF16)
    n, d = x.shape
    tri = (lax.broadcasted_iota(jnp.int32, (tm, tm), 0) > lax.broadcasted_iota(jnp.int32, (tm, tm), 1)).astype(BF16)
    row = lambda i: (i, 0)
    const = lambda i: (0, 0)
    w = att.shape[1]
    nsl = y.shape[0]
    full = lambda a: pl.BlockSpec(a.shape, const)
    return pl.pallas_call(
        _merge_kernel,
        grid=(n // tm,),
        in_specs=[pl.BlockSpec((tm, d), row), pl.BlockSpec((nsl, tm, LANES), lambda i: (0, i, 0)),
                  pl.BlockSpec((tm, w), row),
                  full(wglu), full(bglu), full(gs), full(ga), full(wo1), full(wo2), full(g2), full(wrh), full(wrl), full(br),
                  full(tri)],
        out_specs=[pl.BlockSpec((tm, d), row), pl.BlockSpec((tm, d), row), pl.BlockSpec((tm, LANES), row),
                   pl.BlockSpec((1, LANES), const)],
        out_shape=[jax.ShapeDtypeStruct((n, d), F32), jax.ShapeDtypeStruct((n, d), F32),
                   jax.ShapeDtypeStruct((n, LANES), F32), jax.ShapeDtypeStruct((1, LANES), F32)],
        scratch_shapes=[pltpu.VMEM((1, LANES), F32)],
        compiler_params=_cparams(("arbitrary",)), name="merge",
    )(x, y, att, wglu, bglu, gs, ga, wo1, wo2, g2, wrh, wrl, br, tri)


def _start_row_gather(idx_ref, base, src_hbm, dst, sem, nrows):
    for r in range(nrows):
        pltpu.make_async_copy(src_hbm.at[pl.ds(idx_ref[base + r], 1)], dst.at[pl.ds(r, 1)], sem).start()


def _wait_row_gather(src_hbm, dst, sem, nrows):
    pltpu.make_async_copy(src_hbm.at[pl.ds(0, nrows)], dst, sem).wait()


def _pos_kernel(meta_ref, offs_ref, p1_ref, p2_ref):
    meta = meta_ref[...]
    tm = meta.shape[0]
    lane_i = lax.broadcasted_iota(jnp.int32, (tm, LANES), 1)
    lane = lane_i.astype(F32)
    diag = lax.broadcasted_iota(jnp.int32, (tm, LANES), 0) % LANES == lane_i

    def dense(e, r):
        pos = jnp.sum(jnp.where(lane == e, offs_ref[...], 0.0), axis=-1, keepdims=True) + r
        spread = jnp.where(diag, pos, 0.0).reshape(tm // LANES, LANES, LANES)
        return jnp.sum(spread, axis=1).astype(jnp.int32)

    p1_ref[...] = dense(meta[:, 0:1], meta[:, 4:5])
    p2_ref[...] = dense(meta[:, 1:2], meta[:, 5:6])


def _positions(meta, offs_vec, tm):
    n = meta.shape[0]
    rows = tm // LANES
    return pl.pallas_call(
        _pos_kernel,
        grid=(n // tm,),
        in_specs=[pl.BlockSpec((tm, LANES), lambda i: (i, 0)), pl.BlockSpec((1, LANES), lambda i: (0, 0))],
        out_specs=[pl.BlockSpec((rows, LANES), lambda i: (i, 0)), pl.BlockSpec((rows, LANES), lambda i: (i, 0))],
        out_shape=[jax.ShapeDtypeStruct((n // LANES, LANES), jnp.int32)] * 2,
        compiler_params=_cparams(("parallel",)), name="positions",
    )(meta, offs_vec)


def _zero_padding_tiles(tend_ref, xs_hbm, zbuf, sem, tme):
    zbuf[...] = jnp.zeros(zbuf.shape, zbuf.dtype)

    def last_tile(e, carry, wait):
        end = tend_ref[e]
        begin = jnp.where(e == 0, 0, tend_ref[jnp.maximum(e - 1, 0)])

        @pl.when(end > begin)
        def _():
            cp = pltpu.make_async_copy(zbuf, xs_hbm.at[pl.ds((end - 1) * tme, tme)], sem)
            cp.wait() if wait else cp.start()
        return carry

    def spare_tile(t, carry, wait):
        cp = pltpu.make_async_copy(zbuf, xs_hbm.at[pl.ds(t * tme, tme)], sem)
        cp.wait() if wait else cp.start()
        return carry

    n_used, n_tiles = tend_ref[N_EXPERTS - 1], xs_hbm.shape[0] // tme
    for wait in (False, True):
        lax.fori_loop(0, N_EXPERTS, functools.partial(last_tile, wait=wait), 0)
        lax.fori_loop(n_used, n_tiles, functools.partial(spare_tile, wait=wait), 0)


def _dispatch_kernel(p1_ref, p2_ref, tend_ref, xn_ref, xs_hbm, zbuf, sem, *, tmd, tme):
    base = pl.program_id(0) * tmd

    @pl.when(pl.program_id(0) == 0)
    def _():
        _zero_padding_tiles(tend_ref, xs_hbm, zbuf, sem, tme)

    for r in range(tmd):
        src = xn_ref.at[pl.ds(r, 1)]
        pltpu.make_async_copy(src, xs_hbm.at[pl.ds(p1_ref[base + r], 1)], sem).start()
        pltpu.make_async_copy(src, xs_hbm.at[pl.ds(p2_ref[base + r], 1)], sem).start()
    for _ in range(2):
        pltpu.make_async_copy(xn_ref, xs_hbm.at[pl.ds(0, tmd)], sem).wait()


def _dispatch(pos1, pos2, tile_end, xn, n_rows, tmd, tme):
    n, d = xn.shape
    grid_spec = pltpu.PrefetchScalarGridSpec(
        num_scalar_prefetch=3,
        grid=(n // tmd,),
        in_specs=[pl.BlockSpec((tmd, d), lambda i, p1, p2, te: (i, 0))],
        out_specs=pl.BlockSpec(memory_space=pl.ANY),
        scratch_shapes=[pltpu.VMEM((tme, d), xn.dtype), pltpu.SemaphoreType.DMA],
    )
    return pl.pallas_call(
        functools.partial(_dispatch_kernel, tmd=tmd, tme=tme),
        grid_spec=grid_spec,
        out_shape=jax.ShapeDtypeStruct((n_rows, d), xn.dtype),
        compiler_params=_cparams(("arbitrary",)), name="dispatch",
    )(pos1, pos2, tile_end, xn)


def _experts_kernel(te_ref, nu_ref, x_ref, wg_ref, wu_ref, wd_ref, o_ref, wg_scr, wu_scr, wd_scr):
    i = pl.program_id(0)

    @pl.when((i == 0) | (te_ref[i] != te_ref[jnp.maximum(i - 1, 0)]))
    def _():
        wg_scr[...] = wg_ref[0].astype(BF16)
        wu_scr[...] = wu_ref[0].astype(BF16)
        wd_scr[...] = wd_ref[0].astype(BF16)

    @pl.when(i < nu_ref[0])
    def _():
        x = x_ref[...].astype(BF16)
        hg = jnp.dot(x, wg_scr[...], preferred_element_type=F32)
        hu = jnp.dot(x, wu_scr[...], preferred_element_type=F32)
        hh = (jax.nn.silu(hg) * hu).astype(BF16)
        o_ref[...] = jnp.dot(hh, wd_scr[...], preferred_element_type=F32)

    @pl.when(i >= nu_ref[0])
    def _():
        o_ref[...] = jnp.zeros(o_ref.shape, F32)


def _experts(tile_expert, n_used, xs, w_gate, w_up, w_down, tme):
    nt = tile_expert.shape[0]
    ne, d, de = w_gate.shape
    wmap = lambda i, te, nu: (te[i], 0, 0)
    grid_spec = pltpu.PrefetchScalarGridSpec(
        num_scalar_prefetch=2,
        grid=(nt,),
        in_specs=[pl.BlockSpec((tme, d), lambda i, te, nu: (jnp.minimum(i, nu[0] - 1), 0)),
                  pl.BlockSpec((1, d, de), wmap), pl.BlockSpec((1, d, de), wmap), pl.BlockSpec((1, de, d), wmap)],
        out_specs=pl.BlockSpec((tme, d), lambda i, te, nu: (i, 0)),
        scratch_shapes=[pltpu.VMEM((d, de), BF16), pltpu.VMEM((d, de), BF16), pltpu.VMEM((de, d), BF16)],
    )
    return pl.pallas_call(
        _experts_kernel,
        grid_spec=grid_spec,
        out_shape=jax.ShapeDtypeStruct((nt * tme, d), F32),
        compiler_params=_cparams(("arbitrary",)), name="experts",
    )(tile_expert, n_used, xs, w_gate, w_up, w_down)


def _combine_kernel(p1_ref, p2_ref, h_ref, meta_ref, eo_hbm, y_ref, buf, sems, *, tmc, nsteps):
    i = pl.program_id(0)
    slot = i % 2

    def start(step, slot_):
        _start_row_gather(p1_ref, step * tmc, eo_hbm, buf.at[slot_, pl.ds(0, tmc)], sems.at[slot_], tmc)
        _start_row_gather(p2_ref, step * tmc, eo_hbm, buf.at[slot_, pl.ds(tmc, tmc)], sems.at[slot_], tmc)

    @pl.when(i == 0)
    def _():
        start(0, 0)

    @pl.when(i + 1 < nsteps)
    def _():
        start(i + 1, 1 - slot)

    _wait_row_gather(eo_hbm, buf.at[slot], sems.at[slot], 2 * tmc)
    meta = meta_ref[...]
    y_ref[...] = h_ref[...] + meta[:, 2:3] * buf[slot, :tmc, :] + meta[:, 3:4] * buf[slot, tmc:, :]


def _combine(pos1, pos2, h_all, meta_all, eo, row0, tmc):
    n = pos1.shape[0]
    d = h_all.shape[1]
    nsteps = n // tmc
    off = row0 // tmc
    grid_spec = pltpu.PrefetchScalarGridSpec(
        num_scalar_prefetch=2,
        grid=(nsteps,),
        in_specs=[pl.BlockSpec((tmc, d), lambda i, p1, p2: (i + off, 0)),
                  pl.BlockSpec((tmc, LANES), lambda i, p1, p2: (i + off, 0)),
                  pl.BlockSpec(memory_space=pl.ANY)],
        out_specs=pl.BlockSpec((tmc, d), lambda i, p1, p2: (i, 0)),
        scratch_shapes=[pltpu.VMEM((2, 2 * tmc, d), F32), pltpu.SemaphoreType.DMA((2,))],
    )
    return pl.pallas_call(
        functools.partial(_combine_kernel, tmc=tmc, nsteps=nsteps),
        grid_spec=grid_spec,
        out_shape=jax.ShapeDtypeStruct((n, d), F32),
        compiler_params=_cparams(("arbitrary",)), name="combine",
    )(pos1, pos2, h_all, meta_all, eo)


def _pick_tile(n, pref):
    t = min(pref, n)
    while n % t:
        t //= 2
    return t


def _layer(x_p, x_s, cache_ckv, cache_kr, st_re, st_im, page_table, p):
    nb_p, t_p, d = x_p.shape
    nb_s, t_s, _ = x_s.shape
    n_p, n_s = nb_p * t_p, nb_s * t_s
    past_len = page_table.shape[1] * PAGE
    g = p['ssm_a_re'].shape[0]
    ssm_w = g * SSM_CH
    q_rank = p['w_uq'].shape[0]
    kv_rank = p['w_uk'].shape[0]

    w_in = p['w_in']
    o3 = ssm_w + q_rank + kv_rank
    w_pad = jnp.zeros((d, o3 + LANES), F32).at[:, :o3].set(w_in[:, :o3])
    w_pad = w_pad.at[:, o3 + NOPE:o3 + QK].set(w_in[:, o3:]).astype(BF16)
    row2 = lambda v: v.reshape(1, -1).astype(F32)
    padq = lambda v: jnp.pad(v, (0, LANES - QK)).reshape(1, LANES)
    wq = jnp.pad(p['w_uq'].transpose(1, 0, 2), ((0, 0), (0, 0), (0, LANES - QK))).astype(BF16)
    wk = jnp.pad(p['w_uk'].transpose(1, 0, 2), ((0, 0), (0, 0), (0, LANES - NOPE))).astype(BF16)
    wvp = p['w_uv'].reshape(kv_rank, N_HEADS // 2, 2 * V_DIM).transpose(1, 0, 2).astype(BF16)
    wv_h = p['w_uv'].transpose(1, 0, 2)
    wv8 = jnp.stack([jnp.pad(wv_h[h], ((0, 0), (V_DIM - _ones_lane(h), _ones_lane(h)))) for h in range(N_HEADS)])
    wv8 = wv8.astype(BF16)
    wukt_all = p['w_uk'].reshape(kv_rank, N_HEADS * NOPE).T.astype(BF16)
    wukt = jnp.pad(p['w_uk'].transpose(1, 2, 0), ((0, 0), (0, LANES - NOPE), (0, 0))).astype(BF16)
    sel = (jnp.arange(LANES)[:, None] == NOPE + jnp.arange(ROPE)[None, :]).astype(BF16)
    gq, gk = padq(p['qk_norm_q']), padq(p['qk_norm_k'])
    ssm_params = (p['ssm_a_re'], p['ssm_a_im'], p['ssm_log_dt'], p['ssm_b_re'], p['ssm_b_im'],
                  p['ssm_c_re'], p['ssm_c_im'], p['ssm_d'])

    tm_p = _pick_tile(t_p, 512)
    tab_p = _rope_tables(jnp.arange(t_p))
    u, cq, ckv_p, krp_p = _in_proj(x_p.reshape(n_p, d), row2(p['norm1_g']), w_pad, row2(p['q_norm_g']),
                                   row2(p['kv_norm_g']), tab_p, tm_p, ssm_w, q_rank, kv_rank)
    lc_p = _pick_tile(t_p, 8)
    nsl = ssm_w // LANES
    s5w_p = _s5_weights(*ssm_params, lc_p)
    y_ssm_p, hf_p = _s5(u, jnp.zeros((nsl, nb_p, 2 * GPB * SSM_P), F32), s5w_p, nb_p, t_p, lc_p)
    tq = _pick_tile(t_p, 512)
    h1, h2 = NOPE + ROPE // 2, QK
    wq_rot = jnp.concatenate([jnp.zeros_like(wq[..., :NOPE]), -wq[..., h1:h2], wq[..., NOPE:h1],
                              jnp.zeros_like(wq[..., h2:])], axis=-1)
    att_p = _attn_prompt(cq, ckv_p, krp_p, tab_p, jnp.concatenate([wq, wq_rot], axis=-1), wk, wv8, gq, gk,
                         nb_p, t_p, tq, tq)

    w_out = p['w_out'].astype(BF16)
    wr = jnp.zeros((d, LANES), F32).at[:, :N_GROUPS].set(p['w_router_group'])
    wr = wr.at[:, N_GROUPS:N_GROUPS + N_EXPERTS].set(p['w_router_expert'])
    br = jnp.zeros((1, LANES), F32).at[0, :N_GROUPS].set(p['b_router_group'])
    br = br.at[0, N_GROUPS:N_GROUPS + N_EXPERTS].set(p['b_router_expert'])

    def route(x, y_ssm, att, tme):
        n = x.shape[0]
        h, xn, meta, cnt = _merge(
            x, y_ssm, att, p['ssm_w_glu'].astype(BF16), row2(p['ssm_b_glu']), row2(p['out_norm_ssm']),
            row2(p['out_norm_attn']), w_out[:ssm_w], w_out[ssm_w:], row2(p['norm2_g']), wr, br, _pick_tile(n, 512))
        nt = (2 * n) // tme + N_EXPERTS
        counts = cnt[0, :N_EXPERTS].astype(jnp.int32)
        tiles_per = (counts + tme - 1) // tme
        tile_end = jnp.cumsum(tiles_per).astype(jnp.int32)
        tile_start = tile_end - tiles_per
        tile_id = jnp.arange(nt, dtype=jnp.int32)
        tile_expert = jnp.minimum(jnp.sum(tile_end[None, :] <= tile_id[:, None], axis=1), N_EXPERTS - 1).astype(jnp.int32)
        offs_vec = jnp.zeros((1, LANES), F32).at[0, :N_EXPERTS].set((tile_start * tme).astype(F32))
        pos1, pos2 = (a.reshape(-1) for a in _positions(meta, offs_vec, _pick_tile(n, 1024)))
        return dict(h=h, xn=xn, meta=meta, pos1=pos1, pos2=pos2, tile_end=tile_end, tile_expert=tile_expert,
                    n_used=tile_end[-1:], n_rows=nt * tme, tme=tme)

    def moe_tail(r, xs):
        eo = _experts(r['tile_expert'], r['n_used'], xs, p['w_gate'], p['w_up'], p['w_down'], r['tme'])
        return _combine(r['pos1'], r['pos2'], r['h'], r['meta'], eo, 0, _pick_tile(r['h'].shape[0], 256))

    rp = route(x_p.reshape(n_p, d), y_ssm_p, att_p, 512)

    tm_s = _pick_tile(n_s, 512)
    pos_s = past_len + jnp.arange(t_s)
    tab_s = tuple(jnp.tile(a, (tm_s // t_s, 1)) for a in _rope_tables(pos_s))
    u, cq, ckv_s, krp_s = _in_proj(x_s.reshape(n_s, d), row2(p['norm1_g']), w_pad, row2(p['q_norm_g']),
                                   row2(p['kv_norm_g']), tab_s, tm_s, ssm_w, q_rank, kv_rank)
    slab_state = lambda s: s.astype(F32).reshape(nb_s, nsl, GPB * SSM_P).transpose(1, 0, 2)
    h0 = jnp.concatenate([slab_state(st_re), slab_state(st_im)], axis=-1)
    s5w_s = s5w_p if t_s == lc_p else _s5_weights(*ssm_params, t_s)
    y_ssm_s, hf_s = _s5(u, h0, s5w_s, nb_s, t_s, t_s)
    tab_q = tuple(jnp.tile(a, (n_s // tm_s, 1)) for a in tab_s)
    qa, qr = _q_sample(cq, tab_q, wq, wukt, sel, gq, gk)
    kr_s = krp_s[:, NOPE:QK]
    cnew = jnp.pad(ckv_s.reshape(nb_s, t_s, kv_rank), ((0, 0), (0, PAGE - t_s), (0, 0)))
    krnew = jnp.pad(kr_s.reshape(nb_s, t_s, ROPE), ((0, 0), (0, PAGE - t_s), (0, 0))).swapaxes(1, 2)
    pc = _pick_tile(page_table.shape[1], 32)
    att_s, xs_p = _attn_sample(page_table, qa, qr, cnew, krnew, wukt_all, wvp, cache_ckv, jnp.swapaxes(cache_kr, 1, 2),
                               t_s, pc, rp['xn'], rp['pos1'], rp['pos2'], rp['tile_end'], rp['n_rows'], rp['tme'])
    y_p = moe_tail(rp, xs_p)

    rs = route(x_s.reshape(n_s, d), y_ssm_s, att_s, 128)
    xs_s = _dispatch(rs['pos1'], rs['pos2'], rs['tile_end'], rs['xn'], rs['n_rows'], _pick_tile(n_s, 1024), rs['tme'])
    y_s = moe_tail(rs, xs_s)

    n_pg = t_p // PAGE

    def split(hf):
        nb = hf.shape[1]
        half = GPB * SSM_P
        unslab = lambda a: a.reshape(nsl, nb, GPB, SSM_P).transpose(1, 0, 2, 3).reshape(nb, g, SSM_P)
        return unslab(hf[..., :half]), unslab(hf[..., half:])

    hp_re, hp_im = split(hf_p)
    hs_re, hs_im = split(hf_s)
    return (y_p.reshape(nb_p, t_p, d), y_s.reshape(nb_s, t_s, d),
            ckv_p.reshape(nb_p, n_pg, PAGE, kv_rank), krp_p[:, NOPE:QK].reshape(nb_p, n_pg, PAGE, ROPE),
            hp_re, hp_im, ckv_s.reshape(nb_s, t_s, kv_rank), kr_s.reshape(nb_s, t_s, ROPE), hs_re, hs_im)


_PARAM_NAMES = ('norm1_g', 'w_in', 'ssm_a_re', 'ssm_a_im', 'ssm_log_dt', 'ssm_b_re', 'ssm_b_im', 'ssm_c_re',
                'ssm_c_im', 'ssm_d', 'ssm_w_glu', 'ssm_b_glu', 'q_norm_g', 'w_uq', 'kv_norm_g', 'w_uk', 'w_uv',
                'qk_norm_q', 'qk_norm_k', 'out_norm_ssm', 'out_norm_attn', 'w_out', 'norm2_g', 'w_router_group',
                'b_router_group', 'w_router_expert', 'b_router_expert', 'w_gate', 'w_up', 'w_down')


def kernel(x_prompt, x_sample, cache_ckv, cache_krope, state_ssm_re, state_ssm_im, page_table, norm1_g, w_in, ssm_a_re, ssm_a_im, ssm_log_dt, ssm_b_re, ssm_b_im, ssm_c_re, ssm_c_im, ssm_d, ssm_w_glu, ssm_b_glu, q_norm_g, w_uq, kv_norm_g, w_uk, w_uv, qk_norm_q, qk_norm_k, out_norm_ssm, out_norm_attn, w_out, norm2_g, w_router_group, b_router_group, w_router_expert, b_router_expert, w_gate, w_up, w_down):
    params = (norm1_g, w_in, ssm_a_re, ssm_a_im, ssm_log_dt, ssm_b_re, ssm_b_im, ssm_c_re, ssm_c_im, ssm_d,
              ssm_w_glu, ssm_b_glu, q_norm_g, w_uq, kv_norm_g, w_uk, w_uv, qk_norm_q, qk_norm_k, out_norm_ssm,
              out_norm_attn, w_out, norm2_g, w_router_group, b_router_group, w_router_expert, b_router_expert,
              w_gate, w_up, w_down)
    depth = w_in.shape[0]
    h_p, h_s = x_prompt, x_sample
    outs = [[] for _ in range(8)]
    for layer in range(depth):
        p = {k: v[layer] for k, v in zip(_PARAM_NAMES, params)}
        res = _layer(h_p, h_s, cache_ckv[layer], cache_krope[layer], state_ssm_re[layer], state_ssm_im[layer],
                     page_table, p)
        h_p, h_s = res[0], res[1]
        for acc, r in zip(outs, res[2:]):
            acc.append(r)
    return (h_p, h_s) + tuple(jnp.stack(o) for o in outs)
```

```python
import functools
import math

import jax
import jax.numpy as jnp
from jax import lax
from jax.experimental import pallas as pl
from jax.experimental.pallas import tpu as pltpu

F32 = jnp.float32
BF16 = jnp.bfloat16
HIGHEST = lax.Precision.HIGHEST

LANES = 128
SUBLANES = 8
VMEM_LIMIT = 56 * 1024 * 1024

EPS = 1e-6
ROPE_BASE = 10000.0
PAGE = 128
SSM_CH = 16
SSM_P = 64
GPB = LANES // SSM_CH
S5_MAX_TOKENS = 8192
N_HEADS = 8
NOPE = 64
ROPE = 32
QK = NOPE + ROPE
V_DIM = 64
N_GROUPS = 4
EXP_PER_GROUP = 8
N_EXPERTS = N_GROUPS * EXP_PER_GROUP
ATTN_HEADS_PER_BODY = 8

NT_DIMS = (((1,), (1,)), ((), ()))
LOG2E = math.log2(math.e)


def _cparams(sem, vmem=VMEM_LIMIT):
    return pltpu.CompilerParams(dimension_semantics=sem, vmem_limit_bytes=vmem)


def _rms(x, g):
    return x * lax.rsqrt(jnp.mean(x * x, axis=-1, keepdims=True) + EPS) * g


def _rms_qk(x, g):
    return x * lax.rsqrt(jnp.sum(x * x, axis=-1, keepdims=True) * (1.0 / QK) + EPS) * g


def _rope(x, cos, sinp, sinm):
    return x * cos + pltpu.roll(x, 16, 1) * sinp + pltpu.roll(x, LANES - 16, 1) * sinm


def _row_chunks(n, size=256):
    size = min(size, n)
    return [pl.ds(r, size) for r in range(0, n, size)]


def _rope_tables(pos):
    inv_freq = ROPE_BASE ** (-jnp.arange(0, ROPE, 2, dtype=F32) / ROPE)
    ang = pos.astype(F32)[:, None] * inv_freq[None, :]
    c, s = jnp.cos(ang), jnp.sin(ang)
    t = pos.shape[0]
    z = lambda n: jnp.zeros((t, n), F32)
    cosf = jnp.concatenate([jnp.ones((t, NOPE), F32), c, c, z(LANES - QK)], axis=1)
    sinp = jnp.concatenate([z(NOPE + ROPE // 2), s, z(LANES - QK)], axis=1)
    sinm = jnp.concatenate([z(NOPE), -s, z(LANES - QK + ROPE // 2)], axis=1)
    return cosf, sinp, sinm


def _in_proj_kernel(x_ref, g1_ref, w_ref, gq_ref, gkv_ref, cos_ref, sinp_ref, sinm_ref,
                    u_ref, cq_ref, ckv_ref, kr_ref, *, ssm_w, q_rank, kv_rank):
    o1, o2, o3 = ssm_w, ssm_w + q_rank, ssm_w + q_rank + kv_rank
    for rows in _row_chunks(x_ref.shape[0], 128):
        n = _rms(x_ref[rows, :], g1_ref[...])
        z = jnp.dot(n.astype(BF16), w_ref[...], preferred_element_type=F32)
        for j in range(ssm_w // LANES):
            u_ref[j, rows, :] = z[:, j * LANES:(j + 1) * LANES]
        cq_ref[rows, :] = _rms(z[:, o1:o2], gq_ref[...]).astype(cq_ref.dtype)
        ckv_ref[rows, :] = _rms(z[:, o2:o3], gkv_ref[...])
        kr_ref[rows, :] = _rope(z[:, o3:o3 + LANES], cos_ref[rows, :], sinp_ref[rows, :], sinm_ref[rows, :])


def _in_proj(x, g1, w_pad, gq, gkv, tables, tm, ssm_w, q_rank, kv_rank):
    n, d = x.shape
    period = tables[0].shape[0]
    nper = period // tm
    row = lambda i: (i, 0)
    const = lambda i: (0, 0)
    tab = pl.BlockSpec((tm, LANES), lambda i: (i % nper, 0))
    return pl.pallas_call(
        functools.partial(_in_proj_kernel, ssm_w=ssm_w, q_rank=q_rank, kv_rank=kv_rank),
        grid=(n // tm,),
        in_specs=[pl.BlockSpec((tm, d), row), pl.BlockSpec((1, d), const),
                  pl.BlockSpec(w_pad.shape, const), pl.BlockSpec((1, q_rank), const),
                  pl.BlockSpec((1, kv_rank), const), tab, tab, tab],
        out_specs=[pl.BlockSpec((ssm_w // LANES, tm, LANES), lambda i: (0, i, 0)), pl.BlockSpec((tm, q_rank), row),
                   pl.BlockSpec((tm, kv_rank), row), pl.BlockSpec((tm, LANES), row)],
        out_shape=[jax.ShapeDtypeStruct((ssm_w // LANES, n, LANES), F32), jax.ShapeDtypeStruct((n, q_rank), BF16),
                   jax.ShapeDtypeStruct((n, kv_rank), F32), jax.ShapeDtypeStruct((n, LANES), F32)],
        compiler_params=_cparams(("parallel",)), name="in_proj",
    )(x, g1, w_pad, gq, gkv, *tables)


def _s5_weights(a_re, a_im, log_dt, b_re, b_im, c_re, c_im, d, lc):
    g, p = a_re.shape
    dt = jnp.exp(log_dt)[:, None]
    den = a_re * a_re + a_im * a_im
    mag, ang = jnp.exp(a_re * dt), a_im * dt
    lb_re, lb_im = mag * jnp.cos(ang), mag * jnp.sin(ang)
    num_re = lb_re - 1.0
    coef_re = (num_re * a_re + lb_im * a_im) / den
    coef_im = (lb_im * a_re - num_re * a_im) / den
    bb_re = coef_re[..., None] * b_re - coef_im[..., None] * b_im
    bb_im = coef_re[..., None] * b_im + coef_im[..., None] * b_re

    def lam_pow(k):
        kk = k.astype(F32)[:, None, None]
        m = jnp.exp(a_re * dt * kk)
        return m * jnp.cos(ang * kk), m * jnp.sin(ang * kk)

    steps = jnp.arange(lc)
    pr, pi = lam_pow(lc - 1 - steps)
    bz_re = pr[..., None] * bb_re[None] - pi[..., None] * bb_im[None]
    bz_im = pr[..., None] * bb_im[None] + pi[..., None] * bb_re[None]
    bz = jnp.concatenate([bz_re, bz_im], axis=2)
    bz = bz.transpose(1, 0, 3, 2).reshape(g, lc * SSM_CH, 2 * p)
    qr, qi = lam_pow(steps + 1)
    cl_re = c_re[None] * qr[:, :, None, :] - c_im[None] * qi[:, :, None, :]
    cl_im = c_re[None] * qi[:, :, None, :] + c_im[None] * qr[:, :, None, :]
    cz = jnp.concatenate([cl_re, -cl_im], axis=3)
    cz = cz.transpose(1, 3, 0, 2).reshape(g, 2 * p, lc * SSM_CH)
    tr, ti = lam_pow(steps)
    cb_re = jnp.einsum('gcp,kgp,gpd->gkcd', c_re, tr, bb_re, precision=HIGHEST)
    cb_re -= jnp.einsum('gcp,kgp,gpd->gkcd', c_re, ti, bb_im, precision=HIGHEST)
    cb_re -= jnp.einsum('gcp,kgp,gpd->gkcd', c_im, tr, bb_im, precision=HIGHEST)
    cb_re -= jnp.einsum('gcp,kgp,gpd->gkcd', c_im, ti, bb_re, precision=HIGHEST)
    tau = steps[None, :] - steps[:, None]
    ksel = cb_re[:, jnp.clip(tau, 0, lc - 1)]
    ksel = jnp.where((tau >= 0)[None, :, :, None, None], ksel, 0.0)
    tz = ksel.transpose(0, 1, 4, 2, 3).reshape(g, lc * SSM_CH, lc * SSM_CH)
    assert lc * SSM_CH == LANES and 2 * p == LANES
    nsl = g // GPB
    sw = GPB * LANES
    col = jnp.arange(sw)
    rep_state = (jnp.arange(LANES)[:, None] == (col // (sw // 2)) * p + col % p).astype(BF16)
    rep_token = (jnp.arange(LANES)[:, None] == (col // LANES) * SSM_CH + col % SSM_CH).astype(BF16)
    grp_state = (col // p) % GPB
    grp_token = (col // SSM_CH) % GPB

    def slab(rows, rep, row_grp, col_grp):
        full = jnp.einsum('jrk,kc->jrc', rows.astype(BF16), rep, preferred_element_type=F32)
        return jnp.where(row_grp[:, None] == col_grp[None, :], full, 0.0).astype(BF16)

    bz_rows = bz.reshape(nsl, GPB, lc, SSM_CH, LANES).transpose(0, 2, 1, 3, 4).reshape(nsl, sw, LANES)
    tz_rows = tz.reshape(nsl, GPB, lc, SSM_CH, LANES).transpose(0, 2, 1, 3, 4).reshape(nsl, sw, LANES)
    cz_rows = cz.reshape(nsl, GPB, 2, p, LANES).transpose(0, 2, 1, 3, 4).reshape(nsl, sw, LANES)
    bz_s = slab(bz_rows, rep_state, grp_token, grp_state)
    wy = jnp.concatenate([slab(cz_rows, rep_token, grp_state, grp_token),
                          slab(tz_rows, rep_token, grp_token, grp_token)], axis=1)
    lr, li = lam_pow(jnp.array([lc]))
    lam_re = lr[0].reshape(nsl, 1, GPB * p)
    lam_im = li[0].reshape(nsl, 1, GPB * p)
    dvec = jnp.tile(d.reshape(nsl, 1, LANES), (1, 1, lc))
    return bz_s, wy, lam_re, lam_im, dvec


def _s5_kernel(u_ref, bz_ref, wy_ref, lr_ref, li_ref, d_ref, h0_ref, y_ref, hf_ref, s_scr, hp_scr,
               *, nk, bs, lc, rb):
    nh = s_scr.shape[0] // 2
    blk = lambda a, c: a[:, c * LANES:(c + 1) * LANES]

    def ucat(r0):
        return jnp.concatenate([u_ref[0, pl.ds(r0 * lc + t, rb, stride=lc), :] for t in range(lc)], axis=1)

    def scan_rows(r0):
        if nk == 1:
            return pl.ds(r0, rb)
        seq = r0 // nk
        return pl.ds((r0 - seq * nk) * bs + seq, rb, stride=bs)

    def phase1(i, _):
        r0 = pl.multiple_of(i * rb, rb)
        s = jnp.dot(ucat(r0).astype(BF16), bz_ref[0], preferred_element_type=F32)
        for c in range(2 * nh):
            s_scr[c, scan_rows(r0), :] = blk(s, c)
        return 0

    lax.fori_loop(0, (bs * nk) // rb, phase1, 0)
    lr, li = lr_ref[0], li_ref[0]

    def step(k, h):
        rows = pl.ds(pl.multiple_of(k * bs, bs), bs)
        new = [None] * (2 * nh)
        for c in range(nh):
            h_re, h_im = h[c], h[nh + c]
            s_re, s_im = s_scr[c, rows, :], s_scr[nh + c, rows, :]
            hp_scr[c, rows, :] = h_re
            hp_scr[nh + c, rows, :] = h_im
            new[c] = blk(lr, c) * h_re - blk(li, c) * h_im + s_re
            new[nh + c] = blk(lr, c) * h_im + blk(li, c) * h_re + s_im
        return tuple(new)

    h0 = h0_ref[0, 0]
    hf = lax.fori_loop(0, nk, step, tuple(blk(h0, c) for c in range(2 * nh)), unroll=math.gcd(nk, 4))
    hf_ref[0, 0] = jnp.concatenate(hf, axis=1)

    def phase3(i, _):
        r0 = pl.multiple_of(i * rb, rb)
        uc = ucat(r0)
        hprev = jnp.concatenate([hp_scr[c, scan_rows(r0), :] for c in range(2 * nh)], axis=1)
        lhs = jnp.concatenate([hprev.astype(BF16), uc.astype(BF16)], axis=1)
        y = jnp.dot(lhs, wy_ref[0], preferred_element_type=F32) + d_ref[0] * uc
        for t in range(lc):
            y_ref[0, pl.ds(r0 * lc + t, rb, stride=lc), :] = y[:, t * LANES:(t + 1) * LANES]
        return 0

    lax.fori_loop(0, (bs * nk) // rb, phase3, 0)


def _s5(u, h0, weights, nb, t, lc):
    bz, wy, lam_re, lam_im, dvec = weights
    nsl = u.shape[0]
    nk = t // lc
    sw = h0.shape[-1]
    bs = nb
    while bs * t > S5_MAX_TOKENS and bs % 2 == 0:
        bs //= 2
    rb = _pick_tile(bs, 256) if nk == 1 else _pick_tile(nk, 256)
    h0 = h0.reshape(nsl, nb // bs, bs, sw)
    slab = lambda j, b: (j, 0, 0)
    tok = lambda j, b: (j, b, 0)
    st = lambda j, b: (j, b, 0, 0)
    y, hf = pl.pallas_call(
        functools.partial(_s5_kernel, nk=nk, bs=bs, lc=lc, rb=rb),
        grid=(nsl, nb // bs),
        in_specs=[pl.BlockSpec((1, bs * t, LANES), tok), pl.BlockSpec((1,) + bz.shape[1:], slab),
                  pl.BlockSpec((1,) + wy.shape[1:], slab), pl.BlockSpec((1, 1, sw // 2), slab),
                  pl.BlockSpec((1, 1, sw // 2), slab), pl.BlockSpec((1, 1, lc * LANES), slab),
                  pl.BlockSpec((1, 1, bs, sw), st)],
        out_specs=[pl.BlockSpec((1, bs * t, LANES), tok), pl.BlockSpec((1, 1, bs, sw), st)],
        out_shape=[jax.ShapeDtypeStruct(u.shape, F32), jax.ShapeDtypeStruct(h0.shape, F32)],
        scratch_shapes=[pltpu.VMEM((sw // LANES, bs * nk, LANES), F32)] * 2,
        compiler_params=_cparams(("parallel", "parallel")), name="s5",
    )(u, bz, wy, lam_re, lam_im, dvec, h0)
    return y, hf.reshape(nsl, nb, sw)


def _ones_lane(h):
    return V_DIM if h % 2 == 0 else 0


def _attn_prompt_kernel(cq_ref, ckv_ref, krp_ref, cos_ref, sinp_ref, sinm_ref, wq_ref, wk_ref, wv_ref,
                        gq_ref, gk_ref, o_ref, k_scr, v_scr, q_scr, *, tq, tk, hb):
    qi = pl.program_id(1)

    @pl.when(qi == 0)
    def _():
        c = ckv_ref[...].astype(BF16)
        krp = krp_ref[...]
        for h in range(N_HEADS):
            kh = jnp.dot(c, wk_ref[h], preferred_element_type=F32) + krp
            k_scr[h] = _rms_qk(kh, gk_ref[...]).astype(BF16)
        vlane = lax.broadcasted_iota(jnp.int32, (1, LANES), 1)
        for h in range(N_HEADS):
            one = jnp.where(vlane == _ones_lane(h), 1.0, 0.0)
            v_scr[h] = (jnp.dot(c, wv_ref[h], preferred_element_type=F32) + one).astype(BF16)

    cq = cq_ref[...]
    cos, sinp, sinm = cos_ref[...], sinp_ref[...], sinm_ref[...]
    sin = sinp - sinm
    for h in range(N_HEADS):
        qq = jnp.dot(cq, wq_ref[h], preferred_element_type=F32)
        q = qq[:, :LANES] * cos + qq[:, LANES:] * sin
        q_scr[h] = (_rms_qk(q, gq_ref[...]) * (LOG2E / math.sqrt(QK))).astype(BF16)

    assert tq == tk
    causal = lax.broadcasted_iota(jnp.int32, (tq, tk), 1) <= lax.broadcasted_iota(jnp.int32, (tq, tk), 0)
    lane = lax.broadcasted_iota(jnp.int32, (tq, LANES), 1)

    def fold(m, acc, q, h, ks, mask):
        s = lax.dot_general(q, k_scr[h, ks, :], NT_DIMS, preferred_element_type=F32)
        if mask is not None:
            s = jnp.where(mask, s, -jnp.inf)
        m_new = jnp.maximum(m, jnp.max(s, axis=-1, keepdims=True))
        p = jnp.exp2(s - m_new)
        acc = acc * jnp.exp2(m - m_new) + jnp.dot(p.astype(BF16), v_scr[h, ks, :], preferred_element_type=F32)
        return m_new, acc

    def kv_step(j, carry, heads, mask):
        ks = pl.ds(pl.multiple_of(j * tk, tk), tk)
        return tuple(fold(m, acc, q_scr[h], h, ks, mask) for (m, acc), h in zip(carry, heads))

    for h0 in range(0, N_HEADS, hb):
        heads = tuple(range(h0, h0 + hb))
        init = (jnp.full((tq, 1), -jnp.inf, F32), jnp.zeros((tq, LANES), F32))
        carry = lax.fori_loop(0, qi, functools.partial(kv_step, heads=heads, mask=None), (init,) * hb)
        carry = kv_step(qi, carry, heads, causal)
        outs = [acc / acc[:, _ones_lane(h):_ones_lane(h) + 1] for (_, acc), h in zip(carry, heads)]
        for k in range(0, hb, 2):
            hp = (h0 + k) // 2
            o_ref[:, hp * LANES:(hp + 1) * LANES] = jnp.where(lane < V_DIM, outs[k], outs[k + 1])


def _attn_prompt(cq, ckv, krp, tables, wq, wk, wvp, gq, gk, nb, t, tq, tk, hb=ATTN_HEADS_PER_BODY):
    n = cq.shape[0]
    nq = t // tq
    qrow = lambda b, i: (b * nq + i, 0)
    seq = lambda b, i: (b, 0)
    tab = pl.BlockSpec((tq, LANES), lambda b, i: (i, 0))
    c3 = lambda b, i: (0, 0, 0)
    c2 = lambda b, i: (0, 0)
    return pl.pallas_call(
        functools.partial(_attn_prompt_kernel, tq=tq, tk=tk, hb=hb),
        grid=(nb, nq),
        in_specs=[pl.BlockSpec((tq, cq.shape[1]), qrow), pl.BlockSpec((t, ckv.shape[1]), seq),
                  pl.BlockSpec((t, LANES), seq), tab, tab, tab,
                  pl.BlockSpec(wq.shape, c3), pl.BlockSpec(wk.shape, c3), pl.BlockSpec(wvp.shape, c3),
                  pl.BlockSpec((1, LANES), c2), pl.BlockSpec((1, LANES), c2)],
        out_specs=pl.BlockSpec((tq, N_HEADS * V_DIM), qrow),
        out_shape=jax.ShapeDtypeStruct((n, N_HEADS * V_DIM), F32),
        scratch_shapes=[pltpu.VMEM((N_HEADS, t, LANES), BF16), pltpu.VMEM((N_HEADS, t, LANES), BF16),
                        pltpu.VMEM((N_HEADS, tq, LANES), BF16)],
        compiler_params=_cparams(("parallel", "arbitrary")), name="attn_prompt",
    )(cq, ckv, krp, *tables, wq, wk, wvp, gq, gk)


def _q_sample_kernel(cq_ref, cos_ref, sinp_ref, sinm_ref, wq_ref, wukt_ref, sel_ref, gq_ref, gk_ref,
                     qa_ref, qr_ref):
    cq = cq_ref[...]
    for h in range(N_HEADS):
        q = _rope(jnp.dot(cq, wq_ref[h], preferred_element_type=F32), cos_ref[...], sinp_ref[...], sinm_ref[...])
        q = _rms_qk(q, gq_ref[...]) * (LOG2E / math.sqrt(QK))
        qk = (q * gk_ref[...]).astype(BF16)
        qa_ref[h] = jnp.dot(qk, wukt_ref[h], preferred_element_type=F32).astype(BF16)
        qr_ref[h] = jnp.dot(qk, sel_ref[...], preferred_element_type=F32).astype(BF16)


def _q_sample(cq, tables, wq, wukt, sel, gq, gk):
    n = cq.shape[0]
    return pl.pallas_call(
        _q_sample_kernel,
        out_shape=[jax.ShapeDtypeStruct((N_HEADS, n, LANES), BF16), jax.ShapeDtypeStruct((N_HEADS, n, ROPE), BF16)],
        compiler_params=pltpu.CompilerParams(vmem_limit_bytes=VMEM_LIMIT), name="q_sample",
    )(cq, *tables, wq, wukt, sel, gq, gk)


def _attn_sample_kernel(pt_ref, p1_ref, p2_ref, tend_ref, qa_ref, qr_ref, qa_prev_ref, qr_prev_ref, cnew_ref, krnew_ref,
                        wukt_ref, wv_ref, xn_ref, ckv_hbm, kr_hbm, o_ref, xs_hbm,
                        cbuf, kbuf, sems, s_scr, cb_scr, m_scr, l_scr, acc_scr, zbuf, dsem,
                        *, pc, nc, total, t_dec, td, nds, tme):
    g = pl.program_id(0)
    slot = g % 2
    rows = N_HEADS * t_dec

    def dispatch_copies():
        base = jnp.minimum(g, nds - 1) * td
        cps = []
        for r in range(td):
            src = xn_ref.at[pl.ds(r, 1)]
            cps.append(pltpu.make_async_copy(src, xs_hbm.at[pl.ds(p1_ref[base + r], 1)], dsem))
            cps.append(pltpu.make_async_copy(src, xs_hbm.at[pl.ds(p2_ref[base + r], 1)], dsem))
        return cps

    def page_copies(step, slot_):
        base = step * pc
        cps = []
        for j in range(pc):
            pg = pt_ref[base + j]
            cps.append(pltpu.make_async_copy(ckv_hbm.at[pg], cbuf.at[slot_, j], sems.at[0, slot_]))
            cps.append(pltpu.make_async_copy(kr_hbm.at[pg], kbuf.at[slot_, j], sems.at[1, slot_]))
        return cps

    def wait_pages(slot_):
        pltpu.make_async_copy(ckv_hbm.at[pl.ds(0, pc)], cbuf.at[slot_], sems.at[0, slot_]).wait()
        pltpu.make_async_copy(kr_hbm.at[pl.ds(0, pc)], kbuf.at[slot_], sems.at[1, slot_]).wait()

    def reset_softmax():
        m_scr[...] = jnp.full(m_scr.shape, -jnp.inf, F32)
        l_scr[...] = jnp.zeros(l_scr.shape, F32)
        acc_scr[...] = jnp.zeros(acc_scr.shape, F32)

    @pl.when(g == 0)
    def _():
        for cp in page_copies(g, slot):
            cp.start()
        s_scr[1] = jnp.zeros(s_scr.shape[1:], F32)
        cb_scr[1] = jnp.zeros(cb_scr.shape[1:], BF16)
        reset_softmax()
        _zero_padding_tiles(tend_ref, xs_hbm, zbuf, dsem, tme)

    @pl.when((g >= 1) & ((g - 1) % nc == 0))
    def _():
        reset_softmax()

    def scores(qa, qr, cb, krt):
        nk = cb.shape[0]
        knt = lax.dot_general(wukt_ref[...], cb, NT_DIMS, preferred_element_type=F32)
        n2 = jnp.sum((knt * knt).reshape(N_HEADS, NOPE, nk), axis=1)
        kr2 = jnp.sum(krt * krt, axis=0, keepdims=True)
        rinv = lax.rsqrt((n2 + kr2) * (1.0 / QK) + EPS)
        s = lax.dot_general(qa, cb, NT_DIMS, preferred_element_type=F32)
        s += jnp.dot(qr, krt.astype(BF16), preferred_element_type=F32)
        return (s.reshape(N_HEADS, t_dec, nk) * rinv[:, None, :]).reshape(rows, nk)

    def fold(s, cb):
        m = m_scr[...]
        m_new = jnp.maximum(m, jnp.max(s, axis=-1, keepdims=True))
        p = jnp.exp2(s - m_new)
        corr = jnp.exp2(m - m_new)
        l_scr[...] = l_scr[...] * corr + jnp.sum(p, axis=-1, keepdims=True)
        acc_scr[...] = acc_scr[...] * corr + jnp.dot(p.astype(BF16), cb, preferred_element_type=F32)
        m_scr[...] = m_new

    def score_and_fold(cur, prev):
        wait_pages(cur)
        for cp in dispatch_copies() + page_copies(jnp.minimum(g + 1, total - 1), prev):
            cp.start()
        s_prev, cb_prev = s_scr[prev], cb_scr[prev]
        cb = cbuf[cur].reshape(pc * PAGE, cbuf.shape[-1]).astype(BF16)
        krt = jnp.concatenate([kbuf[cur, j] for j in range(pc)], axis=1)
        s_scr[cur] = scores(qa_ref[...].reshape(rows, LANES), qr_ref[...].reshape(rows, ROPE), cb, krt)
        cb_scr[cur] = cb
        fold(s_prev, cb_prev)
        for _ in range(2):
            pltpu.make_async_copy(xn_ref, xs_hbm.at[pl.ds(0, td)], dsem).wait()

    for parity in range(2):
        pl.when(slot == parity)(functools.partial(score_and_fold, parity, 1 - parity))

    @pl.when(g == total)
    def _():
        wait_pages(1 - slot)

    @pl.when((g >= 1) & ((g - 1) % nc == nc - 1))
    def _():
        key = lax.broadcasted_iota(jnp.int32, (rows, PAGE), 1)
        step = lax.broadcasted_iota(jnp.int32, (rows, PAGE), 0) % t_dec
        cb_new = cnew_ref[0].astype(BF16)
        s_new = scores(qa_prev_ref[...].reshape(rows, LANES), qr_prev_ref[...].reshape(rows, ROPE), cb_new, krnew_ref[0])
        fold(jnp.where(key <= step, s_new, -jnp.inf), cb_new)
        o_lat = (acc_scr[...] / l_scr[...]).astype(BF16)
        lane = lax.broadcasted_iota(jnp.int32, (t_dec, LANES), 1)
        for hp in range(N_HEADS // 2):
            lo = jnp.dot(o_lat[(2 * hp) * t_dec:(2 * hp + 1) * t_dec], wv_ref[hp], preferred_element_type=F32)
            hi = jnp.dot(o_lat[(2 * hp + 1) * t_dec:(2 * hp + 2) * t_dec], wv_ref[hp], preferred_element_type=F32)
            o_ref[:, hp * LANES:(hp + 1) * LANES] = jnp.where(lane < V_DIM, lo, hi)


def _attn_sample(page_table, qa, qr, cnew_pad, krnew_pad, wukt_all, wvp, cache_ckv, cache_kr, t_dec, pc,
                 xn, pos1, pos2, tile_end, n_rows, tme):
    nb, npg = page_table.shape
    nc = npg // pc
    total = nb * nc
    kv_rank = cache_ckv.shape[-1]
    nkeys = pc * PAGE
    rows = N_HEADS * t_dec
    n_tok, d = xn.shape
    td = SUBLANES
    while n_tok // td > total + 1:
        td *= 2
    assert n_tok % td == 0
    nds = n_tok // td
    seq_cur = lambda g: jnp.minimum(g // nc, nb - 1)
    seq_prev = lambda g: jnp.maximum(g - 1, 0) // nc
    grid_spec = pltpu.PrefetchScalarGridSpec(
        num_scalar_prefetch=4,
        grid=(total + 1,),
        in_specs=[pl.BlockSpec((N_HEADS, t_dec, LANES), lambda g, *_: (0, seq_cur(g), 0)),
                  pl.BlockSpec((N_HEADS, t_dec, ROPE), lambda g, *_: (0, seq_cur(g), 0)),
                  pl.BlockSpec((N_HEADS, t_dec, LANES), lambda g, *_: (0, seq_prev(g), 0)),
                  pl.BlockSpec((N_HEADS, t_dec, ROPE), lambda g, *_: (0, seq_prev(g), 0)),
                  pl.BlockSpec((1, PAGE, kv_rank), lambda g, *_: (seq_prev(g), 0, 0)),
                  pl.BlockSpec((1, ROPE, PAGE), lambda g, *_: (seq_prev(g), 0, 0)),
                  pl.BlockSpec(wukt_all.shape, lambda g, *_: (0, 0)),
                  pl.BlockSpec(wvp.shape, lambda g, *_: (0, 0, 0)),
                  pl.BlockSpec((td, d), lambda g, *_: (jnp.minimum(g, nds - 1), 0)),
                  pl.BlockSpec(memory_space=pl.ANY), pl.BlockSpec(memory_space=pl.ANY)],
        out_specs=[pl.BlockSpec((t_dec, N_HEADS * V_DIM), lambda g, *_: (seq_prev(g), 0)),
                   pl.BlockSpec(memory_space=pl.ANY)],
        scratch_shapes=[pltpu.VMEM((2, pc, PAGE, kv_rank), F32), pltpu.VMEM((2, pc, ROPE, PAGE), F32),
                        pltpu.SemaphoreType.DMA((2, 2)),
                        pltpu.VMEM((2, rows, nkeys), F32), pltpu.VMEM((2, nkeys, kv_rank), BF16),
                        pltpu.VMEM((rows, 1), F32), pltpu.VMEM((rows, 1), F32), pltpu.VMEM((rows, kv_rank), F32),
                        pltpu.VMEM((tme, d), xn.dtype), pltpu.SemaphoreType.DMA],
    )
    return pl.pallas_call(
        functools.partial(_attn_sample_kernel, pc=pc, nc=nc, total=total, t_dec=t_dec, td=td, nds=nds, tme=tme),
        grid_spec=grid_spec,
        out_shape=[jax.ShapeDtypeStruct((nb * t_dec, N_HEADS * V_DIM), F32), jax.ShapeDtypeStruct((n_rows, d), xn.dtype)],
        compiler_params=_cparams(("arbitrary",)), name="attn_sample",
    )(page_table.reshape(-1), pos1, pos2, tile_end, qa, qr, qa, qr, cnew_pad, krnew_pad, wukt_all, wvp, xn,
      cache_ckv, cache_kr)


def _merge_kernel(x_ref, y_ref, a_ref, wglu_ref, bglu_ref, gs_ref, ga_ref,
                  wo1_ref, wo2_ref, g2_ref, wrh_ref, wrl_ref, br_ref, tri_ref,
                  h_ref, xn_ref, meta_ref, cnt_ref, run_scr):
    @pl.when(pl.program_id(0) == 0)
    def _():
        run_scr[...] = jnp.zeros(run_scr.shape, F32)

    run = run_scr[...]
    for rows in _row_chunks(h_ref.shape[0], tri_ref.shape[0]):
        x = x_ref[rows, :]
        att = a_ref[rows, :]
        y = jnp.concatenate([y_ref[j, rows, :] for j in range(y_ref.shape[0])], axis=1)
        y = jax.nn.gelu(y)
        glu = jnp.dot(y.astype(BF16), wglu_ref[...], preferred_element_type=F32) + bglu_ref[...]
        ssm = y * jax.nn.sigmoid(glu)
        mix = jnp.dot(_rms(ssm, gs_ref[...]).astype(BF16), wo1_ref[...], preferred_element_type=F32)
        mix += jnp.dot(_rms(att, ga_ref[...]).astype(BF16), wo2_ref[...], preferred_element_type=F32)
        h = x + mix
        h_ref[rows, :] = h
        xn = _rms(h, g2_ref[...])
        xn_ref[rows, :] = xn
        xh = xn.astype(BF16)
        xl = (xn - xh.astype(F32)).astype(BF16)
        logits = (jnp.dot(xh, wrh_ref[...], preferred_element_type=F32)
                  + jnp.dot(xl, wrh_ref[...], preferred_element_type=F32)
                  + jnp.dot(xh, wrl_ref[...], preferred_element_type=F32)) + br_ref[...]

        tc = logits.shape[0]
        lane_i = lax.broadcasted_iota(jnp.int32, (tc, LANES), 1)
        lane = lane_i.astype(F32)
        big = float(LANES)
        first = lambda hit: jnp.min(jnp.where(hit, lane, big), axis=-1, keepdims=True)
        gl = jnp.where(lane_i < N_GROUPS, logits, -jnp.inf)
        gmax = jnp.max(gl, axis=-1, keepdims=True)
        grp = first(gl == gmax)
        p_sel = 1.0 / jnp.sum(jnp.exp(gl - gmax), axis=-1, keepdims=True)
        lane_grp = ((lane_i - N_GROUPS) >> 3).astype(F32)
        in_grp = (lane_i >= N_GROUPS) & (lane_i < N_GROUPS + N_EXPERTS) & (lane_grp == grp)
        el = jnp.where(in_grp, logits, -jnp.inf)
        m1 = jnp.max(el, axis=-1, keepdims=True)
        i1 = first(el == m1)
        el2 = jnp.where(lane == i1, -jnp.inf, el)
        m2 = jnp.max(el2, axis=-1, keepdims=True)
        i2 = first(el2 == m2)
        e21 = jnp.exp(m2 - m1)
        w1 = p_sel / (1.0 + e21)
        w2 = p_sel * e21 / (1.0 + e21)
        e1, e2 = i1 - N_GROUPS, i2 - N_GROUPS

        oh1, oh2 = lane == e1, lane == e2
        oh = jnp.where(oh1 | oh2, 1.0, 0.0)
        before = jnp.dot(tri_ref[...], oh.astype(BF16), preferred_element_type=F32) + run
        r1 = jnp.sum(jnp.where(oh1, before, 0.0), axis=-1, keepdims=True)
        r2 = jnp.sum(jnp.where(oh2, before, 0.0), axis=-1, keepdims=True)
        run = run + jnp.sum(oh, axis=0, keepdims=True)
        meta = jnp.zeros((tc, LANES), F32)
        for k, v in enumerate((e1, e2, w1, w2, r1, r2)):
            meta = jnp.where(lane_i == k, v, meta)
        meta_ref[rows, :] = meta

    run_scr[...] = run
    cnt_ref[...] = run


def _merge(x, y, att, wglu, bglu, gs, ga, wo1, wo2, g2, wr, br, tm):
    wrh = wr.astype(BF16)
    wrl = (wr - wrh.astype(F32)).astype(BF16)
    n, d = x.shape
    tri = (lax.broadcasted_iota(jnp.int32, (tm, tm), 0) > lax.broadcasted_iota(jnp.int32, (tm, tm), 1)).astype(BF16)
    row = lambda i: (i, 0)
    const = lambda i: (0, 0)
    w = att.shape[1]
    nsl = y.shape[0]
    full = lambda a: pl.BlockSpec(a.shape, const)
    return pl.pallas_call(
        _merge_kernel,
        grid=(n // tm,),
        in_specs=[pl.BlockSpec((tm, d), row), pl.BlockSpec((nsl, tm, LANES), lambda i: (0, i, 0)),
                  pl.BlockSpec((tm, w), row),
                  full(wglu), full(bglu), full(gs), full(ga), full(wo1), full(wo2), full(g2), full(wrh), full(wrl), full(br),
                  full(tri)],
        out_specs=[pl.BlockSpec((tm, d), row), pl.BlockSpec((tm, d), row), pl.BlockSpec((tm, LANES), row),
                   pl.BlockSpec((1, LANES), const)],
        out_shape=[jax.ShapeDtypeStruct((n, d), F32), jax.ShapeDtypeStruct((n, d), F32),
                   jax.ShapeDtypeStruct((n, LANES), F32), jax.ShapeDtypeStruct((1, LANES), F32)],
        scratch_shapes=[pltpu.VMEM((1, LANES), F32)],
        compiler_params=_cparams(("arbitrary",)), name="merge",
    )(x, y, att, wglu, bglu, gs, ga, wo1, wo2, g2, wrh, wrl, br, tri)


def _start_row_gather(idx_ref, base, src_hbm, dst, sem, nrows):
    for r in range(nrows):
        pltpu.make_async_copy(src_hbm.at[pl.ds(idx_ref[base + r], 1)], dst.at[pl.ds(r, 1)], sem).start()


def _wait_row_gather(src_hbm, dst, sem, nrows):
    pltpu.make_async_copy(src_hbm.at[pl.ds(0, nrows)], dst, sem).wait()


def _pos_kernel(meta_ref, offs_ref, p1_ref, p2_ref):
    meta = meta_ref[...]
    tm = meta.shape[0]
    lane_i = lax.broadcasted_iota(jnp.int32, (tm, LANES), 1)
    lane = lane_i.astype(F32)
    diag = lax.broadcasted_iota(jnp.int32, (tm, LANES), 0) % LANES == lane_i

    def dense(e, r):
        pos = jnp.sum(jnp.where(lane == e, offs_ref[...], 0.0), axis=-1, keepdims=True) + r
        spread = jnp.where(diag, pos, 0.0).reshape(tm // LANES, LANES, LANES)
        return jnp.sum(spread, axis=1).astype(jnp.int32)

    p1_ref[...] = dense(meta[:, 0:1], meta[:, 4:5])
    p2_ref[...] = dense(meta[:, 1:2], meta[:, 5:6])


def _positions(meta, offs_vec, tm):
    n = meta.shape[0]
    rows = tm // LANES
    return pl.pallas_call(
        _pos_kernel,
        grid=(n // tm,),
        in_specs=[pl.BlockSpec((tm, LANES), lambda i: (i, 0)), pl.BlockSpec((1, LANES), lambda i: (0, 0))],
        out_specs=[pl.BlockSpec((rows, LANES), lambda i: (i, 0)), pl.BlockSpec((rows, LANES), lambda i: (i, 0))],
        out_shape=[jax.ShapeDtypeStruct((n // LANES, LANES), jnp.int32)] * 2,
        compiler_params=_cparams(("parallel",)), name="positions",
    )(meta, offs_vec)


def _zero_padding_tiles(tend_ref, xs_hbm, zbuf, sem, tme):
    zbuf[...] = jnp.zeros(zbuf.shape, zbuf.dtype)

    def last_tile(e, carry, wait):
        end = tend_ref[e]
        begin = jnp.where(e == 0, 0, tend_ref[jnp.maximum(e - 1, 0)])

        @pl.when(end > begin)
        def _():
            cp = pltpu.make_async_copy(zbuf, xs_hbm.at[pl.ds((end - 1) * tme, tme)], sem)
            cp.wait() if wait else cp.start()
        return carry

    def spare_tile(t, carry, wait):
        cp = pltpu.make_async_copy(zbuf, xs_hbm.at[pl.ds(t * tme, tme)], sem)
        cp.wait() if wait else cp.start()
        return carry

    n_used, n_tiles = tend_ref[N_EXPERTS - 1], xs_hbm.shape[0] // tme
    for wait in (False, True):
        lax.fori_loop(0, N_EXPERTS, functools.partial(last_tile, wait=wait), 0)
        lax.fori_loop(n_used, n_tiles, functools.partial(spare_tile, wait=wait), 0)


def _dispatch_kernel(p1_ref, p2_ref, tend_ref, xn_ref, xs_hbm, zbuf, sem, *, tmd, tme):
    base = pl.program_id(0) * tmd

    @pl.when(pl.program_id(0) == 0)
    def _():
        _zero_padding_tiles(tend_ref, xs_hbm, zbuf, sem, tme)

    for r in range(tmd):
        src = xn_ref.at[pl.ds(r, 1)]
        pltpu.make_async_copy(src, xs_hbm.at[pl.ds(p1_ref[base + r], 1)], sem).start()
        pltpu.make_async_copy(src, xs_hbm.at[pl.ds(p2_ref[base + r], 1)], sem).start()
    for _ in range(2):
        pltpu.make_async_copy(xn_ref, xs_hbm.at[pl.ds(0, tmd)], sem).wait()


def _dispatch(pos1, pos2, tile_end, xn, n_rows, tmd, tme):
    n, d = xn.shape
    grid_spec = pltpu.PrefetchScalarGridSpec(
        num_scalar_prefetch=3,
        grid=(n // tmd,),
        in_specs=[pl.BlockSpec((tmd, d), lambda i, p1, p2, te: (i, 0))],
        out_specs=pl.BlockSpec(memory_space=pl.ANY),
        scratch_shapes=[pltpu.VMEM((tme, d), xn.dtype), pltpu.SemaphoreType.DMA],
    )
    return pl.pallas_call(
        functools.partial(_dispatch_kernel, tmd=tmd, tme=tme),
        grid_spec=grid_spec,
        out_shape=jax.ShapeDtypeStruct((n_rows, d), xn.dtype),
        compiler_params=_cparams(("arbitrary",)), name="dispatch",
    )(pos1, pos2, tile_end, xn)


def _experts_kernel(te_ref, nu_ref, x_ref, wg_ref, wu_ref, wd_ref, o_ref, wg_scr, wu_scr, wd_scr):
    i = pl.program_id(0)

    @pl.when((i == 0) | (te_ref[i] != te_ref[jnp.maximum(i - 1, 0)]))
    def _():
        wg_scr[...] = wg_ref[0].astype(BF16)
        wu_scr[...] = wu_ref[0].astype(BF16)
        wd_scr[...] = wd_ref[0].astype(BF16)

    @pl.when(i < nu_ref[0])
    def _():
        x = x_ref[...].astype(BF16)
        hg = jnp.dot(x, wg_scr[...], preferred_element_type=F32)
        hu = jnp.dot(x, wu_scr[...], preferred_element_type=F32)
        hh = (jax.nn.silu(hg) * hu).astype(BF16)
        o_ref[...] = jnp.dot(hh, wd_scr[...], preferred_element_type=F32)

    @pl.when(i >= nu_ref[0])
    def _():
        o_ref[...] = jnp.zeros(o_ref.shape, F32)


def _experts(tile_expert, n_used, xs, w_gate, w_up, w_down, tme):
    nt = tile_expert.shape[0]
    ne, d, de = w_gate.shape
    wmap = lambda i, te, nu: (te[i], 0, 0)
    grid_spec = pltpu.PrefetchScalarGridSpec(
        num_scalar_prefetch=2,
        grid=(nt,),
        in_specs=[pl.BlockSpec((tme, d), lambda i, te, nu: (jnp.minimum(i, nu[0] - 1), 0)),
                  pl.BlockSpec((1, d, de), wmap), pl.BlockSpec((1, d, de), wmap), pl.BlockSpec((1, de, d), wmap)],
        out_specs=pl.BlockSpec((tme, d), lambda i, te, nu: (i, 0)),
        scratch_shapes=[pltpu.VMEM((d, de), BF16), pltpu.VMEM((d, de), BF16), pltpu.VMEM((de, d), BF16)],
    )
    return pl.pallas_call(
        _experts_kernel,
        grid_spec=grid_spec,
        out_shape=jax.ShapeDtypeStruct((nt * tme, d), F32),
        compiler_params=_cparams(("arbitrary",)), name="experts",
    )(tile_expert, n_used, xs, w_gate, w_up, w_down)


def _combine_kernel(p1_ref, p2_ref, h_ref, meta_ref, eo_hbm, y_ref, buf, sems, *, tmc, nsteps):
    i = pl.program_id(0)
    slot = i % 2

    def start(step, slot_):
        _start_row_gather(p1_ref, step * tmc, eo_hbm, buf.at[slot_, pl.ds(0, tmc)], sems.at[slot_], tmc)
        _start_row_gather(p2_ref, step * tmc, eo_hbm, buf.at[slot_, pl.ds(tmc, tmc)], sems.at[slot_], tmc)

    @pl.when(i == 0)
    def _():
        start(0, 0)

    @pl.when(i + 1 < nsteps)
    def _():
        start(i + 1, 1 - slot)

    _wait_row_gather(eo_hbm, buf.at[slot], sems.at[slot], 2 * tmc)
    meta = meta_ref[...]
    y_ref[...] = h_ref[...] + meta[:, 2:3] * buf[slot, :tmc, :] + meta[:, 3:4] * buf[slot, tmc:, :]


def _combine(pos1, pos2, h_all, meta_all, eo, row0, tmc):
    n = pos1.shape[0]
    d = h_all.shape[1]
    nsteps = n // tmc
    off = row0 // tmc
    grid_spec = pltpu.PrefetchScalarGridSpec(
        num_scalar_prefetch=2,
        grid=(nsteps,),
        in_specs=[pl.BlockSpec((tmc, d), lambda i, p1, p2: (i + off, 0)),
                  pl.BlockSpec((tmc, LANES), lambda i, p1, p2: (i + off, 0)),
                  pl.BlockSpec(memory_space=pl.ANY)],
        out_specs=pl.BlockSpec((tmc, d), lambda i, p1, p2: (i, 0)),
        scratch_shapes=[pltpu.VMEM((2, 2 * tmc, d), F32), pltpu.SemaphoreType.DMA((2,))],
    )
    return pl.pallas_call(
        functools.partial(_combine_kernel, tmc=tmc, nsteps=nsteps),
        grid_spec=grid_spec,
        out_shape=jax.ShapeDtypeStruct((n, d), F32),
        compiler_params=_cparams(("arbitrary",)), name="combine",
    )(pos1, pos2, h_all, meta_all, eo)


def _pick_tile(n, pref):
    t = min(pref, n)
    while n % t:
        t //= 2
    return t


def _layer(x_p, x_s, cache_ckv, cache_kr, st_re, st_im, page_table, p):
    nb_p, t_p, d = x_p.shape
    nb_s, t_s, _ = x_s.shape
    n_p, n_s = nb_p * t_p, nb_s * t_s
    past_len = page_table.shape[1] * PAGE
    g = p['ssm_a_re'].shape[0]
    ssm_w = g * SSM_CH
    q_rank = p['w_uq'].shape[0]
    kv_rank = p['w_uk'].shape[0]

    w_in = p['w_in']
    o3 = ssm_w + q_rank + kv_rank
    w_pad = jnp.zeros((d, o3 + LANES), F32).at[:, :o3].set(w_in[:, :o3])
    w_pad = w_pad.at[:, o3 + NOPE:o3 + QK].set(w_in[:, o3:]).astype(BF16)
    row2 = lambda v: v.reshape(1, -1).astype(F32)
    padq = lambda v: jnp.pad(v, (0, LANES - QK)).reshape(1, LANES)
    wq = jnp.pad(p['w_uq'].transpose(1, 0, 2), ((0, 0), (0, 0), (0, LANES - QK))).astype(BF16)
    wk = jnp.pad(p['w_uk'].transpose(1, 0, 2), ((0, 0), (0, 0), (0, LANES - NOPE))).astype(BF16)
    wvp = p['w_uv'].reshape(kv_rank, N_HEADS // 2, 2 * V_DIM).transpose(1, 0, 2).astype(BF16)
    wv_h = p['w_uv'].transpose(1, 0, 2)
    wv8 = jnp.stack([jnp.pad(wv_h[h], ((0, 0), (V_DIM - _ones_lane(h), _ones_lane(h)))) for h in range(N_HEADS)])
    wv8 = wv8.astype(BF16)
    wukt_all = p['w_uk'].reshape(kv_rank, N_HEADS * NOPE).T.astype(BF16)
    wukt = jnp.pad(p['w_uk'].transpose(1, 2, 0), ((0, 0), (0, LANES - NOPE), (0, 0))).astype(BF16)
    sel = (jnp.arange(LANES)[:, None] == NOPE + jnp.arange(ROPE)[None, :]).astype(BF16)
    gq, gk = padq(p['qk_norm_q']), padq(p['qk_norm_k'])
    ssm_params = (p['ssm_a_re'], p['ssm_a_im'], p['ssm_log_dt'], p['ssm_b_re'], p['ssm_b_im'],
                  p['ssm_c_re'], p['ssm_c_im'], p['ssm_d'])

    tm_p = _pick_tile(t_p, 512)
    tab_p = _rope_tables(jnp.arange(t_p))
    u, cq, ckv_p, krp_p = _in_proj(x_p.reshape(n_p, d), row2(p['norm1_g']), w_pad, row2(p['q_norm_g']),
                                   row2(p['kv_norm_g']), tab_p, tm_p, ssm_w, q_rank, kv_rank)
    lc_p = _pick_tile(t_p, 8)
    nsl = ssm_w // LANES
    s5w_p = _s5_weights(*ssm_params, lc_p)
    y_ssm_p, hf_p = _s5(u, jnp.zeros((nsl, nb_p, 2 * GPB * SSM_P), F32), s5w_p, nb_p, t_p, lc_p)
    tq = _pick_tile(t_p, 512)
    h1, h2 = NOPE + ROPE // 2, QK
    wq_rot = jnp.concatenate([jnp.zeros_like(wq[..., :NOPE]), -wq[..., h1:h2], wq[..., NOPE:h1],
                              jnp.zeros_like(wq[..., h2:])], axis=-1)
    att_p = _attn_prompt(cq, ckv_p, krp_p, tab_p, jnp.concatenate([wq, wq_rot], axis=-1), wk, wv8, gq, gk,
                         nb_p, t_p, tq, tq)

    w_out = p['w_out'].astype(BF16)
    wr = jnp.zeros((d, LANES), F32).at[:, :N_GROUPS].set(p['w_router_group'])
    wr = wr.at[:, N_GROUPS:N_GROUPS + N_EXPERTS].set(p['w_router_expert'])
    br = jnp.zeros((1, LANES), F32).at[0, :N_GROUPS].set(p['b_router_group'])
    br = br.at[0, N_GROUPS:N_GROUPS + N_EXPERTS].set(p['b_router_expert'])

    def route(x, y_ssm, att, tme):
        n = x.shape[0]
        h, xn, meta, cnt = _merge(
            x, y_ssm, att, p['ssm_w_glu'].astype(BF16), row2(p['ssm_b_glu']), row2(p['out_norm_ssm']),
            row2(p['out_norm_attn']), w_out[:ssm_w], w_out[ssm_w:], row2(p['norm2_g']), wr, br, _pick_tile(n, 512))
        nt = (2 * n) // tme + N_EXPERTS
        counts = cnt[0, :N_EXPERTS].astype(jnp.int32)
        tiles_per = (counts + tme - 1) // tme
        tile_end = jnp.cumsum(tiles_per).astype(jnp.int32)
        tile_start = tile_end - tiles_per
        tile_id = jnp.arange(nt, dtype=jnp.int32)
        tile_expert = jnp.minimum(jnp.sum(tile_end[None, :] <= tile_id[:, None], axis=1), N_EXPERTS - 1).astype(jnp.int32)
        offs_vec = jnp.zeros((1, LANES), F32).at[0, :N_EXPERTS].set((tile_start * tme).astype(F32))
        pos1, pos2 = (a.reshape(-1) for a in _positions(meta, offs_vec, _pick_tile(n, 1024)))
        return dict(h=h, xn=xn, meta=meta, pos1=pos1, pos2=pos2, tile_end=tile_end, tile_expert=tile_expert,
                    n_used=tile_end[-1:], n_rows=nt * tme, tme=tme)

    def moe_tail(r, xs):
        eo = _experts(r['tile_expert'], r['n_used'], xs, p['w_gate'], p['w_up'], p['w_down'], r['tme'])
        return _combine(r['pos1'], r['pos2'], r['h'], r['meta'], eo, 0, _pick_tile(r['h'].shape[0], 256))

    rp = route(x_p.reshape(n_p, d), y_ssm_p, att_p, 512)

    tm_s = _pick_tile(n_s, 512)
    pos_s = past_len + jnp.arange(t_s)
    tab_s = tuple(jnp.tile(a, (tm_s // t_s, 1)) for a in _rope_tables(pos_s))
    u, cq, ckv_s, krp_s = _in_proj(x_s.reshape(n_s, d), row2(p['norm1_g']), w_pad, row2(p['q_norm_g']),
                                   row2(p['kv_norm_g']), tab_s, tm_s, ssm_w, q_rank, kv_rank)
    slab_state = lambda s: s.astype(F32).reshape(nb_s, nsl, GPB * SSM_P).transpose(1, 0, 2)
    h0 = jnp.concatenate([slab_state(st_re), slab_state(st_im)], axis=-1)
    s5w_s = s5w_p if t_s == lc_p else _s5_weights(*ssm_params, t_s)
    y_ssm_s, hf_s = _s5(u, h0, s5w_s, nb_s, t_s, t_s)
    tab_q = tuple(jnp.tile(a, (n_s // tm_s, 1)) for a in tab_s)
    qa, qr = _q_sample(cq, tab_q, wq, wukt, sel, gq, gk)
    kr_s = krp_s[:, NOPE:QK]
    cnew = jnp.pad(ckv_s.reshape(nb_s, t_s, kv_rank), ((0, 0), (0, PAGE - t_s), (0, 0)))
    krnew = jnp.pad(kr_s.reshape(nb_s, t_s, ROPE), ((0, 0), (0, PAGE - t_s), (0, 0))).swapaxes(1, 2)
    pc = _pick_tile(page_table.shape[1], 32)
    att_s, xs_p = _attn_sample(page_table, qa, qr, cnew, krnew, wukt_all, wvp, cache_ckv, jnp.swapaxes(cache_kr, 1, 2),
                               t_s, pc, rp['xn'], rp['pos1'], rp['pos2'], rp['tile_end'], rp['n_rows'], rp['tme'])
    y_p = moe_tail(rp, xs_p)

    rs = route(x_s.reshape(n_s, d), y_ssm_s, att_s, 128)
    xs_s = _dispatch(rs['pos1'], rs['pos2'], rs['tile_end'], rs['xn'], rs['n_rows'], _pick_tile(n_s, 1024), rs['tme'])
    y_s = moe_tail(rs, xs_s)

    n_pg = t_p // PAGE

    def split(hf):
        nb = hf.shape[1]
        half = GPB * SSM_P
        unslab = lambda a: a.reshape(nsl, nb, GPB, SSM_P).transpose(1, 0, 2, 3).reshape(nb, g, SSM_P)
        return unslab(hf[..., :half]), unslab(hf[..., half:])

    hp_re, hp_im = split(hf_p)
    hs_re, hs_im = split(hf_s)
    return (y_p.reshape(nb_p, t_p, d), y_s.reshape(nb_s, t_s, d),
            ckv_p.reshape(nb_p, n_pg, PAGE, kv_rank), krp_p[:, NOPE:QK].reshape(nb_p, n_pg, PAGE, ROPE),
            hp_re, hp_im, ckv_s.reshape(nb_s, t_s, kv_rank), kr_s.reshape(nb_s, t_s, ROPE), hs_re, hs_im)


_PARAM_NAMES = ('norm1_g', 'w_in', 'ssm_a_re', 'ssm_a_im', 'ssm_log_dt', 'ssm_b_re', 'ssm_b_im', 'ssm_c_re',
                'ssm_c_im', 'ssm_d', 'ssm_w_glu', 'ssm_b_glu', 'q_norm_g', 'w_uq', 'kv_norm_g', 'w_uk', 'w_uv',
                'qk_norm_q', 'qk_norm_k', 'out_norm_ssm', 'out_norm_attn', 'w_out', 'norm2_g', 'w_router_group',
                'b_router_group', 'w_router_expert', 'b_router_expert', 'w_gate', 'w_up', 'w_down')


def kernel(x_prompt, x_sample, cache_ckv, cache_krope, state_ssm_re, state_ssm_im, page_table, norm1_g, w_in, ssm_a_re, ssm_a_im, ssm_log_dt, ssm_b_re, ssm_b_im, ssm_c_re, ssm_c_im, ssm_d, ssm_w_glu, ssm_b_glu, q_norm_g, w_uq, kv_norm_g, w_uk, w_uv, qk_norm_q, qk_norm_k, out_norm_ssm, out_norm_attn, w_out, norm2_g, w_router_group, b_router_group, w_router_expert, b_router_expert, w_gate, w_up, w_down):
    params = (norm1_g, w_in, ssm_a_re, ssm_a_im, ssm_log_dt, ssm_b_re, ssm_b_im, ssm_c_re, ssm_c_im, ssm_d,
              ssm_w_glu, ssm_b_glu, q_norm_g, w_uq, kv_norm_g, w_uk, w_uv, qk_norm_q, qk_norm_k, out_norm_ssm,
              out_norm_attn, w_out, norm2_g, w_router_group, b_router_group, w_router_expert, b_router_expert,
              w_gate, w_up, w_down)
    depth = w_in.shape[0]
    h_p, h_s = x_prompt, x_sample
    outs = [[] for _ in range(8)]
    for layer in range(depth):
        p = {k: v[layer] for k, v in zip(_PARAM_NAMES, params)}
        res = _layer(h_p, h_s, cache_ckv[layer], cache_krope[layer], state_ssm_re[layer], state_ssm_im[layer],
                     page_table, p)
        h_p, h_s = res[0], res[1]
        for acc, r in zip(outs, res[2:]):
            acc.append(r)
    return (h_p, h_s) + tuple(jnp.stack(o) for o in outs)
```

```python
import functools
import math

import jax
import jax.numpy as jnp
from jax import lax
from jax.experimental import pallas as pl
from jax.experimental.pallas import tpu as pltpu

F32 = jnp.float32
BF16 = jnp.bfloat16
HIGHEST = lax.Precision.HIGHEST

LANES = 128
SUBLANES = 8
VMEM_LIMIT = 56 * 1024 * 1024

EPS = 1e-6
ROPE_BASE = 10000.0
PAGE = 128
SSM_CH = 16
SSM_P = 64
GPB = LANES // SSM_CH
S5_MAX_TOKENS = 8192
N_HEADS = 8
NOPE = 64
ROPE = 32
QK = NOPE + ROPE
V_DIM = 64
N_GROUPS = 4
EXP_PER_GROUP = 8
N_EXPERTS = N_GROUPS * EXP_PER_GROUP
ATTN_HEADS_PER_BODY = 8

NT_DIMS = (((1,), (1,)), ((), ()))
LOG2E = math.log2(math.e)


def _cparams(sem, vmem=VMEM_LIMIT):
    return pltpu.CompilerParams(dimension_semantics=sem, vmem_limit_bytes=vmem)


def _rms(x, g):
    return x * lax.rsqrt(jnp.mean(x * x, axis=-1, keepdims=True) + EPS) * g


def _rms_qk(x, g):
    return x * lax.rsqrt(jnp.sum(x * x, axis=-1, keepdims=True) * (1.0 / QK) + EPS) * g


def _rope(x, cos, sinp, sinm):
    return x * cos + pltpu.roll(x, 16, 1) * sinp + pltpu.roll(x, LANES - 16, 1) * sinm


def _row_chunks(n, size=256):
    size = min(size, n)
    return [pl.ds(r, size) for r in range(0, n, size)]


def _rope_tables(pos):
    inv_freq = ROPE_BASE ** (-jnp.arange(0, ROPE, 2, dtype=F32) / ROPE)
    ang = pos.astype(F32)[:, None] * inv_freq[None, :]
    c, s = jnp.cos(ang), jnp.sin(ang)
    t = pos.shape[0]
    z = lambda n: jnp.zeros((t, n), F32)
    cosf = jnp.concatenate([jnp.ones((t, NOPE), F32), c, c, z(LANES - QK)], axis=1)
    sinp = jnp.concatenate([z(NOPE + ROPE // 2), s, z(LANES - QK)], axis=1)
    sinm = jnp.concatenate([z(NOPE), -s, z(LANES - QK + ROPE // 2)], axis=1)
    return cosf, sinp, sinm


def _in_proj_kernel(x_ref, g1_ref, w_ref, gq_ref, gkv_ref, cos_ref, sinp_ref, sinm_ref,
                    u_ref, cq_ref, ckv_ref, kr_ref, *, ssm_w, q_rank, kv_rank):
    o1, o2, o3 = ssm_w, ssm_w + q_rank, ssm_w + q_rank + kv_rank
    for rows in _row_chunks(x_ref.shape[0], 128):
        n = _rms(x_ref[rows, :], g1_ref[...])
        z = jnp.dot(n.astype(BF16), w_ref[...], preferred_element_type=F32)
        for j in range(ssm_w // LANES):
            u_ref[j, rows, :] = z[:, j * LANES:(j + 1) * LANES]
        cq_ref[rows, :] = _rms(z[:, o1:o2], gq_ref[...]).astype(cq_ref.dtype)
        ckv_ref[rows, :] = _rms(z[:, o2:o3], gkv_ref[...])
        kr_ref[rows, :] = _rope(z[:, o3:o3 + LANES], cos_ref[rows, :], sinp_ref[rows, :], sinm_ref[rows, :])


def _in_proj(x, g1, w_pad, gq, gkv, tables, tm, ssm_w, q_rank, kv_rank):
    n, d = x.shape
    period = tables[0].shape[0]
    nper = period // tm
    row = lambda i: (i, 0)
    const = lambda i: (0, 0)
    tab = pl.BlockSpec((tm, LANES), lambda i: (i % nper, 0))
    return pl.pallas_call(
        functools.partial(_in_proj_kernel, ssm_w=ssm_w, q_rank=q_rank, kv_rank=kv_rank),
        grid=(n // tm,),
        in_specs=[pl.BlockSpec((tm, d), row), pl.BlockSpec((1, d), const),
                  pl.BlockSpec(w_pad.shape, const), pl.BlockSpec((1, q_rank), const),
                  pl.BlockSpec((1, kv_rank), const), tab, tab, tab],
        out_specs=[pl.BlockSpec((ssm_w // LANES, tm, LANES), lambda i: (0, i, 0)), pl.BlockSpec((tm, q_rank), row),
                   pl.BlockSpec((tm, kv_rank), row), pl.BlockSpec((tm, LANES), row)],
        out_shape=[jax.ShapeDtypeStruct((ssm_w // LANES, n, LANES), F32), jax.ShapeDtypeStruct((n, q_rank), BF16),
                   jax.ShapeDtypeStruct((n, kv_rank), F32), jax.ShapeDtypeStruct((n, LANES), F32)],
        compiler_params=_cparams(("parallel",)), name="in_proj",
    )(x, g1, w_pad, gq, gkv, *tables)


def _s5_weights(a_re, a_im, log_dt, b_re, b_im, c_re, c_im, d, lc):
    g, p = a_re.shape
    dt = jnp.exp(log_dt)[:, None]
    den = a_re * a_re + a_im * a_im
    mag, ang = jnp.exp(a_re * dt), a_im * dt
    lb_re, lb_im = mag * jnp.cos(ang), mag * jnp.sin(ang)
    num_re = lb_re - 1.0
    coef_re = (num_re * a_re + lb_im * a_im) / den
    coef_im = (lb_im * a_re - num_re * a_im) / den
    bb_re = coef_re[..., None] * b_re - coef_im[..., None] * b_im
    bb_im = coef_re[..., None] * b_im + coef_im[..., None] * b_re

    def lam_pow(k):
        kk = k.astype(F32)[:, None, None]
        m = jnp.exp(a_re * dt * kk)
        return m * jnp.cos(ang * kk), m * jnp.sin(ang * kk)

    steps = jnp.arange(lc)
    pr, pi = lam_pow(lc - 1 - steps)
    bz_re = pr[..., None] * bb_re[None] - pi[..., None] * bb_im[None]
    bz_im = pr[..., None] * bb_im[None] + pi[..., None] * bb_re[None]
    bz = jnp.concatenate([bz_re, bz_im], axis=2)
    bz = bz.transpose(1, 0, 3, 2).reshape(g, lc * SSM_CH, 2 * p)
    qr, qi = lam_pow(steps + 1)
    cl_re = c_re[None] * qr[:, :, None, :] - c_im[None] * qi[:, :, None, :]
    cl_im = c_re[None] * qi[:, :, None, :] + c_im[None] * qr[:, :, None, :]
    cz = jnp.concatenate([cl_re, -cl_im], axis=3)
    cz = cz.transpose(1, 3, 0, 2).reshape(g, 2 * p, lc * SSM_CH)
    tr, ti = lam_pow(steps)
    cb_re = jnp.einsum('gcp,kgp,gpd->gkcd', c_re, tr, bb_re, precision=HIGHEST)
    cb_re -= jnp.einsum('gcp,kgp,gpd->gkcd', c_re, ti, bb_im, precision=HIGHEST)
    cb_re -= jnp.einsum('gcp,kgp,gpd->gkcd', c_im, tr, bb_im, precision=HIGHEST)
    cb_re -= jnp.einsum('gcp,kgp,gpd->gkcd', c_im, ti, bb_re, precision=HIGHEST)
    tau = steps[None, :] - steps[:, None]
    ksel = cb_re[:, jnp.clip(tau, 0, lc - 1)]
    ksel = jnp.where((tau >= 0)[None, :, :, None, None], ksel, 0.0)
    tz = ksel.transpose(0, 1, 4, 2, 3).reshape(g, lc * SSM_CH, lc * SSM_CH)
    assert lc * SSM_CH == LANES and 2 * p == LANES
    nsl = g // GPB
    sw = GPB * LANES
    col = jnp.arange(sw)
    rep_state = (jnp.arange(LANES)[:, None] == (col // (sw // 2)) * p + col % p).astype(BF16)
    rep_token = (jnp.arange(LANES)[:, None] == (col // LANES) * SSM_CH + col % SSM_CH).astype(BF16)
    grp_state = (col // p) % GPB
    grp_token = (col // SSM_CH) % GPB

    def slab(rows, rep, row_grp, col_grp):
        full = jnp.einsum('jrk,kc->jrc', rows.astype(BF16), rep, preferred_element_type=F32)
        return jnp.where(row_grp[:, None] == col_grp[None, :], full, 0.0).astype(BF16)

    bz_rows = bz.reshape(nsl, GPB, lc, SSM_CH, LANES).transpose(0, 2, 1, 3, 4).reshape(nsl, sw, LANES)
    tz_rows = tz.reshape(nsl, GPB, lc, SSM_CH, LANES).transpose(0, 2, 1, 3, 4).reshape(nsl, sw, LANES)
    cz_rows = cz.reshape(nsl, GPB, 2, p, LANES).transpose(0, 2, 1, 3, 4).reshape(nsl, sw, LANES)
    bz_s = slab(bz_rows, rep_state, grp_token, grp_state)
    wy = jnp.concatenate([slab(cz_rows, rep_token, grp_state, grp_token),
                          slab(tz_rows, rep_token, grp_token, grp_token)], axis=1)
    lr, li = lam_pow(jnp.array([lc]))
    lam_re = lr[0].reshape(nsl, 1, GPB * p)
    lam_im = li[0].reshape(nsl, 1, GPB * p)
    dvec = jnp.tile(d.reshape(nsl, 1, LANES), (1, 1, lc))
    return bz_s, wy, lam_re, lam_im, dvec


def _s5_kernel(u_ref, bz_ref, wy_ref, lr_ref, li_ref, d_ref, h0_ref, y_ref, hf_ref, s_scr, hp_scr,
               *, nk, bs, lc, rb):
    nh = s_scr.shape[0] // 2
    blk = lambda a, c: a[:, c * LANES:(c + 1) * LANES]

    def ucat(r0):
        return jnp.concatenate([u_ref[0, pl.ds(r0 * lc + t, rb, stride=lc), :] for t in range(lc)], axis=1)

    def scan_rows(r0):
        if nk == 1:
            return pl.ds(r0, rb)
        seq = r0 // nk
        return pl.ds((r0 - seq * nk) * bs + seq, rb, stride=bs)

    def phase1(i, _):
        r0 = pl.multiple_of(i * rb, rb)
        s = jnp.dot(ucat(r0).astype(BF16), bz_ref[0], preferred_element_type=F32)
        for c in range(2 * nh):
            s_scr[c, scan_rows(r0), :] = blk(s, c)
        return 0

    lax.fori_loop(0, (bs * nk) // rb, phase1, 0)
    lr, li = lr_ref[0], li_ref[0]

    def step(k, h):
        rows = pl.ds(pl.multiple_of(k * bs, bs), bs)
        new = [None] * (2 * nh)
        for c in range(nh):
            h_re, h_im = h[c], h[nh + c]
            s_re, s_im = s_scr[c, rows, :], s_scr[nh + c, rows, :]
            hp_scr[c, rows, :] = h_re
            hp_scr[nh + c, rows, :] = h_im
            new[c] = blk(lr, c) * h_re - blk(li, c) * h_im + s_re
            new[nh + c] = blk(lr, c) * h_im + blk(li, c) * h_re + s_im
        return tuple(new)

    h0 = h0_ref[0, 0]
    hf = lax.fori_loop(0, nk, step, tuple(blk(h0, c) for c in range(2 * nh)), unroll=math.gcd(nk, 4))
    hf_ref[0, 0] = jnp.concatenate(hf, axis=1)

    def phase3(i, _):
        r0 = pl.multiple_of(i * rb, rb)
        uc = ucat(r0)
        hprev = jnp.concatenate([hp_scr[c, scan_rows(r0), :] for c in range(2 * nh)], axis=1)
        lhs = jnp.concatenate([hprev.astype(BF16), uc.astype(BF16)], axis=1)
        y = jnp.dot(lhs, wy_ref[0], preferred_element_type=F32) + d_ref[0] * uc
        for t in range(lc):
            y_ref[0, pl.ds(r0 * lc + t, rb, stride=lc), :] = y[:, t * LANES:(t + 1) * LANES]
        return 0

    lax.fori_loop(0, (bs * nk) // rb, phase3, 0)


def _s5(u, h0, weights, nb, t, lc):
    bz, wy, lam_re, lam_im, dvec = weights
    nsl = u.shape[0]
    nk = t // lc
    sw = h0.shape[-1]
    bs = nb
    while bs * t > S5_MAX_TOKENS and bs % 2 == 0:
        bs //= 2
    rb = _pick_tile(bs, 256) if nk == 1 else _pick_tile(nk, 256)
    h0 = h0.reshape(nsl, nb // bs, bs, sw)
    slab = lambda j, b: (j, 0, 0)
    tok = lambda j, b: (j, b, 0)
    st = lambda j, b: (j, b, 0, 0)
    y, hf = pl.pallas_call(
        functools.partial(_s5_kernel, nk=nk, bs=bs, lc=lc, rb=rb),
        grid=(nsl, nb // bs),
        in_specs=[pl.BlockSpec((1, bs * t, LANES), tok), pl.BlockSpec((1,) + bz.shape[1:], slab),
                  pl.BlockSpec((1,) + wy.shape[1:], slab), pl.BlockSpec((1, 1, sw // 2), slab),
                  pl.BlockSpec((1, 1, sw // 2), slab), pl.BlockSpec((1, 1, lc * LANES), slab),
                  pl.BlockSpec((1, 1, bs, sw), st)],
        out_specs=[pl.BlockSpec((1, bs * t, LANES), tok), pl.BlockSpec((1, 1, bs, sw), st)],
        out_shape=[jax.ShapeDtypeStruct(u.shape, F32), jax.ShapeDtypeStruct(h0.shape, F32)],
        scratch_shapes=[pltpu.VMEM((sw // LANES, bs * nk, LANES), F32)] * 2,
        compiler_params=_cparams(("parallel", "parallel")), name="s5",
    )(u, bz, wy, lam_re, lam_im, dvec, h0)
    return y, hf.reshape(nsl, nb, sw)


def _ones_lane(h):
    return V_DIM if h % 2 == 0 else 0


def _attn_prompt_kernel(cq_ref, ckv_ref, krp_ref, cos_ref, sinp_ref, sinm_ref, wq_ref, wk_ref, wv_ref,
                        gq_ref, gk_ref, o_ref, k_scr, v_scr, q_scr, *, tq, tk, hb):
    qi = pl.program_id(1)

    @pl.when(qi == 0)
    def _():
        c = ckv_ref[...].astype(BF16)
        krp = krp_ref[...]
        for h in range(N_HEADS):
            kh = jnp.dot(c, wk_ref[h], preferred_element_type=F32) + krp
            k_scr[h] = _rms_qk(kh, gk_ref[...]).astype(BF16)
        vlane = lax.broadcasted_iota(jnp.int32, (1, LANES), 1)
        for h in range(N_HEADS):
            one = jnp.where(vlane == _ones_lane(h), 1.0, 0.0)
            v_scr[h] = (jnp.dot(c, wv_ref[h], preferred_element_type=F32) + one).astype(BF16)

    cq = cq_ref[...]
    cos, sinp, sinm = cos_ref[...], sinp_ref[...], sinm_ref[...]
    sin = sinp - sinm
    for h in range(N_HEADS):
        qq = jnp.dot(cq, wq_ref[h], preferred_element_type=F32)
        q = qq[:, :LANES] * cos + qq[:, LANES:] * sin
        q_scr[h] = (_rms_qk(q, gq_ref[...]) * (LOG2E / math.sqrt(QK))).astype(BF16)

    assert tq == tk
    causal = lax.broadcasted_iota(jnp.int32, (tq, tk), 1) <= lax.broadcasted_iota(jnp.int32, (tq, tk), 0)
    lane = lax.broadcasted_iota(jnp.int32, (tq, LANES), 1)

    def fold(m, acc, q, h, ks, mask):
        s = lax.dot_general(q, k_scr[h, ks, :], NT_DIMS, preferred_element_type=F32)
        if mask is not None:
            s = jnp.where(mask, s, -jnp.inf)
        m_new = jnp.maximum(m, jnp.max(s, axis=-1, keepdims=True))
        p = jnp.exp2(s - m_new)
        acc = acc * jnp.exp2(m - m_new) + jnp.dot(p.astype(BF16), v_scr[h, ks, :], preferred_element_type=F32)
        return m_new, acc

    def kv_step(j, carry, heads, mask):
        ks = pl.ds(pl.multiple_of(j * tk, tk), tk)
        return tuple(fold(m, acc, q_scr[h], h, ks, mask) for (m, acc), h in zip(carry, heads))

    for h0 in range(0, N_HEADS, hb):
        heads = tuple(range(h0, h0 + hb))
        init = (jnp.full((tq, 1), -jnp.inf, F32), jnp.zeros((tq, LANES), F32))
        carry = lax.fori_loop(0, qi, functools.partial(kv_step, heads=heads, mask=None), (init,) * hb)
        carry = kv_step(qi, carry, heads, causal)
        outs = [acc / acc[:, _ones_lane(h):_ones_lane(h) + 1] for (_, acc), h in zip(carry, heads)]
        for k in range(0, hb, 2):
            hp = (h0 + k) // 2
            o_ref[:, hp * LANES:(hp + 1) * LANES] = jnp.where(lane < V_DIM, outs[k], outs[k + 1])


def _attn_prompt(cq, ckv, krp, tables, wq, wk, wvp, gq, gk, nb, t, tq, tk, hb=ATTN_HEADS_PER_BODY):
    n = cq.shape[0]
    nq = t // tq
    qrow = lambda b, i: (b * nq + i, 0)
    seq = lambda b, i: (b, 0)
    tab = pl.BlockSpec((tq, LANES), lambda b, i: (i, 0))
    c3 = lambda b, i: (0, 0, 0)
    c2 = lambda b, i: (0, 0)
    return pl.pallas_call(
        functools.partial(_attn_prompt_kernel, tq=tq, tk=tk, hb=hb),
        grid=(nb, nq),
        in_specs=[pl.BlockSpec((tq, cq.shape[1]), qrow), pl.BlockSpec((t, ckv.shape[1]), seq),
                  pl.BlockSpec((t, LANES), seq), tab, tab, tab,
                  pl.BlockSpec(wq.shape, c3), pl.BlockSpec(wk.shape, c3), pl.BlockSpec(wvp.shape, c3),
                  pl.BlockSpec((1, LANES), c2), pl.BlockSpec((1, LANES), c2)],
        out_specs=pl.BlockSpec((tq, N_HEADS * V_DIM), qrow),
        out_shape=jax.ShapeDtypeStruct((n, N_HEADS * V_DIM), F32),
        scratch_shapes=[pltpu.VMEM((N_HEADS, t, LANES), BF16), pltpu.VMEM((N_HEADS, t, LANES), BF16),
                        pltpu.VMEM((N_HEADS, tq, LANES), BF16)],
        compiler_params=_cparams(("parallel", "arbitrary")), name="attn_prompt",
    )(cq, ckv, krp, *tables, wq, wk, wvp, gq, gk)


def _q_sample_kernel(cq_ref, cos_ref, sinp_ref, sinm_ref, wq_ref, wukt_ref, sel_ref, gq_ref, gk_ref,
                     qa_ref, qr_ref):
    cq = cq_ref[...]
    for h in range(N_HEADS):
        q = _rope(jnp.dot(cq, wq_ref[h], preferred_element_type=F32), cos_ref[...], sinp_ref[...], sinm_ref[...])
        q = _rms_qk(q, gq_ref[...]) * (LOG2E / math.sqrt(QK))
        qk = (q * gk_ref[...]).astype(BF16)
        qa_ref[h] = jnp.dot(qk, wukt_ref[h], preferred_element_type=F32).astype(BF16)
        qr_ref[h] = jnp.dot(qk, sel_ref[...], preferred_element_type=F32).astype(BF16)


def _q_sample(cq, tables, wq, wukt, sel, gq, gk):
    n = cq.shape[0]
    return pl.pallas_call(
        _q_sample_kernel,
        out_shape=[jax.ShapeDtypeStruct((N_HEADS, n, LANES), BF16), jax.ShapeDtypeStruct((N_HEADS, n, ROPE), BF16)],
        compiler_params=pltpu.CompilerParams(vmem_limit_bytes=VMEM_LIMIT), name="q_sample",
    )(cq, *tables, wq, wukt, sel, gq, gk)


def _attn_sample_kernel(pt_ref, qa_ref, qr_ref, qa_prev_ref, qr_prev_ref, cnew_ref, krnew_ref, wukt_ref, wv_ref,
                        ckv_hbm, kr_hbm, o_ref, cbuf, kbuf, sems, s_scr, cb_scr, m_scr, l_scr, acc_scr,
                        *, pc, nc, total, t_dec):
    g = pl.program_id(0)
    slot = g % 2
    rows = N_HEADS * t_dec

    def page_copies(step, slot_):
        base = step * pc
        cps = []
        for j in range(pc):
            pg = pt_ref[base + j]
            cps.append(pltpu.make_async_copy(ckv_hbm.at[pg], cbuf.at[slot_, j], sems.at[0, slot_]))
            cps.append(pltpu.make_async_copy(kr_hbm.at[pg], kbuf.at[slot_, j], sems.at[1, slot_]))
        return cps

    def wait_pages(slot_):
        pltpu.make_async_copy(ckv_hbm.at[pl.ds(0, pc)], cbuf.at[slot_], sems.at[0, slot_]).wait()
        pltpu.make_async_copy(kr_hbm.at[pl.ds(0, pc)], kbuf.at[slot_], sems.at[1, slot_]).wait()

    def reset_softmax():
        m_scr[...] = jnp.full(m_scr.shape, -jnp.inf, F32)
        l_scr[...] = jnp.zeros(l_scr.shape, F32)
        acc_scr[...] = jnp.zeros(acc_scr.shape, F32)

    @pl.when(g == 0)
    def _():
        for cp in page_copies(g, slot):
            cp.start()
        s_scr[1] = jnp.zeros(s_scr.shape[1:], F32)
        cb_scr[1] = jnp.zeros(cb_scr.shape[1:], BF16)
        reset_softmax()

    @pl.when((g >= 1) & ((g - 1) % nc == 0))
    def _():
        reset_softmax()

    def scores(qa, qr, cb, krt):
        nk = cb.shape[0]
        knt = lax.dot_general(wukt_ref[...], cb, NT_DIMS, preferred_element_type=F32)
        n2 = jnp.sum((knt * knt).reshape(N_HEADS, NOPE, nk), axis=1)
        kr2 = jnp.sum(krt * krt, axis=0, keepdims=True)
        rinv = lax.rsqrt((n2 + kr2) * (1.0 / QK) + EPS)
        s = lax.dot_general(qa, cb, NT_DIMS, preferred_element_type=F32)
        s += jnp.dot(qr, krt.astype(BF16), preferred_element_type=F32)
        return (s.reshape(N_HEADS, t_dec, nk) * rinv[:, None, :]).reshape(rows, nk)

    def fold(s, cb):
        m = m_scr[...]
        m_new = jnp.maximum(m, jnp.max(s, axis=-1, keepdims=True))
        p = jnp.exp2(s - m_new)
        corr = jnp.exp2(m - m_new)
        l_scr[...] = l_scr[...] * corr + jnp.sum(p, axis=-1, keepdims=True)
        acc_scr[...] = acc_scr[...] * corr + jnp.dot(p.astype(BF16), cb, preferred_element_type=F32)
        m_scr[...] = m_new

    def score_and_fold(cur, prev):
        wait_pages(cur)
        for cp in page_copies(jnp.minimum(g + 1, total - 1), prev):
            cp.start()
        s_prev, cb_prev = s_scr[prev], cb_scr[prev]
        cb = cbuf[cur].reshape(pc * PAGE, cbuf.shape[-1]).astype(BF16)
        krt = jnp.concatenate([kbuf[cur, j] for j in range(pc)], axis=1)
        s_scr[cur] = scores(qa_ref[...].reshape(rows, LANES), qr_ref[...].reshape(rows, ROPE), cb, krt)
        cb_scr[cur] = cb
        fold(s_prev, cb_prev)

    for parity in range(2):
        pl.when(slot == parity)(functools.partial(score_and_fold, parity, 1 - parity))

    @pl.when(g == total)
    def _():
        wait_pages(1 - slot)

    @pl.when((g >= 1) & ((g - 1) % nc == nc - 1))
    def _():
        key = lax.broadcasted_iota(jnp.int32, (rows, PAGE), 1)
        step = lax.broadcasted_iota(jnp.int32, (rows, PAGE), 0) % t_dec
        cb_new = cnew_ref[0].astype(BF16)
        s_new = scores(qa_prev_ref[...].reshape(rows, LANES), qr_prev_ref[...].reshape(rows, ROPE), cb_new, krnew_ref[0])
        fold(jnp.where(key <= step, s_new, -jnp.inf), cb_new)
        o_lat = (acc_scr[...] / l_scr[...]).astype(BF16)
        lane = lax.broadcasted_iota(jnp.int32, (t_dec, LANES), 1)
        for hp in range(N_HEADS // 2):
            lo = jnp.dot(o_lat[(2 * hp) * t_dec:(2 * hp + 1) * t_dec], wv_ref[hp], preferred_element_type=F32)
            hi = jnp.dot(o_lat[(2 * hp + 1) * t_dec:(2 * hp + 2) * t_dec], wv_ref[hp], preferred_element_type=F32)
            o_ref[:, hp * LANES:(hp + 1) * LANES] = jnp.where(lane < V_DIM, lo, hi)


def _attn_sample(page_table, qa, qr, cnew_pad, krnew_pad, wukt_all, wvp, cache_ckv, cache_kr, t_dec, pc):
    nb, npg = page_table.shape
    nc = npg // pc
    total = nb * nc
    kv_rank = cache_ckv.shape[-1]
    nkeys = pc * PAGE
    rows = N_HEADS * t_dec
    seq_cur = lambda g: jnp.minimum(g // nc, nb - 1)
    seq_prev = lambda g: jnp.maximum(g - 1, 0) // nc
    grid_spec = pltpu.PrefetchScalarGridSpec(
        num_scalar_prefetch=1,
        grid=(total + 1,),
        in_specs=[pl.BlockSpec((N_HEADS, t_dec, LANES), lambda g, pt: (0, seq_cur(g), 0)),
                  pl.BlockSpec((N_HEADS, t_dec, ROPE), lambda g, pt: (0, seq_cur(g), 0)),
                  pl.BlockSpec((N_HEADS, t_dec, LANES), lambda g, pt: (0, seq_prev(g), 0)),
                  pl.BlockSpec((N_HEADS, t_dec, ROPE), lambda g, pt: (0, seq_prev(g), 0)),
                  pl.BlockSpec((1, PAGE, kv_rank), lambda g, pt: (seq_prev(g), 0, 0)),
                  pl.BlockSpec((1, ROPE, PAGE), lambda g, pt: (seq_prev(g), 0, 0)),
                  pl.BlockSpec(wukt_all.shape, lambda g, pt: (0, 0)),
                  pl.BlockSpec(wvp.shape, lambda g, pt: (0, 0, 0)),
                  pl.BlockSpec(memory_space=pl.ANY), pl.BlockSpec(memory_space=pl.ANY)],
        out_specs=pl.BlockSpec((t_dec, N_HEADS * V_DIM), lambda g, pt: (seq_prev(g), 0)),
        scratch_shapes=[pltpu.VMEM((2, pc, PAGE, kv_rank), F32), pltpu.VMEM((2, pc, ROPE, PAGE), F32),
                        pltpu.SemaphoreType.DMA((2, 2)),
                        pltpu.VMEM((2, rows, nkeys), F32), pltpu.VMEM((2, nkeys, kv_rank), BF16),
                        pltpu.VMEM((rows, 1), F32), pltpu.VMEM((rows, 1), F32), pltpu.VMEM((rows, kv_rank), F32)],
    )
    return pl.pallas_call(
        functools.partial(_attn_sample_kernel, pc=pc, nc=nc, total=total, t_dec=t_dec),
        grid_spec=grid_spec,
        out_shape=jax.ShapeDtypeStruct((nb * t_dec, N_HEADS * V_DIM), F32),
        compiler_params=_cparams(("arbitrary",)), name="attn_sample",
    )(page_table.reshape(-1), qa, qr, qa, qr, cnew_pad, krnew_pad, wukt_all, wvp, cache_ckv, cache_kr)


def _merge_kernel(xp_ref, xs_ref, yp_ref, ys_ref, ap_ref, as_ref, wglu_ref, bglu_ref, gs_ref, ga_ref,
                  wo1_ref, wo2_ref, g2_ref, wrh_ref, wrl_ref, br_ref, tri_ref,
                  h_ref, xn_ref, meta_ref, cnt_ref, run_scr, *, n_prompt_tiles):
    i = pl.program_id(0)

    @pl.when(i == 0)
    def _():
        run_scr[...] = jnp.zeros(run_scr.shape, F32)

    is_p = i < n_prompt_tiles
    run = run_scr[...]
    for rows in _row_chunks(h_ref.shape[0], tri_ref.shape[0]):
        x = jnp.where(is_p, xp_ref[rows, :], xs_ref[rows, :])
        slabs = lambda ref: jnp.concatenate([ref[j, rows, :] for j in range(ref.shape[0])], axis=1)
        y = jnp.where(is_p, slabs(yp_ref), slabs(ys_ref))
        att = jnp.where(is_p, ap_ref[rows, :], as_ref[rows, :])

        y = jax.nn.gelu(y)
        glu = jnp.dot(y.astype(BF16), wglu_ref[...], preferred_element_type=F32) + bglu_ref[...]
        ssm = y * jax.nn.sigmoid(glu)
        mix = jnp.dot(_rms(ssm, gs_ref[...]).astype(BF16), wo1_ref[...], preferred_element_type=F32)
        mix += jnp.dot(_rms(att, ga_ref[...]).astype(BF16), wo2_ref[...], preferred_element_type=F32)
        h = x + mix
        h_ref[rows, :] = h
        xn = _rms(h, g2_ref[...])
        xn_ref[rows, :] = xn
        xh = xn.astype(BF16)
        xl = (xn - xh.astype(F32)).astype(BF16)
        logits = (jnp.dot(xh, wrh_ref[...], preferred_element_type=F32)
                  + jnp.dot(xl, wrh_ref[...], preferred_element_type=F32)
                  + jnp.dot(xh, wrl_ref[...], preferred_element_type=F32)) + br_ref[...]

        tc = logits.shape[0]
        lane_i = lax.broadcasted_iota(jnp.int32, (tc, LANES), 1)
        lane = lane_i.astype(F32)
        big = float(LANES)
        first = lambda hit: jnp.min(jnp.where(hit, lane, big), axis=-1, keepdims=True)
        gl = jnp.where(lane_i < N_GROUPS, logits, -jnp.inf)
        gmax = jnp.max(gl, axis=-1, keepdims=True)
        grp = first(gl == gmax)
        p_sel = 1.0 / jnp.sum(jnp.exp(gl - gmax), axis=-1, keepdims=True)
        lane_grp = ((lane_i - N_GROUPS) >> 3).astype(F32)
        in_grp = (lane_i >= N_GROUPS) & (lane_i < N_GROUPS + N_EXPERTS) & (lane_grp == grp)
        el = jnp.where(in_grp, logits, -jnp.inf)
        m1 = jnp.max(el, axis=-1, keepdims=True)
        i1 = first(el == m1)
        el2 = jnp.where(lane == i1, -jnp.inf, el)
        m2 = jnp.max(el2, axis=-1, keepdims=True)
        i2 = first(el2 == m2)
        e21 = jnp.exp(m2 - m1)
        w1 = p_sel / (1.0 + e21)
        w2 = p_sel * e21 / (1.0 + e21)
        e1, e2 = i1 - N_GROUPS, i2 - N_GROUPS

        oh1, oh2 = lane == e1, lane == e2
        oh = jnp.where(oh1 | oh2, 1.0, 0.0)
        before = jnp.dot(tri_ref[...], oh.astype(BF16), preferred_element_type=F32) + run
        r1 = jnp.sum(jnp.where(oh1, before, 0.0), axis=-1, keepdims=True)
        r2 = jnp.sum(jnp.where(oh2, before, 0.0), axis=-1, keepdims=True)
        run = run + jnp.sum(oh, axis=0, keepdims=True)
        meta = jnp.zeros((tc, LANES), F32)
        for k, v in enumerate((e1, e2, w1, w2, r1, r2)):
            meta = jnp.where(lane_i == k, v, meta)
        meta_ref[rows, :] = meta

    run_scr[...] = run
    cnt_ref[...] = run


def _merge(x_p, x_s, y_p, y_s, a_p, a_s, wglu, bglu, gs, ga, wo1, wo2, g2, wr, br, tm):
    wrh = wr.astype(BF16)
    wrl = (wr - wrh.astype(F32)).astype(BF16)
    n_p, d = x_p.shape
    n_s = x_s.shape[0]
    npt, nst = n_p // tm, n_s // tm
    n_all = n_p + n_s
    tc = tm
    tri = (lax.broadcasted_iota(jnp.int32, (tc, tc), 0) > lax.broadcasted_iota(jnp.int32, (tc, tc), 1)).astype(BF16)
    prow = lambda i: (jnp.minimum(i, npt - 1), 0)
    srow = lambda i: (jnp.maximum(i - npt, 0), 0)
    row = lambda i: (i, 0)
    const = lambda i: (0, 0)
    w = a_p.shape[1]
    nsl = y_p.shape[0]
    full = lambda a: pl.BlockSpec(a.shape, const)
    return pl.pallas_call(
        functools.partial(_merge_kernel, n_prompt_tiles=npt),
        grid=(npt + nst,),
        in_specs=[pl.BlockSpec((tm, d), prow), pl.BlockSpec((tm, d), srow),
                  pl.BlockSpec((nsl, tm, LANES), lambda i: (0, jnp.minimum(i, npt - 1), 0)),
                  pl.BlockSpec((nsl, tm, LANES), lambda i: (0, jnp.maximum(i - npt, 0), 0)),
                  pl.BlockSpec((tm, w), prow), pl.BlockSpec((tm, w), srow),
                  full(wglu), full(bglu), full(gs), full(ga), full(wo1), full(wo2), full(g2), full(wrh), full(wrl), full(br),
                  full(tri)],
        out_specs=[pl.BlockSpec((tm, d), row), pl.BlockSpec((tm, d), row), pl.BlockSpec((tm, LANES), row),
                   pl.BlockSpec((1, LANES), const)],
        out_shape=[jax.ShapeDtypeStruct((n_all, d), F32), jax.ShapeDtypeStruct((n_all, d), F32),
                   jax.ShapeDtypeStruct((n_all, LANES), F32), jax.ShapeDtypeStruct((1, LANES), F32)],
        scratch_shapes=[pltpu.VMEM((1, LANES), F32)],
        compiler_params=_cparams(("arbitrary",)), name="merge",
    )(x_p, x_s, y_p, y_s, a_p, a_s, wglu, bglu, gs, ga, wo1, wo2, g2, wrh, wrl, br, tri)


def _start_row_gather(idx_ref, base, src_hbm, dst, sem, nrows):
    for r in range(nrows):
        pltpu.make_async_copy(src_hbm.at[pl.ds(idx_ref[base + r], 1)], dst.at[pl.ds(r, 1)], sem).start()


def _wait_row_gather(src_hbm, dst, sem, nrows):
    pltpu.make_async_copy(src_hbm.at[pl.ds(0, nrows)], dst, sem).wait()


def _pos_kernel(meta_ref, offs_ref, p1_ref, p2_ref):
    meta = meta_ref[...]
    tm = meta.shape[0]
    lane_i = lax.broadcasted_iota(jnp.int32, (tm, LANES), 1)
    lane = lane_i.astype(F32)
    diag = lax.broadcasted_iota(jnp.int32, (tm, LANES), 0) % LANES == lane_i

    def dense(e, r):
        pos = jnp.sum(jnp.where(lane == e, offs_ref[...], 0.0), axis=-1, keepdims=True) + r
        spread = jnp.where(diag, pos, 0.0).reshape(tm // LANES, LANES, LANES)
        return jnp.sum(spread, axis=1).astype(jnp.int32)

    p1_ref[...] = dense(meta[:, 0:1], meta[:, 4:5])
    p2_ref[...] = dense(meta[:, 1:2], meta[:, 5:6])


def _positions(meta, offs_vec, tm):
    n = meta.shape[0]
    rows = tm // LANES
    return pl.pallas_call(
        _pos_kernel,
        grid=(n // tm,),
        in_specs=[pl.BlockSpec((tm, LANES), lambda i: (i, 0)), pl.BlockSpec((1, LANES), lambda i: (0, 0))],
        out_specs=[pl.BlockSpec((rows, LANES), lambda i: (i, 0)), pl.BlockSpec((rows, LANES), lambda i: (i, 0))],
        out_shape=[jax.ShapeDtypeStruct((n // LANES, LANES), jnp.int32)] * 2,
        compiler_params=_cparams(("parallel",)), name="positions",
    )(meta, offs_vec)


def _zero_padding_tiles(tend_ref, xs_hbm, zbuf, sem, tme):
    zbuf[...] = jnp.zeros(zbuf.shape, zbuf.dtype)

    def last_tile(e, carry, wait):
        end = tend_ref[e]
        begin = jnp.where(e == 0, 0, tend_ref[jnp.maximum(e - 1, 0)])

        @pl.when(end > begin)
        def _():
            cp = pltpu.make_async_copy(zbuf, xs_hbm.at[pl.ds((end - 1) * tme, tme)], sem)
            cp.wait() if wait else cp.start()
        return carry

    def spare_tile(t, carry, wait):
        cp = pltpu.make_async_copy(zbuf, xs_hbm.at[pl.ds(t * tme, tme)], sem)
        cp.wait() if wait else cp.start()
        return carry

    n_used, n_tiles = tend_ref[N_EXPERTS - 1], xs_hbm.shape[0] // tme
    for wait in (False, True):
        lax.fori_loop(0, N_EXPERTS, functools.partial(last_tile, wait=wait), 0)
        lax.fori_loop(n_used, n_tiles, functools.partial(spare_tile, wait=wait), 0)


def _dispatch_kernel(p1_ref, p2_ref, tend_ref, xn_ref, xs_hbm, zbuf, sem, *, tmd, tme):
    base = pl.program_id(0) * tmd

    @pl.when(pl.program_id(0) == 0)
    def _():
        _zero_padding_tiles(tend_ref, xs_hbm, zbuf, sem, tme)

    for r in range(tmd):
        src = xn_ref.at[pl.ds(r, 1)]
        pltpu.make_async_copy(src, xs_hbm.at[pl.ds(p1_ref[base + r], 1)], sem).start()
        pltpu.make_async_copy(src, xs_hbm.at[pl.ds(p2_ref[base + r], 1)], sem).start()
    for _ in range(2):
        pltpu.make_async_copy(xn_ref, xs_hbm.at[pl.ds(0, tmd)], sem).wait()


def _dispatch(pos1, pos2, tile_end, xn, n_rows, tmd, tme):
    n, d = xn.shape
    grid_spec = pltpu.PrefetchScalarGridSpec(
        num_scalar_prefetch=3,
        grid=(n // tmd,),
        in_specs=[pl.BlockSpec((tmd, d), lambda i, p1, p2, te: (i, 0))],
        out_specs=pl.BlockSpec(memory_space=pl.ANY),
        scratch_shapes=[pltpu.VMEM((tme, d), xn.dtype), pltpu.SemaphoreType.DMA],
    )
    return pl.pallas_call(
        functools.partial(_dispatch_kernel, tmd=tmd, tme=tme),
        grid_spec=grid_spec,
        out_shape=jax.ShapeDtypeStruct((n_rows, d), xn.dtype),
        compiler_params=_cparams(("arbitrary",)), name="dispatch",
    )(pos1, pos2, tile_end, xn)


def _experts_kernel(te_ref, nu_ref, x_ref, wg_ref, wu_ref, wd_ref, o_ref, wg_scr, wu_scr, wd_scr):
    i = pl.program_id(0)

    @pl.when((i == 0) | (te_ref[i] != te_ref[jnp.maximum(i - 1, 0)]))
    def _():
        wg_scr[...] = wg_ref[0].astype(BF16)
        wu_scr[...] = wu_ref[0].astype(BF16)
        wd_scr[...] = wd_ref[0].astype(BF16)

    @pl.when(i < nu_ref[0])
    def _():
        x = x_ref[...].astype(BF16)
        hg = jnp.dot(x, wg_scr[...], preferred_element_type=F32)
        hu = jnp.dot(x, wu_scr[...], preferred_element_type=F32)
        hh = (jax.nn.silu(hg) * hu).astype(BF16)
        o_ref[...] = jnp.dot(hh, wd_scr[...], preferred_element_type=F32)

    @pl.when(i >= nu_ref[0])
    def _():
        o_ref[...] = jnp.zeros(o_ref.shape, F32)


def _experts(tile_expert, n_used, xs, w_gate, w_up, w_down, tme):
    nt = tile_expert.shape[0]
    ne, d, de = w_gate.shape
    wmap = lambda i, te, nu: (te[i], 0, 0)
    grid_spec = pltpu.PrefetchScalarGridSpec(
        num_scalar_prefetch=2,
        grid=(nt,),
        in_specs=[pl.BlockSpec((tme, d), lambda i, te, nu: (jnp.minimum(i, nu[0] - 1), 0)),
                  pl.BlockSpec((1, d, de), wmap), pl.BlockSpec((1, d, de), wmap), pl.BlockSpec((1, de, d), wmap)],
        out_specs=pl.BlockSpec((tme, d), lambda i, te, nu: (i, 0)),
        scratch_shapes=[pltpu.VMEM((d, de), BF16), pltpu.VMEM((d, de), BF16), pltpu.VMEM((de, d), BF16)],
    )
    return pl.pallas_call(
        _experts_kernel,
        grid_spec=grid_spec,
        out_shape=jax.ShapeDtypeStruct((nt * tme, d), F32),
        compiler_params=_cparams(("arbitrary",)), name="experts",
    )(tile_expert, n_used, xs, w_gate, w_up, w_down)


def _combine_kernel(p1_ref, p2_ref, h_ref, meta_ref, eo_hbm, y_ref, buf, sems, *, tmc, nsteps):
    i = pl.program_id(0)
    slot = i % 2

    def start(step, slot_):
        _start_row_gather(p1_ref, step * tmc, eo_hbm, buf.at[slot_, pl.ds(0, tmc)], sems.at[slot_], tmc)
        _start_row_gather(p2_ref, step * tmc, eo_hbm, buf.at[slot_, pl.ds(tmc, tmc)], sems.at[slot_], tmc)

    @pl.when(i == 0)
    def _():
        start(0, 0)

    @pl.when(i + 1 < nsteps)
    def _():
        start(i + 1, 1 - slot)

    _wait_row_gather(eo_hbm, buf.at[slot], sems.at[slot], 2 * tmc)
    meta = meta_ref[...]
    y_ref[...] = h_ref[...] + meta[:, 2:3] * buf[slot, :tmc, :] + meta[:, 3:4] * buf[slot, tmc:, :]


def _combine(pos1, pos2, h_all, meta_all, eo, row0, tmc):
    n = pos1.shape[0]
    d = h_all.shape[1]
    nsteps = n // tmc
    off = row0 // tmc
    grid_spec = pltpu.PrefetchScalarGridSpec(
        num_scalar_prefetch=2,
        grid=(nsteps,),
        in_specs=[pl.BlockSpec((tmc, d), lambda i, p1, p2: (i + off, 0)),
                  pl.BlockSpec((tmc, LANES), lambda i, p1, p2: (i + off, 0)),
                  pl.BlockSpec(memory_space=pl.ANY)],
        out_specs=pl.BlockSpec((tmc, d), lambda i, p1, p2: (i, 0)),
        scratch_shapes=[pltpu.VMEM((2, 2 * tmc, d), F32), pltpu.SemaphoreType.DMA((2,))],
    )
    return pl.pallas_call(
        functools.partial(_combine_kernel, tmc=tmc, nsteps=nsteps),
        grid_spec=grid_spec,
        out_shape=jax.ShapeDtypeStruct((n, d), F32),
        compiler_params=_cparams(("arbitrary",)), name="combine",
    )(pos1, pos2, h_all, meta_all, eo)


def _pick_tile(n, pref):
    t = min(pref, n)
    while n % t:
        t //= 2
    return t


def _layer(x_p, x_s, cache_ckv, cache_kr, st_re, st_im, page_table, p):
    nb_p, t_p, d = x_p.shape
    nb_s, t_s, _ = x_s.shape
    n_p, n_s = nb_p * t_p, nb_s * t_s
    past_len = page_table.shape[1] * PAGE
    g = p['ssm_a_re'].shape[0]
    ssm_w = g * SSM_CH
    q_rank = p['w_uq'].shape[0]
    kv_rank = p['w_uk'].shape[0]

    w_in = p['w_in']
    o3 = ssm_w + q_rank + kv_rank
    w_pad = jnp.zeros((d, o3 + LANES), F32).at[:, :o3].set(w_in[:, :o3])
    w_pad = w_pad.at[:, o3 + NOPE:o3 + QK].set(w_in[:, o3:]).astype(BF16)
    row2 = lambda v: v.reshape(1, -1).astype(F32)
    padq = lambda v: jnp.pad(v, (0, LANES - QK)).reshape(1, LANES)
    wq = jnp.pad(p['w_uq'].transpose(1, 0, 2), ((0, 0), (0, 0), (0, LANES - QK))).astype(BF16)
    wk = jnp.pad(p['w_uk'].transpose(1, 0, 2), ((0, 0), (0, 0), (0, LANES - NOPE))).astype(BF16)
    wvp = p['w_uv'].reshape(kv_rank, N_HEADS // 2, 2 * V_DIM).transpose(1, 0, 2).astype(BF16)
    wv_h = p['w_uv'].transpose(1, 0, 2)
    wv8 = jnp.stack([jnp.pad(wv_h[h], ((0, 0), (V_DIM - _ones_lane(h), _ones_lane(h)))) for h in range(N_HEADS)])
    wv8 = wv8.astype(BF16)
    wukt_all = p['w_uk'].reshape(kv_rank, N_HEADS * NOPE).T.astype(BF16)
    wukt = jnp.pad(p['w_uk'].transpose(1, 2, 0), ((0, 0), (0, LANES - NOPE), (0, 0))).astype(BF16)
    sel = (jnp.arange(LANES)[:, None] == NOPE + jnp.arange(ROPE)[None, :]).astype(BF16)
    gq, gk = padq(p['qk_norm_q']), padq(p['qk_norm_k'])
    ssm_params = (p['ssm_a_re'], p['ssm_a_im'], p['ssm_log_dt'], p['ssm_b_re'], p['ssm_b_im'],
                  p['ssm_c_re'], p['ssm_c_im'], p['ssm_d'])

    tm_p = _pick_tile(t_p, 512)
    tab_p = _rope_tables(jnp.arange(t_p))
    u, cq, ckv_p, krp_p = _in_proj(x_p.reshape(n_p, d), row2(p['norm1_g']), w_pad, row2(p['q_norm_g']),
                                   row2(p['kv_norm_g']), tab_p, tm_p, ssm_w, q_rank, kv_rank)
    lc_p = _pick_tile(t_p, 8)
    nsl = ssm_w // LANES
    s5w_p = _s5_weights(*ssm_params, lc_p)
    y_ssm_p, hf_p = _s5(u, jnp.zeros((nsl, nb_p, 2 * GPB * SSM_P), F32), s5w_p, nb_p, t_p, lc_p)
    tq = _pick_tile(t_p, 512)
    h1, h2 = NOPE + ROPE // 2, QK
    wq_rot = jnp.concatenate([jnp.zeros_like(wq[..., :NOPE]), -wq[..., h1:h2], wq[..., NOPE:h1],
                              jnp.zeros_like(wq[..., h2:])], axis=-1)
    att_p = _attn_prompt(cq, ckv_p, krp_p, tab_p, jnp.concatenate([wq, wq_rot], axis=-1), wk, wv8, gq, gk,
                         nb_p, t_p, tq, tq)

    tm_s = _pick_tile(n_s, 512)
    pos_s = past_len + jnp.arange(t_s)
    tab_s = tuple(jnp.tile(a, (tm_s // t_s, 1)) for a in _rope_tables(pos_s))
    u, cq, ckv_s, krp_s = _in_proj(x_s.reshape(n_s, d), row2(p['norm1_g']), w_pad, row2(p['q_norm_g']),
                                   row2(p['kv_norm_g']), tab_s, tm_s, ssm_w, q_rank, kv_rank)
    slab_state = lambda s: s.astype(F32).reshape(nb_s, nsl, GPB * SSM_P).transpose(1, 0, 2)
    h0 = jnp.concatenate([slab_state(st_re), slab_state(st_im)], axis=-1)
    s5w_s = s5w_p if t_s == lc_p else _s5_weights(*ssm_params, t_s)
    y_ssm_s, hf_s = _s5(u, h0, s5w_s, nb_s, t_s, t_s)
    tab_q = tuple(jnp.tile(a, (n_s // tm_s, 1)) for a in tab_s)
    qa, qr = _q_sample(cq, tab_q, wq, wukt, sel, gq, gk)
    kr_s = krp_s[:, NOPE:QK]
    cnew = jnp.pad(ckv_s.reshape(nb_s, t_s, kv_rank), ((0, 0), (0, PAGE - t_s), (0, 0)))
    krnew = jnp.pad(kr_s.reshape(nb_s, t_s, ROPE), ((0, 0), (0, PAGE - t_s), (0, 0))).swapaxes(1, 2)
    pc = _pick_tile(page_table.shape[1], 64)
    att_s = _attn_sample(page_table, qa, qr, cnew, krnew, wukt_all, wvp, cache_ckv, jnp.swapaxes(cache_kr, 1, 2),
                         t_s, pc)

    tm = _pick_tile(math.gcd(n_p, n_s), 512)
    w_out = p['w_out'].astype(BF16)
    wr = jnp.zeros((d, LANES), F32).at[:, :N_GROUPS].set(p['w_router_group'])
    wr = wr.at[:, N_GROUPS:N_GROUPS + N_EXPERTS].set(p['w_router_expert'])
    br = jnp.zeros((1, LANES), F32).at[0, :N_GROUPS].set(p['b_router_group'])
    br = br.at[0, N_GROUPS:N_GROUPS + N_EXPERTS].set(p['b_router_expert'])
    h_all, xn_all, meta, cnt = _merge(
        x_p.reshape(n_p, d), x_s.reshape(n_s, d), y_ssm_p, y_ssm_s, att_p, att_s,
        p['ssm_w_glu'].astype(BF16), row2(p['ssm_b_glu']), row2(p['out_norm_ssm']), row2(p['out_norm_attn']),
        w_out[:ssm_w], w_out[ssm_w:], row2(p['norm2_g']), wr, br, tm)

    n_all = n_p + n_s
    tme = 512
    nt = (2 * n_all) // tme + N_EXPERTS
    counts = cnt[0, :N_EXPERTS].astype(jnp.int32)
    tiles_per = (counts + tme - 1) // tme
    tile_end = jnp.cumsum(tiles_per)
    tile_start = tile_end - tiles_per
    tile_id = jnp.arange(nt, dtype=jnp.int32)
    tile_expert = jnp.minimum(jnp.sum(tile_end[None, :] <= tile_id[:, None], axis=1), N_EXPERTS - 1).astype(jnp.int32)
    n_used = tile_end[-1:].astype(jnp.int32)
    offs_vec = jnp.zeros((1, LANES), F32).at[0, :N_EXPERTS].set((tile_start * tme).astype(F32))
    tmd = _pick_tile(n_all, 1024)
    pos1, pos2 = (a.reshape(-1) for a in _positions(meta, offs_vec, tmd))
    xs = _dispatch(pos1, pos2, tile_end.astype(jnp.int32), xn_all, nt * tme, tmd, tme)
    eo = _experts(tile_expert, n_used, xs, p['w_gate'], p['w_up'], p['w_down'], tme)

    tmc = _pick_tile(math.gcd(n_p, n_s), 256)
    y_p = _combine(pos1[:n_p], pos2[:n_p], h_all, meta, eo, 0, tmc)
    y_s = _combine(pos1[n_p:], pos2[n_p:], h_all, meta, eo, n_p, tmc)

    n_pg = t_p // PAGE

    def split(hf):
        nb = hf.shape[1]
        half = GPB * SSM_P
        unslab = lambda a: a.reshape(nsl, nb, GPB, SSM_P).transpose(1, 0, 2, 3).reshape(nb, g, SSM_P)
        return unslab(hf[..., :half]), unslab(hf[..., half:])

    hp_re, hp_im = split(hf_p)
    hs_re, hs_im = split(hf_s)
    return (y_p.reshape(nb_p, t_p, d), y_s.reshape(nb_s, t_s, d),
            ckv_p.reshape(nb_p, n_pg, PAGE, kv_rank), krp_p[:, NOPE:QK].reshape(nb_p, n_pg, PAGE, ROPE),
            hp_re, hp_im, ckv_s.reshape(nb_s, t_s, kv_rank), kr_s.reshape(nb_s, t_s, ROPE), hs_re, hs_im)


_PARAM_NAMES = ('norm1_g', 'w_in', 'ssm_a_re', 'ssm_a_im', 'ssm_log_dt', 'ssm_b_re', 'ssm_b_im', 'ssm_c_re',
                'ssm_c_im', 'ssm_d', 'ssm_w_glu', 'ssm_b_glu', 'q_norm_g', 'w_uq', 'kv_norm_g', 'w_uk', 'w_uv',
                'qk_norm_q', 'qk_norm_k', 'out_norm_ssm', 'out_norm_attn', 'w_out', 'norm2_g', 'w_router_group',
                'b_router_group', 'w_router_expert', 'b_router_expert', 'w_gate', 'w_up', 'w_down')


def kernel(x_prompt, x_sample, cache_ckv, cache_krope, state_ssm_re, state_ssm_im, page_table, norm1_g, w_in, ssm_a_re, ssm_a_im, ssm_log_dt, ssm_b_re, ssm_b_im, ssm_c_re, ssm_c_im, ssm_d, ssm_w_glu, ssm_b_glu, q_norm_g, w_uq, kv_norm_g, w_uk, w_uv, qk_norm_q, qk_norm_k, out_norm_ssm, out_norm_attn, w_out, norm2_g, w_router_group, b_router_group, w_router_expert, b_router_expert, w_gate, w_up, w_down):
    params = (norm1_g, w_in, ssm_a_re, ssm_a_im, ssm_log_dt, ssm_b_re, ssm_b_im, ssm_c_re, ssm_c_im, ssm_d,
              ssm_w_glu, ssm_b_glu, q_norm_g, w_uq, kv_norm_g, w_uk, w_uv, qk_norm_q, qk_norm_k, out_norm_ssm,
              out_norm_attn, w_out, norm2_g, w_router_group, b_router_group, w_router_expert, b_router_expert,
              w_gate, w_up, w_down)
    depth = w_in.shape[0]
    h_p, h_s = x_prompt, x_sample
    outs = [[] for _ in range(8)]
    for layer in range(depth):
        p = {k: v[layer] for k, v in zip(_PARAM_NAMES, params)}
        res = _layer(h_p, h_s, cache_ckv[layer], cache_krope[layer], state_ssm_re[layer], state_ssm_im[layer],
                     page_table, p)
        h_p, h_s = res[0], res[1]
        for acc, r in zip(outs, res[2:]):
            acc.append(r)
    return (h_p, h_s) + tuple(jnp.stack(o) for o in outs)
```

```python
import functools
import math

import jax
import jax.numpy as jnp
from jax import lax
from jax.experimental import pallas as pl
from jax.experimental.pallas import tpu as pltpu

F32 = jnp.float32
BF16 = jnp.bfloat16
HIGHEST = lax.Precision.HIGHEST

LANES = 128
SUBLANES = 8
VMEM_LIMIT = 56 * 1024 * 1024

EPS = 1e-6
ROPE_BASE = 10000.0
PAGE = 128
SSM_CH = 16
SSM_P = 64
GPB = LANES // SSM_CH
S5_MAX_TOKENS = 8192
N_HEADS = 8
NOPE = 64
ROPE = 32
QK = NOPE + ROPE
V_DIM = 64
N_GROUPS = 4
EXP_PER_GROUP = 8
N_EXPERTS = N_GROUPS * EXP_PER_GROUP
ATTN_HEADS_PER_BODY = 8

NT_DIMS = (((1,), (1,)), ((), ()))
LOG2E = math.log2(math.e)


def _cparams(sem, vmem=VMEM_LIMIT):
    return pltpu.CompilerParams(dimension_semantics=sem, vmem_limit_bytes=vmem)


def _rms(x, g):
    return x * lax.rsqrt(jnp.mean(x * x, axis=-1, keepdims=True) + EPS) * g


def _rms_qk(x, g):
    return x * lax.rsqrt(jnp.sum(x * x, axis=-1, keepdims=True) * (1.0 / QK) + EPS) * g


def _rope(x, cos, sinp, sinm):
    return x * cos + pltpu.roll(x, 16, 1) * sinp + pltpu.roll(x, LANES - 16, 1) * sinm


def _row_chunks(n, size=256):
    size = min(size, n)
    return [pl.ds(r, size) for r in range(0, n, size)]


def _rope_tables(pos):
    inv_freq = ROPE_BASE ** (-jnp.arange(0, ROPE, 2, dtype=F32) / ROPE)
    ang = pos.astype(F32)[:, None] * inv_freq[None, :]
    c, s = jnp.cos(ang), jnp.sin(ang)
    t = pos.shape[0]
    z = lambda n: jnp.zeros((t, n), F32)
    cosf = jnp.concatenate([jnp.ones((t, NOPE), F32), c, c, z(LANES - QK)], axis=1)
    sinp = jnp.concatenate([z(NOPE + ROPE // 2), s, z(LANES - QK)], axis=1)
    sinm = jnp.concatenate([z(NOPE), -s, z(LANES - QK + ROPE // 2)], axis=1)
    return cosf, sinp, sinm


def _in_proj_kernel(x_ref, g1_ref, w_ref, gq_ref, gkv_ref, cos_ref, sinp_ref, sinm_ref,
                    u_ref, cq_ref, ckv_ref, kr_ref, *, ssm_w, q_rank, kv_rank):
    o1, o2, o3 = ssm_w, ssm_w + q_rank, ssm_w + q_rank + kv_rank
    for rows in _row_chunks(x_ref.shape[0], 128):
        n = _rms(x_ref[rows, :], g1_ref[...])
        z = jnp.dot(n.astype(BF16), w_ref[...], preferred_element_type=F32)
        for j in range(ssm_w // LANES):
            u_ref[j, rows, :] = z[:, j * LANES:(j + 1) * LANES]
        cq_ref[rows, :] = _rms(z[:, o1:o2], gq_ref[...]).astype(cq_ref.dtype)
        ckv_ref[rows, :] = _rms(z[:, o2:o3], gkv_ref[...])
        kr_ref[rows, :] = _rope(z[:, o3:o3 + LANES], cos_ref[rows, :], sinp_ref[rows, :], sinm_ref[rows, :])


def _in_proj(x, g1, w_pad, gq, gkv, tables, tm, ssm_w, q_rank, kv_rank):
    n, d = x.shape
    period = tables[0].shape[0]
    nper = period // tm
    row = lambda i: (i, 0)
    const = lambda i: (0, 0)
    tab = pl.BlockSpec((tm, LANES), lambda i: (i % nper, 0))
    return pl.pallas_call(
        functools.partial(_in_proj_kernel, ssm_w=ssm_w, q_rank=q_rank, kv_rank=kv_rank),
        grid=(n // tm,),
        in_specs=[pl.BlockSpec((tm, d), row), pl.BlockSpec((1, d), const),
                  pl.BlockSpec(w_pad.shape, const), pl.BlockSpec((1, q_rank), const),
                  pl.BlockSpec((1, kv_rank), const), tab, tab, tab],
        out_specs=[pl.BlockSpec((ssm_w // LANES, tm, LANES), lambda i: (0, i, 0)), pl.BlockSpec((tm, q_rank), row),
                   pl.BlockSpec((tm, kv_rank), row), pl.BlockSpec((tm, LANES), row)],
        out_shape=[jax.ShapeDtypeStruct((ssm_w // LANES, n, LANES), F32), jax.ShapeDtypeStruct((n, q_rank), BF16),
                   jax.ShapeDtypeStruct((n, kv_rank), F32), jax.ShapeDtypeStruct((n, LANES), F32)],
        compiler_params=_cparams(("parallel",)), name="in_proj",
    )(x, g1, w_pad, gq, gkv, *tables)


def _s5_weights(a_re, a_im, log_dt, b_re, b_im, c_re, c_im, d, lc):
    g, p = a_re.shape
    dt = jnp.exp(log_dt)[:, None]
    den = a_re * a_re + a_im * a_im
    mag, ang = jnp.exp(a_re * dt), a_im * dt
    lb_re, lb_im = mag * jnp.cos(ang), mag * jnp.sin(ang)
    num_re = lb_re - 1.0
    coef_re = (num_re * a_re + lb_im * a_im) / den
    coef_im = (lb_im * a_re - num_re * a_im) / den
    bb_re = coef_re[..., None] * b_re - coef_im[..., None] * b_im
    bb_im = coef_re[..., None] * b_im + coef_im[..., None] * b_re

    def lam_pow(k):
        kk = k.astype(F32)[:, None, None]
        m = jnp.exp(a_re * dt * kk)
        return m * jnp.cos(ang * kk), m * jnp.sin(ang * kk)

    steps = jnp.arange(lc)
    pr, pi = lam_pow(lc - 1 - steps)
    bz_re = pr[..., None] * bb_re[None] - pi[..., None] * bb_im[None]
    bz_im = pr[..., None] * bb_im[None] + pi[..., None] * bb_re[None]
    bz = jnp.concatenate([bz_re, bz_im], axis=2)
    bz = bz.transpose(1, 0, 3, 2).reshape(g, lc * SSM_CH, 2 * p)
    qr, qi = lam_pow(steps + 1)
    cl_re = c_re[None] * qr[:, :, None, :] - c_im[None] * qi[:, :, None, :]
    cl_im = c_re[None] * qi[:, :, None, :] + c_im[None] * qr[:, :, None, :]
    cz = jnp.concatenate([cl_re, -cl_im], axis=3)
    cz = cz.transpose(1, 3, 0, 2).reshape(g, 2 * p, lc * SSM_CH)
    tr, ti = lam_pow(steps)
    cb_re = jnp.einsum('gcp,kgp,gpd->gkcd', c_re, tr, bb_re, precision=HIGHEST)
    cb_re -= jnp.einsum('gcp,kgp,gpd->gkcd', c_re, ti, bb_im, precision=HIGHEST)
    cb_re -= jnp.einsum('gcp,kgp,gpd->gkcd', c_im, tr, bb_im, precision=HIGHEST)
    cb_re -= jnp.einsum('gcp,kgp,gpd->gkcd', c_im, ti, bb_re, precision=HIGHEST)
    tau = steps[None, :] - steps[:, None]
    ksel = cb_re[:, jnp.clip(tau, 0, lc - 1)]
    ksel = jnp.where((tau >= 0)[None, :, :, None, None], ksel, 0.0)
    tz = ksel.transpose(0, 1, 4, 2, 3).reshape(g, lc * SSM_CH, lc * SSM_CH)
    assert lc * SSM_CH == LANES and 2 * p == LANES
    nsl = g // GPB
    sw = GPB * LANES
    col = jnp.arange(sw)
    rep_state = (jnp.arange(LANES)[:, None] == (col // (sw // 2)) * p + col % p).astype(BF16)
    rep_token = (jnp.arange(LANES)[:, None] == (col // LANES) * SSM_CH + col % SSM_CH).astype(BF16)
    grp_state = (col // p) % GPB
    grp_token = (col // SSM_CH) % GPB

    def slab(rows, rep, row_grp, col_grp):
        full = jnp.einsum('jrk,kc->jrc', rows.astype(BF16), rep, preferred_element_type=F32)
        return jnp.where(row_grp[:, None] == col_grp[None, :], full, 0.0).astype(BF16)

    bz_rows = bz.reshape(nsl, GPB, lc, SSM_CH, LANES).transpose(0, 2, 1, 3, 4).reshape(nsl, sw, LANES)
    tz_rows = tz.reshape(nsl, GPB, lc, SSM_CH, LANES).transpose(0, 2, 1, 3, 4).reshape(nsl, sw, LANES)
    cz_rows = cz.reshape(nsl, GPB, 2, p, LANES).transpose(0, 2, 1, 3, 4).reshape(nsl, sw, LANES)
    bz_s = slab(bz_rows, rep_state, grp_token, grp_state)
    wy = jnp.concatenate([slab(cz_rows, rep_token, grp_state, grp_token),
                          slab(tz_rows, rep_token, grp_token, grp_token)], axis=1)
    lr, li = lam_pow(jnp.array([lc]))
    lam_re = lr[0].reshape(nsl, 1, GPB * p)
    lam_im = li[0].reshape(nsl, 1, GPB * p)
    dvec = jnp.tile(d.reshape(nsl, 1, LANES), (1, 1, lc))
    return bz_s, wy, lam_re, lam_im, dvec


def _s5_kernel(u_ref, bz_ref, wy_ref, lr_ref, li_ref, d_ref, h0_ref, y_ref, hf_ref, s_scr, hp_scr,
               *, nk, bs, lc, rb):
    nh = s_scr.shape[0] // 2
    blk = lambda a, c: a[:, c * LANES:(c + 1) * LANES]

    def ucat(r0):
        return jnp.concatenate([u_ref[0, pl.ds(r0 * lc + t, rb, stride=lc), :] for t in range(lc)], axis=1)

    def scan_rows(r0):
        if nk == 1:
            return pl.ds(r0, rb)
        seq = r0 // nk
        return pl.ds((r0 - seq * nk) * bs + seq, rb, stride=bs)

    def phase1(i, _):
        r0 = pl.multiple_of(i * rb, rb)
        s = jnp.dot(ucat(r0).astype(BF16), bz_ref[0], preferred_element_type=F32)
        for c in range(2 * nh):
            s_scr[c, scan_rows(r0), :] = blk(s, c)
        return 0

    lax.fori_loop(0, (bs * nk) // rb, phase1, 0)
    lr, li = lr_ref[0], li_ref[0]

    def step(k, h):
        rows = pl.ds(pl.multiple_of(k * bs, bs), bs)
        new = [None] * (2 * nh)
        for c in range(nh):
            h_re, h_im = h[c], h[nh + c]
            s_re, s_im = s_scr[c, rows, :], s_scr[nh + c, rows, :]
            hp_scr[c, rows, :] = h_re
            hp_scr[nh + c, rows, :] = h_im
            new[c] = blk(lr, c) * h_re - blk(li, c) * h_im + s_re
            new[nh + c] = blk(lr, c) * h_im + blk(li, c) * h_re + s_im
        return tuple(new)

    h0 = h0_ref[0, 0]
    hf = lax.fori_loop(0, nk, step, tuple(blk(h0, c) for c in range(2 * nh)), unroll=math.gcd(nk, 4))
    hf_ref[0, 0] = jnp.concatenate(hf, axis=1)

    def phase3(i, _):
        r0 = pl.multiple_of(i * rb, rb)
        uc = ucat(r0)
        hprev = jnp.concatenate([hp_scr[c, scan_rows(r0), :] for c in range(2 * nh)], axis=1)
        lhs = jnp.concatenate([hprev.astype(BF16), uc.astype(BF16)], axis=1)
        y = jnp.dot(lhs, wy_ref[0], preferred_element_type=F32) + d_ref[0] * uc
        for t in range(lc):
            y_ref[0, pl.ds(r0 * lc + t, rb, stride=lc), :] = y[:, t * LANES:(t + 1) * LANES]
        return 0

    lax.fori_loop(0, (bs * nk) // rb, phase3, 0)


def _s5(u, h0, weights, nb, t, lc):
    bz, wy, lam_re, lam_im, dvec = weights
    nsl = u.shape[0]
    nk = t // lc
    sw = h0.shape[-1]
    bs = nb
    while bs * t > S5_MAX_TOKENS and bs % 2 == 0:
        bs //= 2
    rb = _pick_tile(bs, 256) if nk == 1 else _pick_tile(nk, 256)
    h0 = h0.reshape(nsl, nb // bs, bs, sw)
    slab = lambda j, b: (j, 0, 0)
    tok = lambda j, b: (j, b, 0)
    st = lambda j, b: (j, b, 0, 0)
    y, hf = pl.pallas_call(
        functools.partial(_s5_kernel, nk=nk, bs=bs, lc=lc, rb=rb),
        grid=(nsl, nb // bs),
        in_specs=[pl.BlockSpec((1, bs * t, LANES), tok), pl.BlockSpec((1,) + bz.shape[1:], slab),
                  pl.BlockSpec((1,) + wy.shape[1:], slab), pl.BlockSpec((1, 1, sw // 2), slab),
                  pl.BlockSpec((1, 1, sw // 2), slab), pl.BlockSpec((1, 1, lc * LANES), slab),
                  pl.BlockSpec((1, 1, bs, sw), st)],
        out_specs=[pl.BlockSpec((1, bs * t, LANES), tok), pl.BlockSpec((1, 1, bs, sw), st)],
        out_shape=[jax.ShapeDtypeStruct(u.shape, F32), jax.ShapeDtypeStruct(h0.shape, F32)],
        scratch_shapes=[pltpu.VMEM((sw // LANES, bs * nk, LANES), F32)] * 2,
        compiler_params=_cparams(("parallel", "parallel")), name="s5",
    )(u, bz, wy, lam_re, lam_im, dvec, h0)
    return y, hf.reshape(nsl, nb, sw)


def _ones_lane(h):
    return V_DIM if h % 2 == 0 else 0


def _attn_prompt_kernel(cq_ref, ckv_ref, krp_ref, cos_ref, sinp_ref, sinm_ref, wq_ref, wk_ref, wv_ref,
                        gq_ref, gk_ref, o_ref, k_scr, v_scr, q_scr, *, tq, tk, hb):
    qi = pl.program_id(1)

    @pl.when(qi == 0)
    def _():
        c = ckv_ref[...].astype(BF16)
        krp = krp_ref[...]
        for h in range(N_HEADS):
            kh = jnp.dot(c, wk_ref[h], preferred_element_type=F32) + krp
            k_scr[h] = _rms_qk(kh, gk_ref[...]).astype(BF16)
        vlane = lax.broadcasted_iota(jnp.int32, (1, LANES), 1)
        for h in range(N_HEADS):
            one = jnp.where(vlane == _ones_lane(h), 1.0, 0.0)
            v_scr[h] = (jnp.dot(c, wv_ref[h], preferred_element_type=F32) + one).astype(BF16)

    cq = cq_ref[...]
    cos, sinp, sinm = cos_ref[...], sinp_ref[...], sinm_ref[...]
    sin = sinp - sinm
    for h in range(N_HEADS):
        qq = jnp.dot(cq, wq_ref[h], preferred_element_type=F32)
        q = qq[:, :LANES] * cos + qq[:, LANES:] * sin
        q_scr[h] = (_rms_qk(q, gq_ref[...]) * (LOG2E / math.sqrt(QK))).astype(BF16)

    assert tq == tk
    causal = lax.broadcasted_iota(jnp.int32, (tq, tk), 1) <= lax.broadcasted_iota(jnp.int32, (tq, tk), 0)
    lane = lax.broadcasted_iota(jnp.int32, (tq, LANES), 1)

    def fold(m, acc, q, h, ks, mask):
        s = lax.dot_general(q, k_scr[h, ks, :], NT_DIMS, preferred_element_type=F32)
        if mask is not None:
            s = jnp.where(mask, s, -jnp.inf)
        m_new = jnp.maximum(m, jnp.max(s, axis=-1, keepdims=True))
        p = jnp.exp2(s - m_new)
        acc = acc * jnp.exp2(m - m_new) + jnp.dot(p.astype(BF16), v_scr[h, ks, :], preferred_element_type=F32)
        return m_new, acc

    def kv_step(j, carry, heads, mask):
        ks = pl.ds(pl.multiple_of(j * tk, tk), tk)
        return tuple(fold(m, acc, q_scr[h], h, ks, mask) for (m, acc), h in zip(carry, heads))

    for h0 in range(0, N_HEADS, hb):
        heads = tuple(range(h0, h0 + hb))
        init = (jnp.full((tq, 1), -jnp.inf, F32), jnp.zeros((tq, LANES), F32))
        carry = lax.fori_loop(0, qi, functools.partial(kv_step, heads=heads, mask=None), (init,) * hb)
        carry = kv_step(qi, carry, heads, causal)
        outs = [acc / acc[:, _ones_lane(h):_ones_lane(h) + 1] for (_, acc), h in zip(carry, heads)]
        for k in range(0, hb, 2):
            hp = (h0 + k) // 2
            o_ref[:, hp * LANES:(hp + 1) * LANES] = jnp.where(lane < V_DIM, outs[k], outs[k + 1])


def _attn_prompt(cq, ckv, krp, tables, wq, wk, wvp, gq, gk, nb, t, tq, tk, hb=ATTN_HEADS_PER_BODY):
    n = cq.shape[0]
    nq = t // tq
    qrow = lambda b, i: (b * nq + i, 0)
    seq = lambda b, i: (b, 0)
    tab = pl.BlockSpec((tq, LANES), lambda b, i: (i, 0))
    c3 = lambda b, i: (0, 0, 0)
    c2 = lambda b, i: (0, 0)
    return pl.pallas_call(
        functools.partial(_attn_prompt_kernel, tq=tq, tk=tk, hb=hb),
        grid=(nb, nq),
        in_specs=[pl.BlockSpec((tq, cq.shape[1]), qrow), pl.BlockSpec((t, ckv.shape[1]), seq),
                  pl.BlockSpec((t, LANES), seq), tab, tab, tab,
                  pl.BlockSpec(wq.shape, c3), pl.BlockSpec(wk.shape, c3), pl.BlockSpec(wvp.shape, c3),
                  pl.BlockSpec((1, LANES), c2), pl.BlockSpec((1, LANES), c2)],
        out_specs=pl.BlockSpec((tq, N_HEADS * V_DIM), qrow),
        out_shape=jax.ShapeDtypeStruct((n, N_HEADS * V_DIM), F32),
        scratch_shapes=[pltpu.VMEM((N_HEADS, t, LANES), BF16), pltpu.VMEM((N_HEADS, t, LANES), BF16),
                        pltpu.VMEM((N_HEADS, tq, LANES), BF16)],
        compiler_params=_cparams(("parallel", "arbitrary")), name="attn_prompt",
    )(cq, ckv, krp, *tables, wq, wk, wvp, gq, gk)


def _q_sample_kernel(cq_ref, cos_ref, sinp_ref, sinm_ref, wq_ref, wukt_ref, sel_ref, gq_ref, gk_ref,
                     qa_ref, qr_ref):
    cq = cq_ref[...]
    for h in range(N_HEADS):
        q = _rope(jnp.dot(cq, wq_ref[h], preferred_element_type=F32), cos_ref[...], sinp_ref[...], sinm_ref[...])
        q = _rms_qk(q, gq_ref[...]) * (LOG2E / math.sqrt(QK))
        qk = (q * gk_ref[...]).astype(BF16)
        qa_ref[h] = jnp.dot(qk, wukt_ref[h], preferred_element_type=F32).astype(BF16)
        qr_ref[h] = jnp.dot(qk, sel_ref[...], preferred_element_type=F32).astype(BF16)


def _q_sample(cq, tables, wq, wukt, sel, gq, gk):
    n = cq.shape[0]
    return pl.pallas_call(
        _q_sample_kernel,
        out_shape=[jax.ShapeDtypeStruct((N_HEADS, n, LANES), BF16), jax.ShapeDtypeStruct((N_HEADS, n, ROPE), BF16)],
        compiler_params=pltpu.CompilerParams(vmem_limit_bytes=VMEM_LIMIT), name="q_sample",
    )(cq, *tables, wq, wukt, sel, gq, gk)


def _attn_sample_kernel(pt_ref, qa_ref, qr_ref, qa_prev_ref, qr_prev_ref, cnew_ref, krnew_ref, wukt_ref, wv_ref,
                        ckv_hbm, kr_hbm, o_ref, cbuf, kbuf, sems, s_scr, cb_scr, m_scr, l_scr, acc_scr,
                        *, pc, nc, total, t_dec):
    g = pl.program_id(0)
    slot = g % 2
    rows = N_HEADS * t_dec

    def page_copies(step, slot_):
        base = step * pc
        cps = []
        for j in range(pc):
            pg = pt_ref[base + j]
            cps.append(pltpu.make_async_copy(ckv_hbm.at[pg], cbuf.at[slot_, j], sems.at[0, slot_]))
            cps.append(pltpu.make_async_copy(kr_hbm.at[pg], kbuf.at[slot_, j], sems.at[1, slot_]))
        return cps

    def wait_pages(slot_):
        pltpu.make_async_copy(ckv_hbm.at[pl.ds(0, pc)], cbuf.at[slot_], sems.at[0, slot_]).wait()
        pltpu.make_async_copy(kr_hbm.at[pl.ds(0, pc)], kbuf.at[slot_], sems.at[1, slot_]).wait()

    def reset_softmax():
        m_scr[...] = jnp.full(m_scr.shape, -jnp.inf, F32)
        l_scr[...] = jnp.zeros(l_scr.shape, F32)
        acc_scr[...] = jnp.zeros(acc_scr.shape, F32)

    @pl.when(g == 0)
    def _():
        for cp in page_copies(g, slot):
            cp.start()
        s_scr[1] = jnp.zeros(s_scr.shape[1:], F32)
        cb_scr[1] = jnp.zeros(cb_scr.shape[1:], BF16)
        reset_softmax()

    @pl.when((g >= 1) & ((g - 1) % nc == 0))
    def _():
        reset_softmax()

    def scores(qa, qr, cb, krt):
        nk = cb.shape[0]
        knt = lax.dot_general(wukt_ref[...], cb, NT_DIMS, preferred_element_type=F32)
        n2 = jnp.sum((knt * knt).reshape(N_HEADS, NOPE, nk), axis=1)
        kr2 = jnp.sum(krt * krt, axis=0, keepdims=True)
        rinv = lax.rsqrt((n2 + kr2) * (1.0 / QK) + EPS)
        s = lax.dot_general(qa, cb, NT_DIMS, preferred_element_type=F32)
        s += jnp.dot(qr, krt.astype(BF16), preferred_element_type=F32)
        return (s.reshape(N_HEADS, t_dec, nk) * rinv[:, None, :]).reshape(rows, nk)

    def fold(s, cb):
        m = m_scr[...]
        m_new = jnp.maximum(m, jnp.max(s, axis=-1, keepdims=True))
        p = jnp.exp2(s - m_new)
        corr = jnp.exp2(m - m_new)
        l_scr[...] = l_scr[...] * corr + jnp.sum(p, axis=-1, keepdims=True)
        acc_scr[...] = acc_scr[...] * corr + jnp.dot(p.astype(BF16), cb, preferred_element_type=F32)
        m_scr[...] = m_new

    def score_and_fold(cur, prev):
        wait_pages(cur)
        for cp in page_copies(jnp.minimum(g + 1, total - 1), prev):
            cp.start()
        s_prev, cb_prev = s_scr[prev], cb_scr[prev]
        cb = cbuf[cur].reshape(pc * PAGE, cbuf.shape[-1]).astype(BF16)
        krt = jnp.concatenate([kbuf[cur, j] for j in range(pc)], axis=1)
        s_scr[cur] = scores(qa_ref[...].reshape(rows, LANES), qr_ref[...].reshape(rows, ROPE), cb, krt)
        cb_scr[cur] = cb
        fold(s_prev, cb_prev)

    for parity in range(2):
        pl.when(slot == parity)(functools.partial(score_and_fold, parity, 1 - parity))

    @pl.when(g == total)
    def _():
        wait_pages(1 - slot)

    @pl.when((g >= 1) & ((g - 1) % nc == nc - 1))
    def _():
        key = lax.broadcasted_iota(jnp.int32, (rows, PAGE), 1)
        step = lax.broadcasted_iota(jnp.int32, (rows, PAGE), 0) % t_dec
        cb_new = cnew_ref[0].astype(BF16)
        s_new = scores(qa_prev_ref[...].reshape(rows, LANES), qr_prev_ref[...].reshape(rows, ROPE), cb_new, krnew_ref[0])
        fold(jnp.where(key <= step, s_new, -jnp.inf), cb_new)
        o_lat = (acc_scr[...] / l_scr[...]).astype(BF16)
        lane = lax.broadcasted_iota(jnp.int32, (t_dec, LANES), 1)
        for hp in range(N_HEADS // 2):
            lo = jnp.dot(o_lat[(2 * hp) * t_dec:(2 * hp + 1) * t_dec], wv_ref[hp], preferred_element_type=F32)
            hi = jnp.dot(o_lat[(2 * hp + 1) * t_dec:(2 * hp + 2) * t_dec], wv_ref[hp], preferred_element_type=F32)
            o_ref[:, hp * LANES:(hp + 1) * LANES] = jnp.where(lane < V_DIM, lo, hi)


def _attn_sample(page_table, qa, qr, cnew_pad, krnew_pad, wukt_all, wvp, cache_ckv, cache_kr, t_dec, pc):
    nb, npg = page_table.shape
    nc = npg // pc
    total = nb * nc
    kv_rank = cache_ckv.shape[-1]
    nkeys = pc * PAGE
    rows = N_HEADS * t_dec
    seq_cur = lambda g: jnp.minimum(g // nc, nb - 1)
    seq_prev = lambda g: jnp.maximum(g - 1, 0) // nc
    grid_spec = pltpu.PrefetchScalarGridSpec(
        num_scalar_prefetch=1,
        grid=(total + 1,),
        in_specs=[pl.BlockSpec((N_HEADS, t_dec, LANES), lambda g, pt: (0, seq_cur(g), 0)),
                  pl.BlockSpec((N_HEADS, t_dec, ROPE), lambda g, pt: (0, seq_cur(g), 0)),
                  pl.BlockSpec((N_HEADS, t_dec, LANES), lambda g, pt: (0, seq_prev(g), 0)),
                  pl.BlockSpec((N_HEADS, t_dec, ROPE), lambda g, pt: (0, seq_prev(g), 0)),
                  pl.BlockSpec((1, PAGE, kv_rank), lambda g, pt: (seq_prev(g), 0, 0)),
                  pl.BlockSpec((1, ROPE, PAGE), lambda g, pt: (seq_prev(g), 0, 0)),
                  pl.BlockSpec(wukt_all.shape, lambda g, pt: (0, 0)),
                  pl.BlockSpec(wvp.shape, lambda g, pt: (0, 0, 0)),
                  pl.BlockSpec(memory_space=pl.ANY), pl.BlockSpec(memory_space=pl.ANY)],
        out_specs=pl.BlockSpec((t_dec, N_HEADS * V_DIM), lambda g, pt: (seq_prev(g), 0)),
        scratch_shapes=[pltpu.VMEM((2, pc, PAGE, kv_rank), F32), pltpu.VMEM((2, pc, ROPE, PAGE), F32),
                        pltpu.SemaphoreType.DMA((2, 2)),
                        pltpu.VMEM((2, rows, nkeys), F32), pltpu.VMEM((2, nkeys, kv_rank), BF16),
                        pltpu.VMEM((rows, 1), F32), pltpu.VMEM((rows, 1), F32), pltpu.VMEM((rows, kv_rank), F32)],
    )
    return pl.pallas_call(
        functools.partial(_attn_sample_kernel, pc=pc, nc=nc, total=total, t_dec=t_dec),
        grid_spec=grid_spec,
        out_shape=jax.ShapeDtypeStruct((nb * t_dec, N_HEADS * V_DIM), F32),
        compiler_params=_cparams(("arbitrary",)), name="attn_sample",
    )(page_table.reshape(-1), qa, qr, qa, qr, cnew_pad, krnew_pad, wukt_all, wvp, cache_ckv, cache_kr)


def _merge_kernel(xp_ref, xs_ref, yp_ref, ys_ref, ap_ref, as_ref, wglu_ref, bglu_ref, gs_ref, ga_ref,
                  wo1_ref, wo2_ref, g2_ref, wrh_ref, wrl_ref, br_ref, tri_ref,
                  h_ref, xn_ref, meta_ref, cnt_ref, run_scr, *, n_prompt_tiles):
    i = pl.program_id(0)

    @pl.when(i == 0)
    def _():
        run_scr[...] = jnp.zeros(run_scr.shape, F32)

    is_p = i < n_prompt_tiles
    run = run_scr[...]
    for rows in _row_chunks(h_ref.shape[0], tri_ref.shape[0]):
        x = jnp.where(is_p, xp_ref[rows, :], xs_ref[rows, :])
        slabs = lambda ref: jnp.concatenate([ref[j, rows, :] for j in range(ref.shape[0])], axis=1)
        y = jnp.where(is_p, slabs(yp_ref), slabs(ys_ref))
        att = jnp.where(is_p, ap_ref[rows, :], as_ref[rows, :])

        y = jax.nn.gelu(y)
        glu = jnp.dot(y.astype(BF16), wglu_ref[...], preferred_element_type=F32) + bglu_ref[...]
        ssm = y * jax.nn.sigmoid(glu)
        mix = jnp.dot(_rms(ssm, gs_ref[...]).astype(BF16), wo1_ref[...], preferred_element_type=F32)
        mix += jnp.dot(_rms(att, ga_ref[...]).astype(BF16), wo2_ref[...], preferred_element_type=F32)
        h = x + mix
        h_ref[rows, :] = h
        xn = _rms(h, g2_ref[...])
        xn_ref[rows, :] = xn
        xh = xn.astype(BF16)
        xl = (xn - xh.astype(F32)).astype(BF16)
        logits = (jnp.dot(xh, wrh_ref[...], preferred_element_type=F32)
                  + jnp.dot(xl, wrh_ref[...], preferred_element_type=F32)
                  + jnp.dot(xh, wrl_ref[...], preferred_element_type=F32)) + br_ref[...]

        tc = logits.shape[0]
        lane_i = lax.broadcasted_iota(jnp.int32, (tc, LANES), 1)
        lane = lane_i.astype(F32)
        big = float(LANES)
        first = lambda hit: jnp.min(jnp.where(hit, lane, big), axis=-1, keepdims=True)
        gl = jnp.where(lane_i < N_GROUPS, logits, -jnp.inf)
        gmax = jnp.max(gl, axis=-1, keepdims=True)
        grp = first(gl == gmax)
        p_sel = 1.0 / jnp.sum(jnp.exp(gl - gmax), axis=-1, keepdims=True)
        lane_grp = ((lane_i - N_GROUPS) >> 3).astype(F32)
        in_grp = (lane_i >= N_GROUPS) & (lane_i < N_GROUPS + N_EXPERTS) & (lane_grp == grp)
        el = jnp.where(in_grp, logits, -jnp.inf)
        m1 = jnp.max(el, axis=-1, keepdims=True)
        i1 = first(el == m1)
        el2 = jnp.where(lane == i1, -jnp.inf, el)
        m2 = jnp.max(el2, axis=-1, keepdims=True)
        i2 = first(el2 == m2)
        e21 = jnp.exp(m2 - m1)
        w1 = p_sel / (1.0 + e21)
        w2 = p_sel * e21 / (1.0 + e21)
        e1, e2 = i1 - N_GROUPS, i2 - N_GROUPS

        oh1, oh2 = lane == e1, lane == e2
        oh = jnp.where(oh1 | oh2, 1.0, 0.0)
        before = jnp.dot(tri_ref[...], oh.astype(BF16), preferred_element_type=F32) + run
        r1 = jnp.sum(jnp.where(oh1, before, 0.0), axis=-1, keepdims=True)
        r2 = jnp.sum(jnp.where(oh2, before, 0.0), axis=-1, keepdims=True)
        run = run + jnp.sum(oh, axis=0, keepdims=True)
        meta = jnp.zeros((tc, LANES), F32)
        for k, v in enumerate((e1, e2, w1, w2, r1, r2)):
            meta = jnp.where(lane_i == k, v, meta)
        meta_ref[rows, :] = meta

    run_scr[...] = run
    cnt_ref[...] = run


def _merge(x_p, x_s, y_p, y_s, a_p, a_s, wglu, bglu, gs, ga, wo1, wo2, g2, wr, br, tm):
    wrh = wr.astype(BF16)
    wrl = (wr - wrh.astype(F32)).astype(BF16)
    n_p, d = x_p.shape
    n_s = x_s.shape[0]
    npt, nst = n_p // tm, n_s // tm
    n_all = n_p + n_s
    tc = tm
    tri = (lax.broadcasted_iota(jnp.int32, (tc, tc), 0) > lax.broadcasted_iota(jnp.int32, (tc, tc), 1)).astype(BF16)
    prow = lambda i: (jnp.minimum(i, npt - 1), 0)
    srow = lambda i: (jnp.maximum(i - npt, 0), 0)
    row = lambda i: (i, 0)
    const = lambda i: (0, 0)
    w = a_p.shape[1]
    nsl = y_p.shape[0]
    full = lambda a: pl.BlockSpec(a.shape, const)
    return pl.pallas_call(
        functools.partial(_merge_kernel, n_prompt_tiles=npt),
        grid=(npt + nst,),
        in_specs=[pl.BlockSpec((tm, d), prow), pl.BlockSpec((tm, d), srow),
                  pl.BlockSpec((nsl, tm, LANES), lambda i: (0, jnp.minimum(i, npt - 1), 0)),
                  pl.BlockSpec((nsl, tm, LANES), lambda i: (0, jnp.maximum(i - npt, 0), 0)),
                  pl.BlockSpec((tm, w), prow), pl.BlockSpec((tm, w), srow),
                  full(wglu), full(bglu), full(gs), full(ga), full(wo1), full(wo2), full(g2), full(wrh), full(wrl), full(br),
                  full(tri)],
        out_specs=[pl.BlockSpec((tm, d), row), pl.BlockSpec((tm, d), row), pl.BlockSpec((tm, LANES), row),
                   pl.BlockSpec((1, LANES), const)],
        out_shape=[jax.ShapeDtypeStruct((n_all, d), F32), jax.ShapeDtypeStruct((n_all, d), F32),
                   jax.ShapeDtypeStruct((n_all, LANES), F32), jax.ShapeDtypeStruct((1, LANES), F32)],
        scratch_shapes=[pltpu.VMEM((1, LANES), F32)],
        compiler_params=_cparams(("arbitrary",)), name="merge",
    )(x_p, x_s, y_p, y_s, a_p, a_s, wglu, bglu, gs, ga, wo1, wo2, g2, wrh, wrl, br, tri)


def _start_row_gather(idx_ref, base, src_hbm, dst, sem, nrows):
    for r in range(nrows):
        pltpu.make_async_copy(src_hbm.at[pl.ds(idx_ref[base + r], 1)], dst.at[pl.ds(r, 1)], sem).start()


def _wait_row_gather(src_hbm, dst, sem, nrows):
    pltpu.make_async_copy(src_hbm.at[pl.ds(0, nrows)], dst, sem).wait()


def _pos_kernel(meta_ref, offs_ref, p1_ref, p2_ref):
    meta = meta_ref[...]
    tm = meta.shape[0]
    lane_i = lax.broadcasted_iota(jnp.int32, (tm, LANES), 1)
    lane = lane_i.astype(F32)
    diag = lax.broadcasted_iota(jnp.int32, (tm, LANES), 0) % LANES == lane_i

    def dense(e, r):
        pos = jnp.sum(jnp.where(lane == e, offs_ref[...], 0.0), axis=-1, keepdims=True) + r
        spread = jnp.where(diag, pos, 0.0).reshape(tm // LANES, LANES, LANES)
        return jnp.sum(spread, axis=1).astype(jnp.int32)

    p1_ref[...] = dense(meta[:, 0:1], meta[:, 4:5])
    p2_ref[...] = dense(meta[:, 1:2], meta[:, 5:6])


def _positions(meta, offs_vec, tm):
    n = meta.shape[0]
    rows = tm // LANES
    return pl.pallas_call(
        _pos_kernel,
        grid=(n // tm,),
        in_specs=[pl.BlockSpec((tm, LANES), lambda i: (i, 0)), pl.BlockSpec((1, LANES), lambda i: (0, 0))],
        out_specs=[pl.BlockSpec((rows, LANES), lambda i: (i, 0)), pl.BlockSpec((rows, LANES), lambda i: (i, 0))],
        out_shape=[jax.ShapeDtypeStruct((n // LANES, LANES), jnp.int32)] * 2,
        compiler_params=_cparams(("parallel",)), name="positions",
    )(meta, offs_vec)


def _zero_padding_tiles(tend_ref, xs_hbm, zbuf, sem, tme):
    zbuf[...] = jnp.zeros(zbuf.shape, zbuf.dtype)

    def last_tile(e, carry, wait):
        end = tend_ref[e]
        begin = jnp.where(e == 0, 0, tend_ref[jnp.maximum(e - 1, 0)])

        @pl.when(end > begin)
        def _():
            cp = pltpu.make_async_copy(zbuf, xs_hbm.at[pl.ds((end - 1) * tme, tme)], sem)
            cp.wait() if wait else cp.start()
        return carry

    def spare_tile(t, carry, wait):
        cp = pltpu.make_async_copy(zbuf, xs_hbm.at[pl.ds(t * tme, tme)], sem)
        cp.wait() if wait else cp.start()
        return carry

    n_used, n_tiles = tend_ref[N_EXPERTS - 1], xs_hbm.shape[0] // tme
    for wait in (False, True):
        lax.fori_loop(0, N_EXPERTS, functools.partial(last_tile, wait=wait), 0)
        lax.fori_loop(n_used, n_tiles, functools.partial(spare_tile, wait=wait), 0)


def _dispatch_kernel(p1_ref, p2_ref, tend_ref, xn_ref, xs_hbm, zbuf, sem, *, tmd, tme):
    base = pl.program_id(0) * tmd

    @pl.when(pl.program_id(0) == 0)
    def _():
        _zero_padding_tiles(tend_ref, xs_hbm, zbuf, sem, tme)

    for r in range(tmd):
        src = xn_ref.at[pl.ds(r, 1)]
        pltpu.make_async_copy(src, xs_hbm.at[pl.ds(p1_ref[base + r], 1)], sem).start()
        pltpu.make_async_copy(src, xs_hbm.at[pl.ds(p2_ref[base + r], 1)], sem).start()
    for _ in range(2):
        pltpu.make_async_copy(xn_ref, xs_hbm.at[pl.ds(0, tmd)], sem).wait()


def _dispatch(pos1, pos2, tile_end, xn, n_rows, tmd, tme):
    n, d = xn.shape
    grid_spec = pltpu.PrefetchScalarGridSpec(
        num_scalar_prefetch=3,
        grid=(n // tmd,),
        in_specs=[pl.BlockSpec((tmd, d), lambda i, p1, p2, te: (i, 0))],
        out_specs=pl.BlockSpec(memory_space=pl.ANY),
        scratch_shapes=[pltpu.VMEM((tme, d), xn.dtype), pltpu.SemaphoreType.DMA],
    )
    return pl.pallas_call(
        functools.partial(_dispatch_kernel, tmd=tmd, tme=tme),
        grid_spec=grid_spec,
        out_shape=jax.ShapeDtypeStruct((n_rows, d), xn.dtype),
        compiler_params=_cparams(("arbitrary",)), name="dispatch",
    )(pos1, pos2, tile_end, xn)


def _experts_kernel(te_ref, nu_ref, x_ref, wg_ref, wu_ref, wd_ref, o_ref, wg_scr, wu_scr, wd_scr):
    i = pl.program_id(0)

    @pl.when((i == 0) | (te_ref[i] != te_ref[jnp.maximum(i - 1, 0)]))
    def _():
        wg_scr[...] = wg_ref[0].astype(BF16)
        wu_scr[...] = wu_ref[0].astype(BF16)
        wd_scr[...] = wd_ref[0].astype(BF16)

    @pl.when(i < nu_ref[0])
    def _():
        x = x_ref[...].astype(BF16)
        hg = jnp.dot(x, wg_scr[...], preferred_element_type=F32)
        hu = jnp.dot(x, wu_scr[...], preferred_element_type=F32)
        hh = (jax.nn.silu(hg) * hu).astype(BF16)
        o_ref[...] = jnp.dot(hh, wd_scr[...], preferred_element_type=F32)

    @pl.when(i >= nu_ref[0])
    def _():
        o_ref[...] = jnp.zeros(o_ref.shape, F32)


def _experts(tile_expert, n_used, xs, w_gate, w_up, w_down, tme):
    nt = tile_expert.shape[0]
    ne, d, de = w_gate.shape
    wmap = lambda i, te, nu: (te[i], 0, 0)
    grid_spec = pltpu.PrefetchScalarGridSpec(
        num_scalar_prefetch=2,
        grid=(nt,),
        in_specs=[pl.BlockSpec((tme, d), lambda i, te, nu: (jnp.minimum(i, nu[0] - 1), 0)),
                  pl.BlockSpec((1, d, de), wmap), pl.BlockSpec((1, d, de), wmap), pl.BlockSpec((1, de, d), wmap)],
        out_specs=pl.BlockSpec((tme, d), lambda i, te, nu: (i, 0)),
        scratch_shapes=[pltpu.VMEM((d, de), BF16), pltpu.VMEM((d, de), BF16), pltpu.VMEM((de, d), BF16)],
    )
    return pl.pallas_call(
        _experts_kernel,
        grid_spec=grid_spec,
        out_shape=jax.ShapeDtypeStruct((nt * tme, d), F32),
        compiler_params=_cparams(("arbitrary",)), name="experts",
    )(tile_expert, n_used, xs, w_gate, w_up, w_down)


def _combine_kernel(p1_ref, p2_ref, h_ref, meta_ref, eo_hbm, y_ref, buf, sems, *, tmc, nsteps):
    i = pl.program_id(0)
    slot = i % 2

    def start(step, slot_):
        _start_row_gather(p1_ref, step * tmc, eo_hbm, buf.at[slot_, pl.ds(0, tmc)], sems.at[slot_], tmc)
        _start_row_gather(p2_ref, step * tmc, eo_hbm, buf.at[slot_, pl.ds(tmc, tmc)], sems.at[slot_], tmc)

    @pl.when(i == 0)
    def _():
        start(0, 0)

    @pl.when(i + 1 < nsteps)
    def _():
        start(i + 1, 1 - slot)

    _wait_row_gather(eo_hbm, buf.at[slot], sems.at[slot], 2 * tmc)
    meta = meta_ref[...]
    y_ref[...] = h_ref[...] + meta[:, 2:3] * buf[slot, :tmc, :] + meta[:, 3:4] * buf[slot, tmc:, :]


def _combine(pos1, pos2, h_all, meta_all, eo, row0, tmc):
    n = pos1.shape[0]
    d = h_all.shape[1]
    nsteps = n // tmc
    off = row0 // tmc
    grid_spec = pltpu.PrefetchScalarGridSpec(
        num_scalar_prefetch=2,
        grid=(nsteps,),
        in_specs=[pl.BlockSpec((tmc, d), lambda i, p1, p2: (i + off, 0)),
                  pl.BlockSpec((tmc, LANES), lambda i, p1, p2: (i + off, 0)),
                  pl.BlockSpec(memory_space=pl.ANY)],
        out_specs=pl.BlockSpec((tmc, d), lambda i, p1, p2: (i, 0)),
        scratch_shapes=[pltpu.VMEM((2, 2 * tmc, d), F32), pltpu.SemaphoreType.DMA((2,))],
    )
    return pl.pallas_call(
        functools.partial(_combine_kernel, tmc=tmc, nsteps=nsteps),
        grid_spec=grid_spec,
        out_shape=jax.ShapeDtypeStruct((n, d), F32),
        compiler_params=_cparams(("arbitrary",)), name="combine",
    )(pos1, pos2, h_all, meta_all, eo)


def _pick_tile(n, pref):
    t = min(pref, n)
    while n % t:
        t //= 2
    return t


def _layer(x_p, x_s, cache_ckv, cache_kr, st_re, st_im, page_table, p):
    nb_p, t_p, d = x_p.shape
    nb_s, t_s, _ = x_s.shape
    n_p, n_s = nb_p * t_p, nb_s * t_s
    past_len = page_table.shape[1] * PAGE
    g = p['ssm_a_re'].shape[0]
    ssm_w = g * SSM_CH
    q_rank = p['w_uq'].shape[0]
    kv_rank = p['w_uk'].shape[0]

    w_in = p['w_in']
    o3 = ssm_w + q_rank + kv_rank
    w_pad = jnp.zeros((d, o3 + LANES), F32).at[:, :o3].set(w_in[:, :o3])
    w_pad = w_pad.at[:, o3 + NOPE:o3 + QK].set(w_in[:, o3:]).astype(BF16)
    row2 = lambda v: v.reshape(1, -1).astype(F32)
    padq = lambda v: jnp.pad(v, (0, LANES - QK)).reshape(1, LANES)
    wq = jnp.pad(p['w_uq'].transpose(1, 0, 2), ((0, 0), (0, 0), (0, LANES - QK))).astype(BF16)
    wk = jnp.pad(p['w_uk'].transpose(1, 0, 2), ((0, 0), (0, 0), (0, LANES - NOPE))).astype(BF16)
    wvp = p['w_uv'].reshape(kv_rank, N_HEADS // 2, 2 * V_DIM).transpose(1, 0, 2).astype(BF16)
    wv_h = p['w_uv'].transpose(1, 0, 2)
    wv8 = jnp.stack([jnp.pad(wv_h[h], ((0, 0), (V_DIM - _ones_lane(h), _ones_lane(h)))) for h in range(N_HEADS)])
    wv8 = wv8.astype(BF16)
    wukt_all = p['w_uk'].reshape(kv_rank, N_HEADS * NOPE).T.astype(BF16)
    wukt = jnp.pad(p['w_uk'].transpose(1, 2, 0), ((0, 0), (0, LANES - NOPE), (0, 0))).astype(BF16)
    sel = (jnp.arange(LANES)[:, None] == NOPE + jnp.arange(ROPE)[None, :]).astype(BF16)
    gq, gk = padq(p['qk_norm_q']), padq(p['qk_norm_k'])
    ssm_params = (p['ssm_a_re'], p['ssm_a_im'], p['ssm_log_dt'], p['ssm_b_re'], p['ssm_b_im'],
                  p['ssm_c_re'], p['ssm_c_im'], p['ssm_d'])

    tm_p = _pick_tile(t_p, 512)
    tab_p = _rope_tables(jnp.arange(t_p))
    u, cq, ckv_p, krp_p = _in_proj(x_p.reshape(n_p, d), row2(p['norm1_g']), w_pad, row2(p['q_norm_g']),
                                   row2(p['kv_norm_g']), tab_p, tm_p, ssm_w, q_rank, kv_rank)
    lc_p = _pick_tile(t_p, 8)
    nsl = ssm_w // LANES
    s5w_p = _s5_weights(*ssm_params, lc_p)
    y_ssm_p, hf_p = _s5(u, jnp.zeros((nsl, nb_p, 2 * GPB * SSM_P), F32), s5w_p, nb_p, t_p, lc_p)
    tq = _pick_tile(t_p, 512)
    h1, h2 = NOPE + ROPE // 2, QK
    wq_rot = jnp.concatenate([jnp.zeros_like(wq[..., :NOPE]), -wq[..., h1:h2], wq[..., NOPE:h1],
                              jnp.zeros_like(wq[..., h2:])], axis=-1)
    att_p = _attn_prompt(cq, ckv_p, krp_p, tab_p, jnp.concatenate([wq, wq_rot], axis=-1), wk, wv8, gq, gk,
                         nb_p, t_p, tq, tq)

    tm_s = _pick_tile(n_s, 512)
    pos_s = past_len + jnp.arange(t_s)
    tab_s = tuple(jnp.tile(a, (tm_s // t_s, 1)) for a in _rope_tables(pos_s))
    u, cq, ckv_s, krp_s = _in_proj(x_s.reshape(n_s, d), row2(p['norm1_g']), w_pad, row2(p['q_norm_g']),
                                   row2(p['kv_norm_g']), tab_s, tm_s, ssm_w, q_rank, kv_rank)
    slab_state = lambda s: s.astype(F32).reshape(nb_s, nsl, GPB * SSM_P).transpose(1, 0, 2)
    h0 = jnp.concatenate([slab_state(st_re), slab_state(st_im)], axis=-1)
    s5w_s = s5w_p if t_s == lc_p else _s5_weights(*ssm_params, t_s)
    y_ssm_s, hf_s = _s5(u, h0, s5w_s, nb_s, t_s, t_s)
    tab_q = tuple(jnp.tile(a, (n_s // tm_s, 1)) for a in tab_s)
    qa, qr = _q_sample(cq, tab_q, wq, wukt, sel, gq, gk)
    kr_s = krp_s[:, NOPE:QK]
    cnew = jnp.pad(ckv_s.reshape(nb_s, t_s, kv_rank), ((0, 0), (0, PAGE - t_s), (0, 0)))
    krnew = jnp.pad(kr_s.reshape(nb_s, t_s, ROPE), ((0, 0), (0, PAGE - t_s), (0, 0))).swapaxes(1, 2)
    pc = _pick_tile(page_table.shape[1], 64)
    att_s = _attn_sample(page_table, qa, qr, cnew, krnew, wukt_all, wvp, cache_ckv, jnp.swapaxes(cache_kr, 1, 2),
                         t_s, pc)

    tm = _pick_tile(math.gcd(n_p, n_s), 512)
    w_out = p['w_out'].astype(BF16)
    wr = jnp.zeros((d, LANES), F32).at[:, :N_GROUPS].set(p['w_router_group'])
    wr = wr.at[:, N_GROUPS:N_GROUPS + N_EXPERTS].set(p['w_router_expert'])
    br = jnp.zeros((1, LANES), F32).at[0, :N_GROUPS].set(p['b_router_group'])
    br = br.at[0, N_GROUPS:N_GROUPS + N_EXPERTS].set(p['b_router_expert'])
    h_all, xn_all, meta, cnt = _merge(
        x_p.reshape(n_p, d), x_s.reshape(n_s, d), y_ssm_p, y_ssm_s, att_p, att_s,
        p['ssm_w_glu'].astype(BF16), row2(p['ssm_b_glu']), row2(p['out_norm_ssm']), row2(p['out_norm_attn']),
        w_out[:ssm_w], w_out[ssm_w:], row2(p['norm2_g']), wr, br, tm)

    n_all = n_p + n_s
    tme = 512
    nt = (2 * n_all) // tme + N_EXPERTS
    counts = cnt[0, :N_EXPERTS].astype(jnp.int32)
    tiles_per = (counts + tme - 1) // tme
    tile_end = jnp.cumsum(tiles_per)
    tile_start = tile_end - tiles_per
    tile_id = jnp.arange(nt, dtype=jnp.int32)
    tile_expert = jnp.minimum(jnp.sum(tile_end[None, :] <= tile_id[:, None], axis=1), N_EXPERTS - 1).astype(jnp.int32)
    n_used = tile_end[-1:].astype(jnp.int32)
    offs_vec = jnp.zeros((1, LANES), F32).at[0, :N_EXPERTS].set((tile_start * tme).astype(F32))
    tmd = _pick_tile(n_all, 1024)
    pos1, pos2 = (a.reshape(-1) for a in _positions(meta, offs_vec, tmd))
    xs = _dispatch(pos1, pos2, tile_end.astype(jnp.int32), xn_all, nt * tme, tmd, tme)
    eo = _experts(tile_expert, n_used, xs, p['w_gate'], p['w_up'], p['w_down'], tme)

    tmc = _pick_tile(math.gcd(n_p, n_s), 512)
    y_p = _combine(pos1[:n_p], pos2[:n_p], h_all, meta, eo, 0, tmc)
    y_s = _combine(pos1[n_p:], pos2[n_p:], h_all, meta, eo, n_p, tmc)

    n_pg = t_p // PAGE

    def split(hf):
        nb = hf.shape[1]
        half = GPB * SSM_P
        unslab = lambda a: a.reshape(nsl, nb, GPB, SSM_P).transpose(1, 0, 2, 3).reshape(nb, g, SSM_P)
        return unslab(hf[..., :half]), unslab(hf[..., half:])

    hp_re, hp_im = split(hf_p)
    hs_re, hs_im = split(hf_s)
    return (y_p.reshape(nb_p, t_p, d), y_s.reshape(nb_s, t_s, d),
            ckv_p.reshape(nb_p, n_pg, PAGE, kv_rank), krp_p[:, NOPE:QK].reshape(nb_p, n_pg, PAGE, ROPE),
            hp_re, hp_im, ckv_s.reshape(nb_s, t_s, kv_rank), kr_s.reshape(nb_s, t_s, ROPE), hs_re, hs_im)


_PARAM_NAMES = ('norm1_g', 'w_in', 'ssm_a_re', 'ssm_a_im', 'ssm_log_dt', 'ssm_b_re', 'ssm_b_im', 'ssm_c_re',
                'ssm_c_im', 'ssm_d', 'ssm_w_glu', 'ssm_b_glu', 'q_norm_g', 'w_uq', 'kv_norm_g', 'w_uk', 'w_uv',
                'qk_norm_q', 'qk_norm_k', 'out_norm_ssm', 'out_norm_attn', 'w_out', 'norm2_g', 'w_router_group',
                'b_router_group', 'w_router_expert', 'b_router_expert', 'w_gate', 'w_up', 'w_down')


def kernel(x_prompt, x_sample, cache_ckv, cache_krope, state_ssm_re, state_ssm_im, page_table, norm1_g, w_in, ssm_a_re, ssm_a_im, ssm_log_dt, ssm_b_re, ssm_b_im, ssm_c_re, ssm_c_im, ssm_d, ssm_w_glu, ssm_b_glu, q_norm_g, w_uq, kv_norm_g, w_uk, w_uv, qk_norm_q, qk_norm_k, out_norm_ssm, out_norm_attn, w_out, norm2_g, w_router_group, b_router_group, w_router_expert, b_router_expert, w_gate, w_up, w_down):
    params = (norm1_g, w_in, ssm_a_re, ssm_a_im, ssm_log_dt, ssm_b_re, ssm_b_im, ssm_c_re, ssm_c_im, ssm_d,
              ssm_w_glu, ssm_b_glu, q_norm_g, w_uq, kv_norm_g, w_uk, w_uv, qk_norm_q, qk_norm_k, out_norm_ssm,
              out_norm_attn, w_out, norm2_g, w_router_group, b_router_group, w_router_expert, b_router_expert,
              w_gate, w_up, w_down)
    depth = w_in.shape[0]
    h_p, h_s = x_prompt, x_sample
    outs = [[] for _ in range(8)]
    for layer in range(depth):
        p = {k: v[layer] for k, v in zip(_PARAM_NAMES, params)}
        res = _layer(h_p, h_s, cache_ckv[layer], cache_krope[layer], state_ssm_re[layer], state_ssm_im[layer],
                     page_table, p)
        h_p, h_s = res[0], res[1]
        for acc, r in zip(outs, res[2:]):
            acc.append(r)
    return (h_p, h_s) + tuple(jnp.stack(o) for o in outs)
```
